```python
import math
import jax
import jax.numpy as jnp
from jax import lax
import numpy as np

D_MODEL = 1024
BATCH = 8
SEQ = 2048
DEPTH = 2

GRID_W = 64
CTX_LEN = 256

ATTN_QK_DIM = 64
ATTN_V_DIM = 2 * ATTN_QK_DIM
ATTN_WIDTH = D_MODEL // 2
ATTN_HEADS = ATTN_WIDTH // ATTN_V_DIM
QK_WIDTH = ATTN_HEADS * ATTN_QK_DIM
FOURIER_WIDTH = D_MODEL // 4
FOURIER_GROUPS = 4
FOURIER_GROUP_DIM = FOURIER_WIDTH // FOURIER_GROUPS
CONV_WIDTH = D_MODEL // 4
CONV_GROUPS = 4
CONV_GROUP_DIM = CONV_WIDTH // CONV_GROUPS
CONV_K = 31
MIX_WIDTH = ATTN_WIDTH + FOURIER_WIDTH + CONV_WIDTH
IN_WIDTH = 4 * QK_WIDTH + ATTN_WIDTH + FOURIER_WIDTH + 2 * CONV_WIDTH
D_FF = -(-8 * D_MODEL // (3 * 256)) * 256
Q_BLOCK = 128
ROPE_BASE = 10000.0
EPS = 1e-6

kernel_name = "hybrid_diffattn_fnet_conformer_dit"


def rms_norm(x, g):
    xf = x.astype(jnp.float32)
    y = xf * lax.rsqrt(jnp.mean(xf * xf, axis=-1, keepdims=True) + EPS)
    return (y * g.astype(jnp.float32)).astype(x.dtype)


def axial_rope_tables(rows):
    row = jnp.repeat(jnp.arange(rows, dtype=jnp.float32), GRID_W)
    col = jnp.tile(jnp.arange(GRID_W, dtype=jnp.float32), rows)
    n_freq = ATTN_QK_DIM // 4
    inv_freq = ROPE_BASE ** (-jnp.arange(n_freq, dtype=jnp.float32) / n_freq)
    ang = jnp.concatenate([row[:, None] * inv_freq, col[:, None] * inv_freq], axis=-1)
    return jnp.cos(ang), jnp.sin(ang)


def apply_axial_rope(x, cos, sin):
    b, n, h, d = x.shape
    xr = x.reshape(b, n, h, 2, 2, d // 4)
    x1, x2 = xr[..., 0, :], xr[..., 1, :]
    cs = cos.reshape(1, n, 1, 2, d // 4).astype(x.dtype)
    sn = sin.reshape(1, n, 1, 2, d // 4).astype(x.dtype)
    out = jnp.stack([x1 * cs - x2 * sn, x2 * cs + x1 * sn], axis=-2)
    return out.reshape(b, n, h, d)


def diff_attention(q1, q2, k1, k2, v, lam):
    scale = ATTN_QK_DIM ** -0.5
    s1 = jnp.einsum('bhqd,bhkd->bhqk', q1, k1, preferred_element_type=jnp.float32) * scale
    s2 = jnp.einsum('bhqd,bhkd->bhqk', q2, k2, preferred_element_type=jnp.float32) * scale
    p = jax.nn.softmax(s1, axis=-1) - lam * jax.nn.softmax(s2, axis=-1)
    return jnp.einsum('bhqk,bhkv->bhqv', p.astype(v.dtype), v)


def blocked_diff_attention(q1, q2, k1, k2, v, lam):
    b, h, n, dk = q1.shape
    nblk = n // Q_BLOCK
    def to_blocks(q):
        return q.reshape(b, h, nblk, Q_BLOCK, dk).transpose(2, 0, 1, 3, 4)
    out = lax.map(lambda qs: diff_attention(qs[0], qs[1], k1, k2, v, lam), (to_blocks(q1), to_blocks(q2)))
    return out.transpose(1, 2, 0, 3, 4).reshape(b, h, n, ATTN_V_DIM)


def fourier_mix(u, w_f):
    b, n, _ = u.shape
    ug = u.astype(jnp.float32).reshape(b, n, FOURIER_GROUPS, FOURIER_GROUP_DIM)
    f = jnp.fft.fftn(ug, axes=(1, 3), norm='ortho').real.astype(u.dtype)
    y = jnp.einsum('bngc,gcd->bngd', f, w_f)
    return y.reshape(b, n, FOURIER_WIDTH)


def conformer_conv(u, conv_w, conv_b, ln_g, ln_b, w_pw2):
    b, n, _ = u.shape
    a, gate = jnp.split(u, 2, axis=-1)
    z = a * jax.nn.sigmoid(gate)
    z = lax.conv_general_dilated(z, conv_w[:, None, :].astype(z.dtype), window_strides=(1,),
                                 padding=[(CONV_K // 2, CONV_K // 2)],
                                 dimension_numbers=('NWC', 'WIO', 'NWC'),
                                 feature_group_count=CONV_WIDTH) + conv_b
    zf = z.astype(jnp.float32).reshape(b, n, CONV_GROUPS, CONV_GROUP_DIM)
    mu = jnp.mean(zf, axis=-1, keepdims=True)
    var = jnp.mean(jnp.square(zf - mu), axis=-1, keepdims=True)
    zn = ((zf - mu) * lax.rsqrt(var + EPS)).reshape(b, n, CONV_WIDTH)
    zn = zn * ln_g.astype(jnp.float32) + ln_b.astype(jnp.float32)
    return jax.nn.silu(zn).astype(u.dtype) @ w_pw2


def hybrid_mixer(h_lat, h_ctx, cos, sin, w_in, lq1, lk1, lq2, lk2, subln_g, w_fourier,
                 conv_w, conv_b, ln_g, ln_b, w_conv_out, w_out, layer_idx, need_ctx_out):
    lam_init = 0.8 - 0.6 * math.exp(-0.3 * layer_idx)
    lam = (jnp.exp(jnp.sum(lq1.astype(jnp.float32) * lk1.astype(jnp.float32)))
           - jnp.exp(jnp.sum(lq2.astype(jnp.float32) * lk2.astype(jnp.float32))) + lam_init)
    split_at = (QK_WIDTH, 2 * QK_WIDTH, 3 * QK_WIDTH, 4 * QK_WIDTH, 4 * QK_WIDTH + ATTN_WIDTH,
                4 * QK_WIDTH + ATTN_WIDTH + FOURIER_WIDTH)

    def project(h):
        return jnp.split(h @ w_in, split_at, axis=-1)

    def heads(t, dh):
        b, n, _ = t.shape
        return t.reshape(b, n, ATTN_HEADS, dh)

    def plain(t, dh):
        return heads(t, dh).transpose(0, 2, 1, 3)

    def rope(t):
        return apply_axial_rope(heads(t, ATTN_QK_DIM), cos, sin).transpose(0, 2, 1, 3)

    q1l, q2l, k1l, k2l, vl, ufl, ucl = project(h_lat)
    q1c, q2c, k1c, k2c, vc, ufc, ucc = project(h_ctx)
    k1_ctx, k2_ctx, v_ctx = plain(k1c, ATTN_QK_DIM), plain(k2c, ATTN_QK_DIM), plain(vc, ATTN_V_DIM)
    k1_all = jnp.concatenate([k1_ctx, rope(k1l)], axis=2)
    k2_all = jnp.concatenate([k2_ctx, rope(k2l)], axis=2)
    v_all = jnp.concatenate([v_ctx, plain(vl, ATTN_V_DIM)], axis=2)
    o_lat = blocked_diff_attention(rope(q1l), rope(q2l), k1_all, k2_all, v_all, lam)

    def finish(o_attn, uf, uc):
        o = rms_norm(o_attn, subln_g) * (1.0 - lam_init)
        b, h, n, dv = o.shape
        o = o.transpose(0, 2, 1, 3).reshape(b, n, h * dv)
        yf = fourier_mix(uf, w_fourier)
        yc = conformer_conv(uc, conv_w, conv_b, ln_g, ln_b, w_conv_out)
        return jnp.concatenate([o, yf, yc], axis=-1) @ w_out

    y_lat = finish(o_lat, ufl, ucl)
    y_ctx = None
    if need_ctx_out:
        o_ctx = diff_attention(plain(q1c, ATTN_QK_DIM), plain(q2c, ATTN_QK_DIM), k1_ctx, k2_ctx, v_ctx, lam)
        y_ctx = finish(o_ctx, ufc, ucc)
    return y_lat, y_ctx


def swiglu(h, w1, w3, w2):
    return (jax.nn.silu(h @ w1) * (h @ w3)) @ w2


def setup_inputs(seed: int = 0) -> dict:
    key = jax.random.key(seed)
    ks = jax.random.split(key, 25)
    f32 = jnp.float32
    D = D_MODEL

    def nrm(k, shape, s):
        return jax.random.normal(k, shape, f32) * s

    return {
        'x': nrm(ks[0], (BATCH, SEQ, D), 1.0),
        'c': nrm(ks[1], (BATCH, D), 1.0),
        'ctx': nrm(ks[2], (BATCH, CTX_LEN, D), 1.0),
        'c_ctx': nrm(ks[3], (D,), 1.0),
        'w_ada': nrm(ks[4], (DEPTH, D, 6 * D), 0.5 * D ** -0.5),
        'b_ada': nrm(ks[5], (DEPTH, 6 * D), 0.02),
        'norm1_g': 1.0 + nrm(ks[6], (DEPTH, D), 0.05),
        'norm2_g': 1.0 + nrm(ks[7], (DEPTH, D), 0.05),
        'w_in': nrm(ks[8], (DEPTH, D, IN_WIDTH), D ** -0.5),
        'lam_q1': nrm(ks[9], (DEPTH, ATTN_QK_DIM), 0.1),
        'lam_k1': nrm(ks[10], (DEPTH, ATTN_QK_DIM), 0.1),
        'lam_q2': nrm(ks[11], (DEPTH, ATTN_QK_DIM), 0.1),
        'lam_k2': nrm(ks[12], (DEPTH, ATTN_QK_DIM), 0.1),
        'subln_g': 1.0 + nrm(ks[13], (DEPTH, ATTN_V_DIM), 0.05),
        'w_fourier': nrm(ks[14], (DEPTH, FOURIER_GROUPS, FOURIER_GROUP_DIM, FOURIER_GROUP_DIM), FOURIER_GROUP_DIM ** -0.5),
        'conv_w': nrm(ks[15], (DEPTH, CONV_K, CONV_WIDTH), CONV_K ** -0.5),
        'conv_b': nrm(ks[16], (DEPTH, CONV_WIDTH), 0.02),
        'conv_ln_g': 1.0 + nrm(ks[17], (DEPTH, CONV_WIDTH), 0.05),
        'conv_ln_b': nrm(ks[18], (DEPTH, CONV_WIDTH), 0.02),
        'w_conv_out': nrm(ks[19], (DEPTH, CONV_WIDTH, CONV_WIDTH), CONV_WIDTH ** -0.5),
        'w_out': nrm(ks[20], (DEPTH, MIX_WIDTH, D), MIX_WIDTH ** -0.5),
        'w_ffn1': nrm(ks[21], (DEPTH, D, D_FF), D ** -0.5),
        'w_ffn3': nrm(ks[22], (DEPTH, D, D_FF), D ** -0.5),
        'w_ffn2': nrm(ks[23], (DEPTH, D_FF, D), D_FF ** -0.5),
        'final_g': 1.0 + nrm(ks[24], (D,), 0.05),
    }


def reference(x, c, ctx, c_ctx, w_ada, b_ada, norm1_g, norm2_g, w_in, lam_q1, lam_k1, lam_q2, lam_k2,
              subln_g, w_fourier, conv_w, conv_b, conv_ln_g, conv_ln_b, w_conv_out, w_out,
              w_ffn1, w_ffn3, w_ffn2, final_g):
    n_lat = x.shape[1]
    rows = n_lat // GRID_W
    cos, sin = axial_rope_tables(rows)
    for l in range(DEPTH):
        last = l == DEPTH - 1
        mod_lat = (jax.nn.silu(c) @ w_ada[l] + b_ada[l])[:, None, :]
        mod_ctx = (jax.nn.silu(c_ctx) @ w_ada[l] + b_ada[l])[None, None, :]
        sh1, sc1, g1, sh2, sc2, g2 = jnp.split(mod_lat, 6, axis=-1)
        csh1, csc1, cg1, csh2, csc2, cg2 = jnp.split(mod_ctx, 6, axis=-1)
        h_lat = rms_norm(x, norm1_g[l]) * (1.0 + sc1) + sh1
        h_ctx = rms_norm(ctx, norm1_g[l]) * (1.0 + csc1) + csh1
        y_lat, y_ctx = hybrid_mixer(h_lat, h_ctx, cos, sin, w_in[l], lam_q1[l], lam_k1[l], lam_q2[l], lam_k2[l],
                                    subln_g[l], w_fourier[l], conv_w[l], conv_b[l], conv_ln_g[l], conv_ln_b[l],
                                    w_conv_out[l], w_out[l], l, not last)
        x = x + g1 * y_lat
        h_lat = rms_norm(x, norm2_g[l]) * (1.0 + sc2) + sh2
        x = x + g2 * swiglu(h_lat, w_ffn1[l], w_ffn3[l], w_ffn2[l])
        if not last:
            ctx = ctx + cg1 * y_ctx
            h_ctx = rms_norm(ctx, norm2_g[l]) * (1.0 + csc2) + csh2
            ctx = ctx + cg2 * swiglu(h_ctx, w_ffn1[l], w_ffn3[l], w_ffn2[l])
    return rms_norm(x, final_g)
```

```python
import functools
import math

import numpy as np
import jax
import jax.numpy as jnp
from jax import lax
from jax.experimental import pallas as pl
from jax.experimental.pallas import tpu as pltpu

F32 = jnp.float32
BF16 = jnp.bfloat16

GRID_W = 64
HEADS = 4
QK_DIM = 64
V_DIM = 2 * QK_DIM
FOURIER_GROUPS = 4
CONV_GROUPS = 4
CONV_K = 31
ROPE_BASE = 10000.0
EPS = 1e-6

LANES = 128
SUBLANES = 8
TM = 256
CONV_HALO = 16
CONV_CHUNK = 128
ADA_TN = 1024
MIB = 1024 * 1024


def _params(sem, vmem_mib):
    return pltpu.CompilerParams(dimension_semantics=sem, vmem_limit_bytes=vmem_mib * MIB)


def _resident(shape):
    zeros = (0,) * len(shape)
    return pl.BlockSpec(shape, lambda *_: zeros, pipeline_mode=pl.Buffered(1))


def _sigmoid(x):
    return 1.0 / (1.0 + jnp.exp(-x))


def _rms(x):
    return x * lax.rsqrt(jnp.mean(x * x, axis=-1, keepdims=True) + EPS)


def _dot(a, b):
    return jnp.dot(a, b, preferred_element_type=F32)


def _ada_kernel(c_ref, w_ref, b_ref, o_ref):
    c = c_ref[...]
    s = (c * _sigmoid(c)).astype(BF16)
    o_ref[...] = _dot(s, w_ref[...].astype(BF16)) + b_ref[...]


def _ada(c_rows, w_ada, b_ada):
    depth, d, n = w_ada.shape
    rows = c_rows.shape[0]
    return pl.pallas_call(
        _ada_kernel,
        grid=(depth, n // ADA_TN),
        in_specs=[
            pl.BlockSpec((rows, d), lambda l, j: (0, 0)),
            pl.BlockSpec((None, d, ADA_TN), lambda l, j: (l, 0, j)),
            pl.BlockSpec((None, 1, ADA_TN), lambda l, j: (l, 0, j)),
        ],
        out_specs=pl.BlockSpec((None, rows, ADA_TN), lambda l, j: (l, 0, j)),
        out_shape=jax.ShapeDtypeStruct((depth, rows, n), F32),
        compiler_params=_params(("arbitrary", "arbitrary"), 24),
        name="ada",
    )(c_rows, w_ada, b_ada.reshape(depth, 1, n))


def _inproj_kernel(x_ref, mod_ref, g_ref, w_ref, cos_ref, sin_ref,
                   q_ref, k_ref, v_ref, uf_ref, uc_ref):
    hw = HEADS * LANES
    h = _rms(x_ref[...]) * g_ref[...] * (1.0 + mod_ref[1:2, :]) + mod_ref[0:1, :]
    r = _dot(h.astype(BF16), w_ref[...])
    cos = cos_ref[...]
    sin = sin_ref[...]
    lane = lax.broadcasted_iota(jnp.int32, cos.shape, 1)
    first_half = (lane & (QK_DIM // 4)) == 0

    def rope(t):
        partner = jnp.where(first_half, pltpu.roll(t, LANES - QK_DIM // 4, 1), pltpu.roll(t, QK_DIM // 4, 1))
        return t * cos + partner * sin

    for j in range(HEADS):
        q_ref[j] = (rope(r[:, j * LANES:(j + 1) * LANES]) * (QK_DIM ** -0.5)).astype(BF16)
        k_ref[j] = rope(r[:, hw + j * LANES:hw + (j + 1) * LANES]).astype(BF16)
        v_ref[j] = r[:, 2 * hw + j * LANES:2 * hw + (j + 1) * LANES].astype(BF16)
    uf_ref[...] = r[:, 3 * hw:3 * hw + uf_ref.shape[-1]].astype(BF16)
    uc_ref[...] = r[:, 3 * hw + uf_ref.shape[-1]:]


def _inproj(xs, modsel, g, w, cos_t, sin_t, n_ctx_tiles, fw, cw2):
    b, nt, d = xs.shape
    t = nt // TM
    hw = HEADS * LANES
    head_spec = pl.BlockSpec((None, HEADS, TM, LANES), lambda i, j: (i, 0, j, 0))
    head_shape = jax.ShapeDtypeStruct((b, HEADS, nt, LANES), BF16)
    return pl.pallas_call(
        _inproj_kernel,
        grid=(b, t),
        in_specs=[
            pl.BlockSpec((None, TM, d), lambda i, j: (i, j, 0)),
            pl.BlockSpec((None, None, 6, d), lambda i, j: (i, jnp.where(j < n_ctx_tiles, 0, 1), 0, 0)),
            _resident((1, d)),
            _resident((d, 3 * hw + fw + cw2)),
            pl.BlockSpec((TM, LANES), lambda i, j: (j, 0)),
            pl.BlockSpec((TM, LANES), lambda i, j: (j, 0)),
        ],
        out_specs=[
            head_spec, head_spec, head_spec,
            pl.BlockSpec((None, TM, fw), lambda i, j: (i, j, 0)),
            pl.BlockSpec((None, TM, cw2), lambda i, j: (i, j, 0)),
        ],
        out_shape=[
            head_shape, head_shape, head_shape,
            jax.ShapeDtypeStruct((b, nt, fw), BF16),
            jax.ShapeDtypeStruct((b, nt, cw2), F32),
        ],
        compiler_params=_params(("arbitrary", "arbitrary"), 40),
        name="inproj",
    )(xs, modsel, g, w, cos_t, sin_t)


def _attn_kernel(lam_ref, g_ref, q_ref, k_ref, v_ref, o_ref, *, lam_init, n_ctx, ctx_tiles):
    lq = lam_ref[...]
    lam = (jnp.exp(jnp.sum(lq[0:1] * lq[1:2], axis=-1, keepdims=True))
           - jnp.exp(jnp.sum(lq[2:3] * lq[3:4], axis=-1, keepdims=True)) + lam_init)

    def attend(nk):
        q = q_ref[...]
        k = k_ref[0:nk, :]
        lane = lax.broadcasted_iota(jnp.int32, q.shape, 1)
        zero = jnp.zeros_like(q)
        nt_dims = (((1,), (1,)), ((), ()))
        s1 = lax.dot_general(jnp.where(lane < QK_DIM, q, zero), k, nt_dims, preferred_element_type=F32)
        s2 = lax.dot_general(jnp.where(lane >= QK_DIM, q, zero), k, nt_dims, preferred_element_type=F32)
        e1 = jnp.exp(s1 - jnp.max(s1, axis=-1, keepdims=True))
        e2 = jnp.exp(s2 - jnp.max(s2, axis=-1, keepdims=True))
        r1 = 1.0 / jnp.sum(e1, axis=-1, keepdims=True)
        r2 = lam / jnp.sum(e2, axis=-1, keepdims=True)
        p = (e1 * r1 - e2 * r2).astype(BF16)
        o = _dot(p, v_ref[0:nk, :])
        o_ref[...] = (_rms(o) * g_ref[...] * (1.0 - lam_init)).astype(BF16)

    if ctx_tiles:
        is_ctx = pl.program_id(2) < ctx_tiles
        pl.when(is_ctx)(lambda: attend(n_ctx))
        pl.when(jnp.logical_not(is_ctx))(lambda: attend(k_ref.shape[0]))
    else:
        attend(k_ref.shape[0])


def _attention(q, k, v, lamv, g, lam_init, n_ctx, with_ctx):
    b, h, nt, _ = q.shape
    ctx_tiles = n_ctx // TM
    skip = 0 if with_ctx else ctx_tiles
    kv_spec = pl.BlockSpec((None, None, nt, LANES), lambda i, j, t: (i, j, 0, 0))
    return pl.pallas_call(
        functools.partial(_attn_kernel, lam_init=lam_init, n_ctx=n_ctx, ctx_tiles=ctx_tiles if with_ctx else 0),
        grid=(b, h, nt // TM - skip),
        in_specs=[
            _resident(lamv.shape),
            _resident(g.shape),
            pl.BlockSpec((None, None, TM, LANES), lambda i, j, t: (i, j, t + skip, 0)),
            kv_spec, kv_spec,
        ],
        out_specs=pl.BlockSpec((None, TM, V_DIM), lambda i, j, t: (i, t + skip, j)),
        out_shape=jax.ShapeDtypeStruct((b, nt, h * V_DIM), BF16),
        compiler_params=_params(("arbitrary", "arbitrary", "arbitrary"), 48),
        name="attention",
    )(lamv, g, q, k, v)


def _fourier_kernel(*refs, segments):
    u_ref, chan_c_ref, chan_s_ref, wf_ref = refs[:4]
    pos_refs = refs[4:-1]
    y_ref = refs[-1]
    group_dim = u_ref.shape[-1] // FOURIER_GROUPS
    for i, (start, n) in enumerate(segments):
        u = u_ref[start:start + n, :]
        a = _dot(pos_refs[2 * i][...], u).astype(BF16)
        b = _dot(pos_refs[2 * i + 1][...], u).astype(BF16)
        f = (_dot(a, chan_c_ref[...]) - _dot(b, chan_s_ref[...])) * ((n * group_dim) ** -0.5)
        y_ref[start:start + n, :] = _dot(f.astype(BF16), wf_ref[...]).astype(BF16)


def _fourier(uf, chan_c, chan_s, wf_bd, pos_tables, segments):
    b, nt, fw = uf.shape
    row_spec = pl.BlockSpec((None, nt, fw), lambda i: (i, 0, 0))
    return pl.pallas_call(
        functools.partial(_fourier_kernel, segments=segments),
        grid=(b,),
        in_specs=[row_spec, _resident(chan_c.shape), _resident(chan_s.shape), _resident(wf_bd.shape)]
                 + [_resident(t.shape) for t in pos_tables],
        out_specs=row_spec,
        out_shape=jax.ShapeDtypeStruct((b, nt, fw), BF16),
        compiler_params=_params(("arbitrary",), 48),
        name="fourier",
    )(uf, chan_c, chan_s, wf_bd, *pos_tables)


def _split_dot(x, m):
    hi = x.astype(BF16)
    lo = (x - hi.astype(F32)).astype(BF16)
    return _dot(hi, m) + _dot(lo, m)


def _conv_kernel(u_ref, w_ref, b_ref, lg_ref, lb_ref, avg_ref, pw_ref, y_ref, zpad_ref, *, segments):
    cw = y_ref.shape[-1]
    halo = jnp.zeros((CONV_HALO, cw), F32)
    for start, n in segments:
        a = u_ref[start:start + n, 0:cw]
        gate = u_ref[start:start + n, cw:2 * cw]
        zpad_ref[0:CONV_HALO, :] = halo
        zpad_ref[CONV_HALO:CONV_HALO + n, :] = a * _sigmoid(gate)
        zpad_ref[CONV_HALO + n:2 * CONV_HALO + n, :] = halo

        def chunk(i, carry, start=start):
            r0 = pl.multiple_of(i * CONV_CHUNK, CONV_CHUNK)
            halves = []
            for lo in range(0, cw, LANES):
                win = zpad_ref[pl.ds(r0, CONV_CHUNK + 2 * CONV_HALO), lo:lo + LANES]
                part = jnp.zeros((CONV_CHUNK, LANES), F32) + b_ref[:, lo:lo + LANES]
                for shift in range(SUBLANES):
                    rolled = win if shift == 0 else pltpu.roll(win, win.shape[0] - shift, 0)
                    for aligned in range(0, 2 * CONV_HALO, SUBLANES):
                        tap = aligned + shift - (CONV_HALO - CONV_K // 2)
                        if 0 <= tap < CONV_K:
                            part = part + rolled[aligned:aligned + CONV_CHUNK] * w_ref[tap:tap + 1, lo:lo + LANES]
                halves.append(part)
            acc = jnp.concatenate(halves, axis=1)
            mu = _split_dot(acc, avg_ref[...])
            dev = acc - mu
            var = _split_dot(dev * dev, avg_ref[...])
            zn = dev * lax.rsqrt(var + EPS) * lg_ref[...] + lb_ref[...]
            act = (zn * _sigmoid(zn)).astype(BF16)
            y_ref[pl.ds(start + r0, CONV_CHUNK), :] = _dot(act, pw_ref[...]).astype(BF16)
            return carry

        lax.fori_loop(0, n // CONV_CHUNK, chunk, 0)


def _conv(uc, conv_w, conv_b, ln_g, ln_b, avg, w_pw, segments):
    b, nt, cw2 = uc.shape
    cw = cw2 // 2
    seg_max = max(n for _, n in segments)
    return pl.pallas_call(
        functools.partial(_conv_kernel, segments=segments),
        grid=(b,),
        in_specs=[
            pl.BlockSpec((None, nt, cw2), lambda i: (i, 0, 0)),
            _resident(conv_w.shape), _resident(conv_b.shape), _resident(ln_g.shape), _resident(ln_b.shape),
            _resident(avg.shape), _resident(w_pw.shape),
        ],
        out_specs=pl.BlockSpec((None, nt, cw), lambda i: (i, 0, 0)),
        out_shape=jax.ShapeDtypeStruct((b, nt, cw), BF16),
        scratch_shapes=[pltpu.VMEM((seg_max + 2 * CONV_HALO, cw), F32)],
        compiler_params=_params(("arbitrary",), 40),
        name="conv",
    )(uc, conv_w, conv_b, ln_g, ln_b, avg, w_pw)


def _ffn_kernel(x_ref, o_ref, yf_ref, yc_ref, mod_ref, g_ref, wout_ref, w1_ref, w3_ref, w2_ref, fg_ref,
                out_ref, *, final):
    aw = o_ref.shape[-1]
    fw = yf_ref.shape[-1]
    y = (_dot(o_ref[...], wout_ref[0:aw, :]) + _dot(yf_ref[...], wout_ref[aw:aw + fw, :])
         + _dot(yc_ref[...], wout_ref[aw + fw:, :]))
    x1 = x_ref[...] + mod_ref[2:3, :] * y
    h = (_rms(x1) * g_ref[...] * (1.0 + mod_ref[4:5, :]) + mod_ref[3:4, :]).astype(BF16)
    a = _dot(h, w1_ref[...])
    gated = (a * _sigmoid(a) * _dot(h, w3_ref[...])).astype(BF16)
    x2 = x1 + mod_ref[5:6, :] * _dot(gated, w2_ref[...])
    if final:
        x2 = _rms(x2) * fg_ref[...]
    out_ref[...] = x2


def _out_ffn(xs, o, yf, yc, modsel, g2, w_out, w1, w3, w2, final_g, n_ctx_tiles, final):
    b, nt, d = xs.shape
    skip = n_ctx_tiles if final else 0
    tiles = nt // TM - skip

    def rows(width):
        return pl.BlockSpec((None, TM, width), lambda i, j: (i, j + skip, 0))

    return pl.pallas_call(
        functools.partial(_ffn_kernel, final=final),
        grid=(b, tiles),
        in_specs=[
            rows(d), rows(o.shape[-1]), rows(yf.shape[-1]), rows(yc.shape[-1]),
            pl.BlockSpec((None, None, 6, d), lambda i, j: (i, jnp.where(j + skip < n_ctx_tiles, 0, 1), 0, 0)),
            _resident(g2.shape), _resident(w_out.shape), _resident(w1.shape), _resident(w3.shape),
            _resident(w2.shape), _resident(final_g.shape),
        ],
        out_specs=pl.BlockSpec((None, TM, d), lambda i, j: (i, j, 0)),
        out_shape=jax.ShapeDtypeStruct((b, tiles * TM, d), F32),
        compiler_params=_params(("arbitrary", "arbitrary"), 56),
        name="out_ffn",
    )(xs, o, yf, yc, modsel, g2, w_out, w1, w3, w2, final_g)


def _rope_tables(n_ctx, n_lat):
    n_freq = QK_DIM // 4
    tok = jnp.arange(n_lat)
    row = (tok // GRID_W).astype(F32)
    col = (tok % GRID_W).astype(F32)
    inv_freq = ROPE_BASE ** (-jnp.arange(n_freq, dtype=F32) / n_freq)
    ang_r = row[:, None] * inv_freq
    ang_c = col[:, None] * inv_freq
    cos = jnp.concatenate([jnp.cos(ang_r)] * 2 + [jnp.cos(ang_c)] * 2, axis=-1)
    sin = jnp.concatenate([-jnp.sin(ang_r), jnp.sin(ang_r), -jnp.sin(ang_c), jnp.sin(ang_c)], axis=-1)
    cos = jnp.concatenate([jnp.ones((n_ctx, QK_DIM), F32), cos], axis=0)
    sin = jnp.concatenate([jnp.zeros((n_ctx, QK_DIM), F32), sin], axis=0)
    reps = LANES // QK_DIM
    return jnp.tile(cos, (1, reps)), jnp.tile(sin, (1, reps))


def _dft_tables(n):
    idx = jnp.arange(n, dtype=jnp.int32)
    ang = ((idx[:, None] * idx[None, :]) % n).astype(F32) * (2.0 * math.pi / n)
    return jnp.cos(ang), jnp.sin(ang)


def _block_diag(blocks):
    g, r, c = blocks.shape
    eye = jnp.eye(g, dtype=blocks.dtype)
    return (eye[:, None, :, None] * blocks[:, :, None, :]).reshape(g * r, g * c)


def _inproj_columns(qk_w, attn_w, fw, cw2):
    cols = []
    for base in (0, 2 * qk_w):
        for h in range(HEADS):
            cols += list(range(base + h * QK_DIM, base + (h + 1) * QK_DIM))
            cols += list(range(base + qk_w + h * QK_DIM, base + qk_w + (h + 1) * QK_DIM))
    cols += list(range(4 * qk_w, 4 * qk_w + attn_w + fw + cw2))
    return np.asarray(cols, dtype=np.int32)


def kernel(x, c, ctx, c_ctx, w_ada, b_ada, norm1_g, norm2_g, w_in, lam_q1, lam_k1, lam_q2, lam_k2, subln_g,
           w_fourier, conv_w, conv_b, conv_ln_g, conv_ln_b, w_conv_out, w_out, w_ffn1, w_ffn3, w_ffn2, final_g):
    b, n_lat, d = x.shape
    n_ctx = ctx.shape[1]
    depth = w_ada.shape[0]
    fw = w_fourier.shape[1] * w_fourier.shape[2]
    cw = conv_w.shape[-1]
    qk_w = HEADS * QK_DIM
    attn_w = HEADS * V_DIM
    assert n_lat % TM == 0 and n_ctx % TM == 0 and n_lat % GRID_W == 0
    assert w_in.shape[-1] == 4 * qk_w + attn_w + fw + 2 * cw
    n_ctx_tiles = n_ctx // TM

    pad = (-(b + 1)) % 8
    c_rows = jnp.concatenate([c, c_ctx[None, :], jnp.zeros((pad, d), c.dtype)], axis=0)
    mod = _ada(c_rows, w_ada, b_ada)

    cos_t, sin_t = _rope_tables(n_ctx, n_lat)
    group_dim = fw // FOURIER_GROUPS
    cc, cs = _dft_tables(group_dim)
    eye = jnp.eye(FOURIER_GROUPS, dtype=F32)
    chan_c = jnp.kron(eye, cc).astype(BF16)
    chan_s = jnp.kron(eye, cs).astype(BF16)
    lat_tables = [t.astype(BF16) for t in _dft_tables(n_lat)]
    ctx_tables = [t.astype(BF16) for t in _dft_tables(n_ctx)]
    avg = jnp.kron(jnp.eye(CONV_GROUPS, dtype=F32),
                   jnp.full((cw // CONV_GROUPS, cw // CONV_GROUPS), CONV_GROUPS / cw, F32)).astype(BF16)
    cols = _inproj_columns(qk_w, attn_w, fw, 2 * cw)

    xs = jnp.concatenate([ctx, x], axis=1)
    lat_seg = (n_ctx, n_lat)
    ctx_seg = (0, n_ctx)

    for l in range(depth):
        last = l == depth - 1
        lam_init = 0.8 - 0.6 * math.exp(-0.3 * l)
        mod_lat = mod[l, :b].reshape(b, 6, d)
        mod_ctx = jnp.broadcast_to(mod[l, b].reshape(1, 6, d), (b, 6, d))
        modsel = jnp.stack([mod_ctx, mod_lat], axis=1)

        w_in_l = w_in[l][:, cols].astype(BF16)
        q, k, v, uf, uc = _inproj(xs, modsel, norm1_g[l][None, :], w_in_l, cos_t, sin_t, n_ctx_tiles, fw, 2 * cw)

        lamv = jnp.stack([lam_q1[l], lam_k1[l], lam_q2[l], lam_k2[l]], axis=0)
        o = _attention(q, k, v, lamv, subln_g[l][None, :], lam_init, n_ctx, with_ctx=not last)

        segments = (lat_seg,) if last else (lat_seg, ctx_seg)
        tables = lat_tables if last else lat_tables + ctx_tables
        yf = _fourier(uf, chan_c, chan_s, _block_diag(w_fourier[l]).astype(BF16), tables, segments)
        yc = _conv(uc, conv_w[l], conv_b[l][None, :], conv_ln_g[l][None, :], conv_ln_b[l][None, :], avg,
                   w_conv_out[l].astype(BF16), segments)

        xs = _out_ffn(xs, o, yf, yc, modsel, norm2_g[l][None, :], w_out[l].astype(BF16),
                      w_ffn1[l].astype(BF16), w_ffn3[l].astype(BF16), w_ffn2[l].astype(BF16),
                      final_g[None, :], n_ctx_tiles, final=last)
    return xs
```

```python
import functools
import math

import numpy as np
import jax
import jax.numpy as jnp
from jax import lax
from jax.experimental import pallas as pl
from jax.experimental.pallas import tpu as pltpu

F32 = jnp.float32
BF16 = jnp.bfloat16

GRID_W = 64
HEADS = 4
QK_DIM = 64
V_DIM = 2 * QK_DIM
FOURIER_GROUPS = 4
CONV_GROUPS = 4
CONV_K = 31
ROPE_BASE = 10000.0
EPS = 1e-6
LOG2_E = math.log2(math.e)

LANES = 128
SUBLANES = 8
TM = 256
ATT_KC = 256
CONV_HALO = 16
CONV_CHUNK = 128
ADA_TN = 1024
MIB = 1024 * 1024


def _params(sem, vmem_mib):
    return pltpu.CompilerParams(dimension_semantics=sem, vmem_limit_bytes=vmem_mib * MIB)


def _resident(shape):
    zeros = (0,) * len(shape)
    return pl.BlockSpec(shape, lambda *_: zeros, pipeline_mode=pl.Buffered(1))


def _sigmoid(x):
    return 1.0 / (1.0 + jnp.exp(-x))


def _rms(x):
    return x * lax.rsqrt(jnp.mean(x * x, axis=-1, keepdims=True) + EPS)


def _dot(a, b):
    return jnp.dot(a, b, preferred_element_type=F32)


def _ada_kernel(c_ref, w_ref, b_ref, o_ref):
    c = c_ref[...]
    s = (c * _sigmoid(c)).astype(BF16)
    o_ref[...] = _dot(s, w_ref[...].astype(BF16)) + b_ref[...]


def _ada(c_rows, w_ada, b_ada):
    depth, d, n = w_ada.shape
    rows = c_rows.shape[0]
    return pl.pallas_call(
        _ada_kernel,
        grid=(depth, n // ADA_TN),
        in_specs=[
            pl.BlockSpec((rows, d), lambda l, j: (0, 0)),
            pl.BlockSpec((None, d, ADA_TN), lambda l, j: (l, 0, j)),
            pl.BlockSpec((None, 1, ADA_TN), lambda l, j: (l, 0, j)),
        ],
        out_specs=pl.BlockSpec((None, rows, ADA_TN), lambda l, j: (l, 0, j)),
        out_shape=jax.ShapeDtypeStruct((depth, rows, n), F32),
        compiler_params=_params(("arbitrary", "arbitrary"), 24),
        name="ada",
    )(c_rows, w_ada, b_ada.reshape(depth, 1, n))


def _inproj_kernel(x_ref, mod_ref, g_ref, w_ref, cos_ref, sin_ref,
                   q_ref, k_ref, vt_ref, uf_ref, uc_ref):
    hw = HEADS * LANES
    h = _rms(x_ref[...]) * g_ref[...] * (1.0 + mod_ref[1:2, :]) + mod_ref[0:1, :]
    r = _dot(h.astype(BF16), w_ref[...])
    cos = cos_ref[...]
    sin = sin_ref[...]
    lane = lax.broadcasted_iota(jnp.int32, cos.shape, 1)
    first_half = (lane & (QK_DIM // 4)) == 0

    def rope(t):
        partner = jnp.where(first_half, pltpu.roll(t, LANES - QK_DIM // 4, 1), pltpu.roll(t, QK_DIM // 4, 1))
        return t * cos + partner * sin

    row = lax.broadcasted_iota(jnp.int32, (LANES, cos.shape[0]), 0)
    zero = jnp.zeros((LANES, cos.shape[0]), F32)
    for j in range(HEADS):
        qt = (rope(r[:, j * LANES:(j + 1) * LANES]) * (QK_DIM ** -0.5 * LOG2_E)).T
        q_ref[j, 0] = jnp.where(row < QK_DIM, qt, zero).astype(BF16)
        q_ref[j, 1] = jnp.where(row >= QK_DIM, qt, zero).astype(BF16)
        k_ref[j] = rope(r[:, hw + j * LANES:hw + (j + 1) * LANES]).astype(BF16)
        vt_ref[j] = r[:, 2 * hw + j * LANES:2 * hw + (j + 1) * LANES].T.astype(BF16)
    uf_ref[...] = r[:, 3 * hw:3 * hw + uf_ref.shape[-1]].astype(BF16)
    uc_ref[...] = r[:, 3 * hw + uf_ref.shape[-1]:]


def _inproj(xs, modsel, g, w, cos_t, sin_t, n_ctx_tiles, fw, cw2):
    b, nt, d = xs.shape
    t = nt // TM
    hw = HEADS * LANES
    return pl.pallas_call(
        _inproj_kernel,
        grid=(b, t),
        in_specs=[
            pl.BlockSpec((None, TM, d), lambda i, j: (i, j, 0)),
            pl.BlockSpec((None, None, 6, d), lambda i, j: (i, jnp.where(j < n_ctx_tiles, 0, 1), 0, 0)),
            _resident((1, d)),
            _resident((d, 3 * hw + fw + cw2)),
            pl.BlockSpec((TM, LANES), lambda i, j: (j, 0)),
            pl.BlockSpec((TM, LANES), lambda i, j: (j, 0)),
        ],
        out_specs=[
            pl.BlockSpec((None, HEADS, None, 2, LANES, TM), lambda i, j: (i, 0, j, 0, 0, 0)),
            pl.BlockSpec((None, HEADS, TM, LANES), lambda i, j: (i, 0, j, 0)),
            pl.BlockSpec((None, HEADS, V_DIM, TM), lambda i, j: (i, 0, 0, j)),
            pl.BlockSpec((None, TM, fw), lambda i, j: (i, j, 0)),
            pl.BlockSpec((None, TM, cw2), lambda i, j: (i, j, 0)),
        ],
        out_shape=[
            jax.ShapeDtypeStruct((b, HEADS, t, 2, LANES, TM), BF16),
            jax.ShapeDtypeStruct((b, HEADS, nt, LANES), BF16),
            jax.ShapeDtypeStruct((b, HEADS, V_DIM, nt), BF16),
            jax.ShapeDtypeStruct((b, nt, fw), BF16),
            jax.ShapeDtypeStruct((b, nt, cw2), F32),
        ],
        compiler_params=_params(("arbitrary", "arbitrary"), 40),
        name="inproj",
    )(xs, modsel, g, w, cos_t, sin_t)


def _attn_kernel(lam_ref, g_ref, q_ref, k_ref, vt_ref, o_ref, s_ref, *, lam_init, n_ctx, with_ctx):
    nt = k_ref.shape[0]
    lq = lam_ref[...]
    lam = (jnp.exp(jnp.sum(lq[0:1] * lq[1:2], axis=-1, keepdims=True))
           - jnp.exp(jnp.sum(lq[2:3] * lq[3:4], axis=-1, keepdims=True)) + lam_init)

    def fold(x):
        return x.reshape(x.shape[0] // SUBLANES, SUBLANES, x.shape[1])

    def finish(t, ot, l1):
        ot = ot * (1.0 / l1)
        ms = jnp.mean(ot * ot, axis=0, keepdims=True)
        on = ot * lax.rsqrt(ms + EPS) * g_ref[...] * (1.0 - lam_init)
        o_ref[t * TM:(t + 1) * TM, :] = on.T.astype(BF16)

    def pipeline(first_tile, n_tiles, nk):
        maxima, sums = {}, {}
        for u in range(n_tiles + 2):
            ta, tb, tc = u, u - 1, u - 2
            do_a, do_b, do_c = ta < n_tiles, 0 <= tb < n_tiles, 0 <= tc < n_tiles
            if do_a:
                qts = [q_ref[first_tile + ta, mp] for mp in range(2)]
                m8 = [None, None]
            if do_b:
                mb = maxima.pop(tb)
                l8 = [jnp.zeros((SUBLANES, TM), F32) for _ in range(2)]
            if do_c:
                lc = sums.pop(tc)
                rho = lam * lc[0] / lc[1]
                ot = None
            for c in range(0, nk, ATT_KC):
                rows = slice(c, c + ATT_KC)
                if do_a:
                    for mp in range(2):
                        s = _dot(k_ref[rows, :], qts[mp])
                        s_ref[ta % 3, mp, rows, :] = s
                        cm = jnp.max(fold(s), axis=0)
                        m8[mp] = cm if m8[mp] is None else jnp.maximum(m8[mp], cm)
                if do_b:
                    for mp in range(2):
                        e = jnp.exp2(s_ref[tb % 3, mp, rows, :] - mb[mp])
                        s_ref[tb % 3, mp, rows, :] = e
                        l8[mp] = l8[mp] + jnp.sum(fold(e), axis=0)
                if do_c:
                    p = (s_ref[tc % 3, 0, rows, :] - rho * s_ref[tc % 3, 1, rows, :]).astype(BF16)
                    part = _dot(vt_ref[:, rows], p)
                    ot = part if ot is None else ot + part
            if do_a:
                maxima[ta] = [jnp.max(m, axis=0, keepdims=True) for m in m8]
            if do_b:
                sums[tb] = [jnp.sum(l, axis=0, keepdims=True) for l in l8]
            if do_c:
                finish(first_tile + tc, ot, lc[0])

    ctx_tiles = n_ctx // TM
    if with_ctx:
        pipeline(0, ctx_tiles, n_ctx)
    pipeline(ctx_tiles, nt // TM - ctx_tiles, nt)


def _attention(q, k, vt, lamv, g2d, lam_init, n_ctx, with_ctx):
    b, h, nt, _ = k.shape
    tiles = q.shape[2]
    return pl.pallas_call(
        functools.partial(_attn_kernel, lam_init=lam_init, n_ctx=n_ctx, with_ctx=with_ctx),
        grid=(b, h),
        in_specs=[
            _resident(lamv.shape),
            _resident(g2d.shape),
            pl.BlockSpec((None, None, tiles, 2, LANES, TM), lambda i, j: (i, j, 0, 0, 0, 0)),
            pl.BlockSpec((None, None, nt, LANES), lambda i, j: (i, j, 0, 0)),
            pl.BlockSpec((None, None, V_DIM, nt), lambda i, j: (i, j, 0, 0)),
        ],
        out_specs=pl.BlockSpec((None, nt, V_DIM), lambda i, j: (i, 0, j)),
        out_shape=jax.ShapeDtypeStruct((b, nt, h * V_DIM), BF16),
        scratch_shapes=[pltpu.VMEM((3, 2, nt, TM), F32)],
        compiler_params=_params(("arbitrary", "arbitrary"), 48),
        name="attention",
    )(lamv, g2d, q, k, vt)


def _fourier_kernel(*refs, segments):
    u_ref, chan_c_ref, chan_s_ref, wf_ref = refs[:4]
    pos_refs = refs[4:-1]
    y_ref = refs[-1]
    group_dim = u_ref.shape[-1] // FOURIER_GROUPS
    for i, (start, n) in enumerate(segments):
        u = u_ref[start:start + n, :]
        a = _dot(pos_refs[2 * i][...], u).astype(BF16)
        b = _dot(pos_refs[2 * i + 1][...], u).astype(BF16)
        f = (_dot(a, chan_c_ref[...]) - _dot(b, chan_s_ref[...])) * ((n * group_dim) ** -0.5)
        y_ref[start:start + n, :] = _dot(f.astype(BF16), wf_ref[...]).astype(BF16)


def _fourier(uf, chan_c, chan_s, wf_bd, pos_tables, segments):
    b, nt, fw = uf.shape
    row_spec = pl.BlockSpec((None, nt, fw), lambda i: (i, 0, 0))
    return pl.pallas_call(
        functools.partial(_fourier_kernel, segments=segments),
        grid=(b,),
        in_specs=[row_spec, _resident(chan_c.shape), _resident(chan_s.shape), _resident(wf_bd.shape)]
                 + [_resident(t.shape) for t in pos_tables],
        out_specs=row_spec,
        out_shape=jax.ShapeDtypeStruct((b, nt, fw), BF16),
        compiler_params=_params(("arbitrary",), 48),
        name="fourier",
    )(uf, chan_c, chan_s, wf_bd, *pos_tables)


def _split_dot(x, m):
    hi = x.astype(BF16)
    lo = (x - hi.astype(F32)).astype(BF16)
    return _dot(hi, m) + _dot(lo, m)


def _conv_kernel(u_ref, w_ref, b_ref, lg_ref, lb_ref, avg_ref, pw_ref, y_ref, zpad_ref, *, segments):
    cw = y_ref.shape[-1]
    halo = jnp.zeros((CONV_HALO, cw), F32)
    for start, n in segments:
        a = u_ref[start:start + n, 0:cw]
        gate = u_ref[start:start + n, cw:2 * cw]
        zpad_ref[0:CONV_HALO, :] = halo
        zpad_ref[CONV_HALO:CONV_HALO + n, :] = a * _sigmoid(gate)
        zpad_ref[CONV_HALO + n:2 * CONV_HALO + n, :] = halo

        def chunk(i, carry, start=start):
            r0 = pl.multiple_of(i * CONV_CHUNK, CONV_CHUNK)
            halves = []
            for lo in range(0, cw, LANES):
                win = zpad_ref[pl.ds(r0, CONV_CHUNK + 2 * CONV_HALO), lo:lo + LANES]
                part = jnp.zeros((CONV_CHUNK, LANES), F32) + b_ref[:, lo:lo + LANES]
                for shift in range(SUBLANES):
                    rolled = win if shift == 0 else pltpu.roll(win, win.shape[0] - shift, 0)
                    for aligned in range(0, 2 * CONV_HALO, SUBLANES):
                        tap = aligned + shift - (CONV_HALO - CONV_K // 2)
                        if 0 <= tap < CONV_K:
                            part = part + rolled[aligned:aligned + CONV_CHUNK] * w_ref[tap:tap + 1, lo:lo + LANES]
                halves.append(part)
            acc = jnp.concatenate(halves, axis=1)
            mu = _split_dot(acc, avg_ref[...])
            dev = acc - mu
            var = _split_dot(dev * dev, avg_ref[...])
            zn = dev * lax.rsqrt(var + EPS) * lg_ref[...] + lb_ref[...]
            act = (zn * _sigmoid(zn)).astype(BF16)
            y_ref[pl.ds(start + r0, CONV_CHUNK), :] = _dot(act, pw_ref[...]).astype(BF16)
            return carry

        lax.fori_loop(0, n // CONV_CHUNK, chunk, 0)


def _conv(uc, conv_w, conv_b, ln_g, ln_b, avg, w_pw, segments):
    b, nt, cw2 = uc.shape
    cw = cw2 // 2
    seg_max = max(n for _, n in segments)
    return pl.pallas_call(
        functools.partial(_conv_kernel, segments=segments),
        grid=(b,),
        in_specs=[
            pl.BlockSpec((None, nt, cw2), lambda i: (i, 0, 0)),
            _resident(conv_w.shape), _resident(conv_b.shape), _resident(ln_g.shape), _resident(ln_b.shape),
            _resident(avg.shape), _resident(w_pw.shape),
        ],
        out_specs=pl.BlockSpec((None, nt, cw), lambda i: (i, 0, 0)),
        out_shape=jax.ShapeDtypeStruct((b, nt, cw), BF16),
        scratch_shapes=[pltpu.VMEM((seg_max + 2 * CONV_HALO, cw), F32)],
        compiler_params=_params(("arbitrary",), 40),
        name="conv",
    )(uc, conv_w, conv_b, ln_g, ln_b, avg, w_pw)


def _ffn_kernel(x_ref, o_ref, yf_ref, yc_ref, mod_ref, g_ref, wout_ref, w1_ref, w3_ref, w2_ref, fg_ref,
                out_ref, *, final):
    aw = o_ref.shape[-1]
    fw = yf_ref.shape[-1]
    y = (_dot(o_ref[...], wout_ref[0:aw, :]) + _dot(yf_ref[...], wout_ref[aw:aw + fw, :])
         + _dot(yc_ref[...], wout_ref[aw + fw:, :]))
    x1 = x_ref[...] + mod_ref[2:3, :] * y
    h = (_rms(x1) * g_ref[...] * (1.0 + mod_ref[4:5, :]) + mod_ref[3:4, :]).astype(BF16)
    a = _dot(h, w1_ref[...])
    gated = (a * _sigmoid(a) * _dot(h, w3_ref[...])).astype(BF16)
    x2 = x1 + mod_ref[5:6, :] * _dot(gated, w2_ref[...])
    if final:
        x2 = _rms(x2) * fg_ref[...]
    out_ref[...] = x2


def _out_ffn(xs, o, yf, yc, modsel, g2, w_out, w1, w3, w2, final_g, n_ctx_tiles, final):
    b, nt, d = xs.shape
    skip = n_ctx_tiles if final else 0
    tiles = nt // TM - skip

    def rows(width):
        return pl.BlockSpec((None, TM, width), lambda i, j: (i, j + skip, 0))

    return pl.pallas_call(
        functools.partial(_ffn_kernel, final=final),
        grid=(b, tiles),
        in_specs=[
            rows(d), rows(o.shape[-1]), rows(yf.shape[-1]), rows(yc.shape[-1]),
            pl.BlockSpec((None, None, 6, d), lambda i, j: (i, jnp.where(j + skip < n_ctx_tiles, 0, 1), 0, 0)),
            _resident(g2.shape), _resident(w_out.shape), _resident(w1.shape), _resident(w3.shape),
            _resident(w2.shape), _resident(final_g.shape),
        ],
        out_specs=pl.BlockSpec((None, TM, d), lambda i, j: (i, j, 0)),
        out_shape=jax.ShapeDtypeStruct((b, tiles * TM, d), F32),
        compiler_params=_params(("arbitrary", "arbitrary"), 56),
        name="out_ffn",
    )(xs, o, yf, yc, modsel, g2, w_out, w1, w3, w2, final_g)


def _rope_tables(n_ctx, n_lat):
    n_freq = QK_DIM // 4
    tok = jnp.arange(n_lat)
    row = (tok // GRID_W).astype(F32)
    col = (tok % GRID_W).astype(F32)
    inv_freq = ROPE_BASE ** (-jnp.arange(n_freq, dtype=F32) / n_freq)
    ang_r = row[:, None] * inv_freq
    ang_c = col[:, None] * inv_freq
    cos = jnp.concatenate([jnp.cos(ang_r)] * 2 + [jnp.cos(ang_c)] * 2, axis=-1)
    sin = jnp.concatenate([-jnp.sin(ang_r), jnp.sin(ang_r), -jnp.sin(ang_c), jnp.sin(ang_c)], axis=-1)
    cos = jnp.concatenate([jnp.ones((n_ctx, QK_DIM), F32), cos], axis=0)
    sin = jnp.concatenate([jnp.zeros((n_ctx, QK_DIM), F32), sin], axis=0)
    reps = LANES // QK_DIM
    return jnp.tile(cos, (1, reps)), jnp.tile(sin, (1, reps))


def _dft_tables(n):
    idx = jnp.arange(n, dtype=jnp.int32)
    ang = ((idx[:, None] * idx[None, :]) % n).astype(F32) * (2.0 * math.pi / n)
    return jnp.cos(ang), jnp.sin(ang)


def _block_diag(blocks):
    g, r, c = blocks.shape
    eye = jnp.eye(g, dtype=blocks.dtype)
    return (eye[:, None, :, None] * blocks[:, :, None, :]).reshape(g * r, g * c)


def _inproj_columns(qk_w, attn_w, fw, cw2):
    cols = []
    for base in (0, 2 * qk_w):
        for h in range(HEADS):
            cols += list(range(base + h * QK_DIM, base + (h + 1) * QK_DIM))
            cols += list(range(base + qk_w + h * QK_DIM, base + qk_w + (h + 1) * QK_DIM))
    cols += list(range(4 * qk_w, 4 * qk_w + attn_w + fw + cw2))
    return np.asarray(cols, dtype=np.int32)


def kernel(x, c, ctx, c_ctx, w_ada, b_ada, norm1_g, norm2_g, w_in, lam_q1, lam_k1, lam_q2, lam_k2, subln_g,
           w_fourier, conv_w, conv_b, conv_ln_g, conv_ln_b, w_conv_out, w_out, w_ffn1, w_ffn3, w_ffn2, final_g):
    b, n_lat, d = x.shape
    n_ctx = ctx.shape[1]
    depth = w_ada.shape[0]
    fw = w_fourier.shape[1] * w_fourier.shape[2]
    cw = conv_w.shape[-1]
    qk_w = HEADS * QK_DIM
    attn_w = HEADS * V_DIM
    assert n_lat % TM == 0 and n_ctx % TM == 0 and n_lat % GRID_W == 0
    assert w_in.shape[-1] == 4 * qk_w + attn_w + fw + 2 * cw
    n_ctx_tiles = n_ctx // TM

    pad = (-(b + 1)) % 8
    c_rows = jnp.concatenate([c, c_ctx[None, :], jnp.zeros((pad, d), c.dtype)], axis=0)
    mod = _ada(c_rows, w_ada, b_ada)

    cos_t, sin_t = _rope_tables(n_ctx, n_lat)
    group_dim = fw // FOURIER_GROUPS
    cc, cs = _dft_tables(group_dim)
    eye = jnp.eye(FOURIER_GROUPS, dtype=F32)
    chan_c = jnp.kron(eye, cc).astype(BF16)
    chan_s = jnp.kron(eye, cs).astype(BF16)
    lat_tables = [t.astype(BF16) for t in _dft_tables(n_lat)]
    ctx_tables = [t.astype(BF16) for t in _dft_tables(n_ctx)]
    avg = jnp.kron(jnp.eye(CONV_GROUPS, dtype=F32),
                   jnp.full((cw // CONV_GROUPS, cw // CONV_GROUPS), CONV_GROUPS / cw, F32)).astype(BF16)
    cols = _inproj_columns(qk_w, attn_w, fw, 2 * cw)

    xs = jnp.concatenate([ctx, x], axis=1)
    lat_seg = (n_ctx, n_lat)
    ctx_seg = (0, n_ctx)

    for l in range(depth):
        last = l == depth - 1
        lam_init = 0.8 - 0.6 * math.exp(-0.3 * l)
        mod_lat = mod[l, :b].reshape(b, 6, d)
        mod_ctx = jnp.broadcast_to(mod[l, b].reshape(1, 6, d), (b, 6, d))
        modsel = jnp.stack([mod_ctx, mod_lat], axis=1)

        w_in_l = w_in[l][:, cols].astype(BF16)
        q, k, vt, uf, uc = _inproj(xs, modsel, norm1_g[l][None, :], w_in_l, cos_t, sin_t, n_ctx_tiles, fw, 2 * cw)

        lamv = jnp.stack([lam_q1[l], lam_k1[l], lam_q2[l], lam_k2[l]], axis=0)
        g2d = jnp.broadcast_to(subln_g[l][:, None], (V_DIM, TM))
        o = _attention(q, k, vt, lamv, g2d, lam_init, n_ctx, with_ctx=not last)

        segments = (lat_seg,) if last else (lat_seg, ctx_seg)
        tables = lat_tables if last else lat_tables + ctx_tables
        yf = _fourier(uf, chan_c, chan_s, _block_diag(w_fourier[l]).astype(BF16), tables, segments)
        yc = _conv(uc, conv_w[l], conv_b[l][None, :], conv_ln_g[l][None, :], conv_ln_b[l][None, :], avg,
                   w_conv_out[l].astype(BF16), segments)

        xs = _out_ffn(xs, o, yf, yc, modsel, norm2_g[l][None, :], w_out[l].astype(BF16),
                      w_ffn1[l].astype(BF16), w_ffn3[l].astype(BF16), w_ffn2[l].astype(BF16),
                      final_g[None, :], n_ctx_tiles, final=last)
    return xs
```

```python
import functools
import math

import numpy as np
import jax
import jax.numpy as jnp
from jax import lax
from jax.experimental import pallas as pl
from jax.experimental.pallas import tpu as pltpu

F32 = jnp.float32
BF16 = jnp.bfloat16

GRID_W = 64
HEADS = 4
QK_DIM = 64
V_DIM = 2 * QK_DIM
FOURIER_GROUPS = 4
CONV_GROUPS = 4
CONV_K = 31
ROPE_BASE = 10000.0
EPS = 1e-6
LOG2_E = math.log2(math.e)

LANES = 128
SUBLANES = 8
TM = 256
ATT_KC = 256
CONV_HALO = 16
CONV_CHUNK = 128
ADA_TN = 1024
DFT_LO = 64
MIB = 1024 * 1024


def _params(sem, vmem_mib):
    return pltpu.CompilerParams(dimension_semantics=sem, vmem_limit_bytes=vmem_mib * MIB)


def _resident(arr, layer=None):
    if layer is None:
        idx = (0,) * arr.ndim
        return pl.BlockSpec(arr.shape, lambda *_: idx, pipeline_mode=pl.Buffered(1))
    idx = (layer,) + (0,) * (arr.ndim - 1)
    return pl.BlockSpec((None,) + arr.shape[1:], lambda *_: idx, pipeline_mode=pl.Buffered(1))


def _sigmoid(x):
    return 1.0 / (1.0 + jnp.exp(-x))


def _rms(x):
    return x * lax.rsqrt(jnp.mean(x * x, axis=-1, keepdims=True) + EPS)


def _dot(a, b):
    return jnp.dot(a, b, preferred_element_type=F32)


def _stream_specs(stream, n_ctx_tiles, skip):
    d = stream[0].shape[-1]
    if len(stream) == 1:
        return [pl.BlockSpec((None, TM, d), lambda i, j: (i, j + skip, 0))]
    return [pl.BlockSpec((None, TM, d), lambda i, j: (i, jnp.minimum(j + skip, n_ctx_tiles - 1), 0)),
            pl.BlockSpec((None, TM, d), lambda i, j: (i, jnp.maximum(j + skip - n_ctx_tiles, 0), 0))]


def _stream_tile(refs, n_ctx_tiles, skip):
    if len(refs) == 1:
        return refs[0][...]
    return jnp.where(pl.program_id(1) + skip < n_ctx_tiles, refs[0][...], refs[1][...])


def _mod_spec(mod, layer, n_batch, n_ctx_tiles, skip):
    return pl.BlockSpec((None, None) + mod.shape[2:],
                        lambda i, j: (layer, jnp.where(j + skip < n_ctx_tiles, n_batch, i), 0, 0))


def _ada_kernel(c_ref, w_ref, b_ref, o_ref):
    c = c_ref[...]
    s = (c * _sigmoid(c)).astype(BF16)
    o_ref[...] = _dot(s, w_ref[...].astype(BF16)) + b_ref[...]


def _ada(c_rows, w_ada, b_ada):
    depth, d, n = w_ada.shape
    rows = c_rows.shape[0]
    return pl.pallas_call(
        _ada_kernel,
        grid=(depth, n // ADA_TN),
        in_specs=[
            pl.BlockSpec((rows, d), lambda l, j: (0, 0)),
            pl.BlockSpec((None, d, ADA_TN), lambda l, j: (l, 0, j)),
            pl.BlockSpec((None, 1, ADA_TN), lambda l, j: (l, 0, j)),
        ],
        out_specs=pl.BlockSpec((None, rows, ADA_TN), lambda l, j: (l, 0, j)),
        out_shape=jax.ShapeDtypeStruct((depth, rows, n), F32),
        compiler_params=_params(("arbitrary", "arbitrary"), 24),
        name="ada",
    )(c_rows, w_ada, b_ada.reshape(depth, 1, n))


def _inproj_kernel(*refs, n_stream, n_ctx_tiles):
    x_refs = refs[:n_stream]
    mod_ref, g_ref, w_ref, cos_ref, sin_ref, q_ref, k_ref, vt_ref, uf_ref, uc_ref = refs[n_stream:]
    hw = HEADS * LANES
    x = _stream_tile(x_refs, n_ctx_tiles, 0)
    h = _rms(x) * g_ref[...] * (1.0 + mod_ref[1:2, :]) + mod_ref[0:1, :]
    r = _dot(h.astype(BF16), w_ref[...])
    cos = cos_ref[...]
    sin = sin_ref[...]
    lane = lax.broadcasted_iota(jnp.int32, cos.shape, 1)
    first_half = (lane & (QK_DIM // 4)) == 0

    def rope(t):
        partner = jnp.where(first_half, pltpu.roll(t, LANES - QK_DIM // 4, 1), pltpu.roll(t, QK_DIM // 4, 1))
        return t * cos + partner * sin

    row = lax.broadcasted_iota(jnp.int32, (LANES, cos.shape[0]), 0)
    zero = jnp.zeros((LANES, cos.shape[0]), F32)
    for j in range(HEADS):
        qt = (rope(r[:, j * LANES:(j + 1) * LANES]) * (QK_DIM ** -0.5 * LOG2_E)).T
        q_ref[j, 0] = jnp.where(row < QK_DIM, qt, zero).astype(BF16)
        q_ref[j, 1] = jnp.where(row >= QK_DIM, qt, zero).astype(BF16)
        k_ref[j] = rope(r[:, hw + j * LANES:hw + (j + 1) * LANES]).astype(BF16)
        vt_ref[j] = r[:, 2 * hw + j * LANES:2 * hw + (j + 1) * LANES].T.astype(BF16)
    uf_ref[...] = r[:, 3 * hw:3 * hw + uf_ref.shape[-1]].astype(BF16)
    uc_ref[...] = r[:, 3 * hw + uf_ref.shape[-1]:]


def _inproj(stream, mod, g, w, cos_t, sin_t, layer, n_ctx_tiles, nt, fw, cw2):
    b = stream[0].shape[0]
    t = nt // TM
    return pl.pallas_call(
        functools.partial(_inproj_kernel, n_stream=len(stream), n_ctx_tiles=n_ctx_tiles),
        grid=(b, t),
        in_specs=_stream_specs(stream, n_ctx_tiles, 0) + [
            _mod_spec(mod, layer, b, n_ctx_tiles, 0),
            _resident(g, layer),
            _resident(w, layer),
            pl.BlockSpec((TM, LANES), lambda i, j: (j, 0)),
            pl.BlockSpec((TM, LANES), lambda i, j: (j, 0)),
        ],
        out_specs=[
            pl.BlockSpec((None, HEADS, None, 2, LANES, TM), lambda i, j: (i, 0, j, 0, 0, 0)),
            pl.BlockSpec((None, HEADS, TM, LANES), lambda i, j: (i, 0, j, 0)),
            pl.BlockSpec((None, HEADS, V_DIM, TM), lambda i, j: (i, 0, 0, j)),
            pl.BlockSpec((None, TM, fw), lambda i, j: (i, j, 0)),
            pl.BlockSpec((None, TM, cw2), lambda i, j: (i, j, 0)),
        ],
        out_shape=[
            jax.ShapeDtypeStruct((b, HEADS, t, 2, LANES, TM), BF16),
            jax.ShapeDtypeStruct((b, HEADS, nt, LANES), BF16),
            jax.ShapeDtypeStruct((b, HEADS, V_DIM, nt), BF16),
            jax.ShapeDtypeStruct((b, nt, fw), BF16),
            jax.ShapeDtypeStruct((b, nt, cw2), F32),
        ],
        compiler_params=_params(("arbitrary", "arbitrary"), 40),
        name="inproj",
    )(*stream, mod, g, w, cos_t, sin_t)


def _attn_kernel(lq1_ref, lk1_ref, lq2_ref, lk2_ref, g_ref, q_ref, k_ref, vt_ref, o_ref, s_ref, *,
                 lam_init, n_ctx, with_ctx):
    nt = k_ref.shape[0]
    lam = (jnp.exp(jnp.sum(lq1_ref[...] * lk1_ref[...], axis=-1, keepdims=True))
           - jnp.exp(jnp.sum(lq2_ref[...] * lk2_ref[...], axis=-1, keepdims=True)) + lam_init)

    def fold(x):
        return x.reshape(x.shape[0] // SUBLANES, SUBLANES, x.shape[1])

    def finish(out_tile, ot, l1):
        ot = ot * (1.0 / l1)
        ms = jnp.mean(ot * ot, axis=0, keepdims=True)
        on = (ot * lax.rsqrt(ms + EPS)).T * g_ref[...] * (1.0 - lam_init)
        o_ref[out_tile * TM:(out_tile + 1) * TM, :] = on.astype(BF16)

    def pipeline(first_tile, first_out_tile, n_tiles, nk):
        maxima, sums = {}, {}
        for u in range(n_tiles + 2):
            ta, tb, tc = u, u - 1, u - 2
            do_a, do_b, do_c = ta < n_tiles, 0 <= tb < n_tiles, 0 <= tc < n_tiles
            if do_a:
                qts = [q_ref[first_tile + ta, mp] for mp in range(2)]
                m8 = [None, None]
            if do_b:
                mb = maxima.pop(tb)
                l8 = [jnp.zeros((SUBLANES, TM), F32) for _ in range(2)]
            if do_c:
                lc = sums.pop(tc)
                rho = lam * lc[0] / lc[1]
                ot = None
            for c in range(0, nk, ATT_KC):
                rows = slice(c, c + ATT_KC)
                if do_a:
                    for mp in range(2):
                        s = _dot(k_ref[rows, :], qts[mp])
                        s_ref[ta % 3, mp, rows, :] = s
                        cm = jnp.max(fold(s), axis=0)
                        m8[mp] = cm if m8[mp] is None else jnp.maximum(m8[mp], cm)
                if do_b:
                    for mp in range(2):
                        e = jnp.exp2(s_ref[tb % 3, mp, rows, :] - mb[mp])
                        s_ref[tb % 3, mp, rows, :] = e
                        l8[mp] = l8[mp] + jnp.sum(fold(e), axis=0)
                if do_c:
                    p = (s_ref[tc % 3, 0, rows, :] - rho * s_ref[tc % 3, 1, rows, :]).astype(BF16)
                    part = _dot(vt_ref[:, rows], p)
                    ot = part if ot is None else ot + part
            if do_a:
                maxima[ta] = [jnp.max(m, axis=0, keepdims=True) for m in m8]
            if do_b:
                sums[tb] = [jnp.sum(l, axis=0, keepdims=True) for l in l8]
            if do_c:
                finish(first_out_tile + tc, ot, lc[0])

    ctx_tiles = n_ctx // TM
    if with_ctx:
        pipeline(0, 0, ctx_tiles, n_ctx)
    pipeline(ctx_tiles, ctx_tiles if with_ctx else 0, nt // TM - ctx_tiles, nt)


def _attention(q, k, vt, lam_vecs, g, layer, lam_init, n_ctx, with_ctx):
    b, h, nt, _ = k.shape
    tiles = q.shape[2]
    n_out = nt if with_ctx else nt - n_ctx
    return pl.pallas_call(
        functools.partial(_attn_kernel, lam_init=lam_init, n_ctx=n_ctx, with_ctx=with_ctx),
        grid=(b, h),
        in_specs=[_resident(v, layer) for v in lam_vecs] + [
            _resident(g, layer),
            pl.BlockSpec((None, None, tiles, 2, LANES, TM), lambda i, j: (i, j, 0, 0, 0, 0)),
            pl.BlockSpec((None, None, nt, LANES), lambda i, j: (i, j, 0, 0)),
            pl.BlockSpec((None, None, V_DIM, nt), lambda i, j: (i, j, 0, 0)),
        ],
        out_specs=pl.BlockSpec((None, n_out, V_DIM), lambda i, j: (i, 0, j)),
        out_shape=jax.ShapeDtypeStruct((b, n_out, h * V_DIM), BF16),
        scratch_shapes=[pltpu.VMEM((3, 2, nt, TM), F32)],
        compiler_params=_params(("arbitrary", "arbitrary"), 48),
        name="attention",
    )(*lam_vecs, g, q, k, vt)


def _fourier_kernel(*refs, segments):
    u_ref, chan_c_ref, chan_s_ref, wf_ref = refs[:4]
    pos_refs = refs[4:-1]
    y_ref = refs[-1]
    group_dim = u_ref.shape[-1] // FOURIER_GROUPS
    for i, (start, out_start, n) in enumerate(segments):
        u = u_ref[start:start + n, :]
        a = _dot(pos_refs[2 * i][...], u).astype(BF16)
        b = _dot(pos_refs[2 * i + 1][...], u).astype(BF16)
        f = (_dot(a, chan_c_ref[...]) - _dot(b, chan_s_ref[...])) * ((n * group_dim) ** -0.5)
        y_ref[out_start:out_start + n, :] = _dot(f.astype(BF16), wf_ref[...]).astype(BF16)


def _fourier(uf, chan_c, chan_s, wf_bd, layer, pos_tables, segments, n_out):
    b, nt, fw = uf.shape
    return pl.pallas_call(
        functools.partial(_fourier_kernel, segments=segments),
        grid=(b,),
        in_specs=[pl.BlockSpec((None, nt, fw), lambda i: (i, 0, 0)),
                  _resident(chan_c), _resident(chan_s), _resident(wf_bd, layer)]
                 + [_resident(t) for t in pos_tables],
        out_specs=pl.BlockSpec((None, n_out, fw), lambda i: (i, 0, 0)),
        out_shape=jax.ShapeDtypeStruct((b, n_out, fw), BF16),
        compiler_params=_params(("arbitrary",), 48),
        name="fourier",
    )(uf, chan_c, chan_s, wf_bd, *pos_tables)


def _split_dot(x, m):
    hi = x.astype(BF16)
    lo = (x - hi.astype(F32)).astype(BF16)
    return _dot(hi, m) + _dot(lo, m)


def _conv_kernel(u_ref, w_ref, b_ref, lg_ref, lb_ref, avg_ref, pw_ref, y_ref, zpad_ref, *, segments):
    cw = y_ref.shape[-1]
    halo = jnp.zeros((CONV_HALO, cw), F32)
    for start, out_start, n in segments:
        a = u_ref[start:start + n, 0:cw]
        gate = u_ref[start:start + n, cw:2 * cw]
        zpad_ref[0:CONV_HALO, :] = halo
        zpad_ref[CONV_HALO:CONV_HALO + n, :] = a * _sigmoid(gate)
        zpad_ref[CONV_HALO + n:2 * CONV_HALO + n, :] = halo

        def chunk(i, carry, out_start=out_start):
            r0 = pl.multiple_of(i * CONV_CHUNK, CONV_CHUNK)
            halves = []
            for lo in range(0, cw, LANES):
                win = zpad_ref[pl.ds(r0, CONV_CHUNK + 2 * CONV_HALO), lo:lo + LANES]
                part = jnp.zeros((CONV_CHUNK, LANES), F32) + b_ref[:, lo:lo + LANES]
                for shift in range(SUBLANES):
                    rolled = win if shift == 0 else pltpu.roll(win, win.shape[0] - shift, 0)
                    for aligned in range(0, 2 * CONV_HALO, SUBLANES):
                        tap = aligned + shift - (CONV_HALO - CONV_K // 2)
                        if 0 <= tap < CONV_K:
                            part = part + rolled[aligned:aligned + CONV_CHUNK] * w_ref[tap:tap + 1, lo:lo + LANES]
                halves.append(part)
            acc = jnp.concatenate(halves, axis=1)
            mu = _split_dot(acc, avg_ref[...])
            dev = acc - mu
            var = _split_dot(dev * dev, avg_ref[...])
            zn = dev * lax.rsqrt(var + EPS) * lg_ref[...] + lb_ref[...]
            act = (zn * _sigmoid(zn)).astype(BF16)
            y_ref[pl.ds(out_start + r0, CONV_CHUNK), :] = _dot(act, pw_ref[...]).astype(BF16)
            return carry

        lax.fori_loop(0, n // CONV_CHUNK, chunk, 0)


def _conv(uc, conv_w, conv_b, ln_g, ln_b, avg, w_pw, layer, segments, n_out):
    b, nt, cw2 = uc.shape
    cw = cw2 // 2
    seg_max = max(n for _, _, n in segments)
    return pl.pallas_call(
        functools.partial(_conv_kernel, segments=segments),
        grid=(b,),
        in_specs=[
            pl.BlockSpec((None, nt, cw2), lambda i: (i, 0, 0)),
            _resident(conv_w, layer), _resident(conv_b, layer), _resident(ln_g, layer), _resident(ln_b, layer),
            _resident(avg), _resident(w_pw, layer),
        ],
        out_specs=pl.BlockSpec((None, n_out, cw), lambda i: (i, 0, 0)),
        out_shape=jax.ShapeDtypeStruct((b, n_out, cw), BF16),
        scratch_shapes=[pltpu.VMEM((seg_max + 2 * CONV_HALO, cw), F32)],
        compiler_params=_params(("arbitrary",), 40),
        name="conv",
    )(uc, conv_w, conv_b, ln_g, ln_b, avg, w_pw)


def _ffn_kernel(*refs, n_stream, n_ctx_tiles, skip, final):
    x_refs = refs[:n_stream]
    o_ref, yf_ref, yc_ref, mod_ref, g_ref, wout_ref, w1_ref, w3_ref, w2_ref, fg_ref, out_ref = refs[n_stream:]
    aw = o_ref.shape[-1]
    fw = yf_ref.shape[-1]
    y = (_dot(o_ref[...], wout_ref[0:aw, :]) + _dot(yf_ref[...], wout_ref[aw:aw + fw, :])
         + _dot(yc_ref[...], wout_ref[aw + fw:, :]))
    x1 = _stream_tile(x_refs, n_ctx_tiles, skip) + mod_ref[2:3, :] * y
    h = (_rms(x1) * g_ref[...] * (1.0 + mod_ref[4:5, :]) + mod_ref[3:4, :]).astype(BF16)
    a = _dot(h, w1_ref[...])
    gated = (a * _sigmoid(a) * _dot(h, w3_ref[...])).astype(BF16)
    x2 = x1 + mod_ref[5:6, :] * _dot(gated, w2_ref[...])
    if final:
        x2 = _rms(x2) * fg_ref[...]
    out_ref[...] = x2


def _out_ffn(stream, o, yf, yc, mod, g2, w_out, w1, w3, w2, final_g, layer, n_ctx_tiles, nt, final):
    b, d = stream[0].shape[0], stream[0].shape[-1]
    skip = n_ctx_tiles if final else 0
    tiles = nt // TM - skip

    def rows(width):
        return pl.BlockSpec((None, TM, width), lambda i, j: (i, j, 0))

    return pl.pallas_call(
        functools.partial(_ffn_kernel, n_stream=len(stream), n_ctx_tiles=n_ctx_tiles, skip=skip, final=final),
        grid=(b, tiles),
        in_specs=_stream_specs(stream, n_ctx_tiles, skip) + [
            rows(o.shape[-1]), rows(yf.shape[-1]), rows(yc.shape[-1]),
            _mod_spec(mod, layer, b, n_ctx_tiles, skip),
            _resident(g2, layer), _resident(w_out, layer), _resident(w1, layer), _resident(w3, layer),
            _resident(w2, layer), _resident(final_g),
        ],
        out_specs=pl.BlockSpec((None, TM, d), lambda i, j: (i, j, 0)),
        out_shape=jax.ShapeDtypeStruct((b, tiles * TM, d), F32),
        compiler_params=_params(("arbitrary", "arbitrary"), 56),
        name="out_ffn",
    )(*stream, o, yf, yc, mod, g2, w_out, w1, w3, w2, final_g)


def _rope_tables(n_ctx, n_lat):
    n_freq = QK_DIM // 4
    tok = np.arange(n_lat)
    inv_freq = np.float32(ROPE_BASE) ** (-np.arange(n_freq, dtype=np.float32) / np.float32(n_freq))
    ang_r = ((tok // GRID_W).astype(np.float32)[:, None] * inv_freq).astype(np.float64)
    ang_c = ((tok % GRID_W).astype(np.float32)[:, None] * inv_freq).astype(np.float64)
    cos = np.concatenate([np.cos(ang_r)] * 2 + [np.cos(ang_c)] * 2, axis=-1)
    sin = np.concatenate([-np.sin(ang_r), np.sin(ang_r), -np.sin(ang_c), np.sin(ang_c)], axis=-1)
    cos = np.concatenate([np.ones((n_ctx, QK_DIM)), cos], axis=0)
    sin = np.concatenate([np.zeros((n_ctx, QK_DIM)), sin], axis=0)
    reps = LANES // QK_DIM
    return (jnp.asarray(np.tile(cos, (1, reps)), dtype=F32), jnp.asarray(np.tile(sin, (1, reps)), dtype=F32))


def _dft_angles(rows, n):
    return 2.0 * np.pi * ((np.asarray(rows, np.int64)[:, None] * np.arange(n, dtype=np.int64)[None, :]) % n) / n


def _dft_tables_small(n):
    ang = _dft_angles(np.arange(n), n)
    return np.cos(ang).astype(np.float32), np.sin(ang).astype(np.float32)


def _dft_tables(n):
    if n <= 4 * DFT_LO:
        return [jnp.asarray(t).astype(BF16) for t in _dft_tables_small(n)]
    hi = n // DFT_LO
    ang_hi = _dft_angles(np.arange(hi) * DFT_LO, n)
    ang_lo = _dft_angles(np.arange(DFT_LO), n)
    ch, sh = (jnp.asarray(f(ang_hi), dtype=F32)[:, None, :] for f in (np.cos, np.sin))
    cl, sl = (jnp.asarray(f(ang_lo), dtype=F32)[None, :, :] for f in (np.cos, np.sin))
    return [(ch * cl - sh * sl).reshape(n, n).astype(BF16), (sh * cl + ch * sl).reshape(n, n).astype(BF16)]


def _block_diag(blocks):
    g, r, c = blocks.shape[-3:]
    eye = jnp.eye(g, dtype=blocks.dtype)
    out = eye[:, None, :, None] * blocks[..., :, :, None, :]
    return out.reshape(blocks.shape[:-3] + (g * r, g * c))


def _permute_inproj(w_in, qk_w):
    lead = w_in.shape[:-1]
    qk = w_in[..., :4 * qk_w].reshape(lead + (2, 2, HEADS, QK_DIM))
    qk = jnp.swapaxes(qk, -3, -2).reshape(lead + (4 * qk_w,))
    return jnp.concatenate([qk, w_in[..., 4 * qk_w:]], axis=-1)


def kernel(x, c, ctx, c_ctx, w_ada, b_ada, norm1_g, norm2_g, w_in, lam_q1, lam_k1, lam_q2, lam_k2, subln_g,
           w_fourier, conv_w, conv_b, conv_ln_g, conv_ln_b, w_conv_out, w_out, w_ffn1, w_ffn3, w_ffn2, final_g):
    b, n_lat, d = x.shape
    n_ctx = ctx.shape[1]
    nt = n_ctx + n_lat
    depth = w_ada.shape[0]
    fw = w_fourier.shape[1] * w_fourier.shape[2]
    cw = conv_w.shape[-1]
    qk_w = HEADS * QK_DIM
    attn_w = HEADS * V_DIM
    assert n_lat % TM == 0 and n_ctx % TM == 0 and n_lat % GRID_W == 0
    assert w_in.shape[-1] == 4 * qk_w + attn_w + fw + 2 * cw
    n_ctx_tiles = n_ctx // TM

    pad = (-(b + 1)) % SUBLANES
    c_rows = jnp.concatenate([c, c_ctx[None, :], jnp.zeros((pad, d), c.dtype)], axis=0)
    mod = _ada(c_rows, w_ada, b_ada)
    mod = mod.reshape(depth, mod.shape[1], 6, d)

    cos_t, sin_t = _rope_tables(n_ctx, n_lat)
    cc, cs = _dft_tables_small(fw // FOURIER_GROUPS)
    eye = np.eye(FOURIER_GROUPS, dtype=np.float32)
    chan_c = jnp.asarray(np.kron(eye, cc)).astype(BF16)
    chan_s = jnp.asarray(np.kron(eye, cs)).astype(BF16)
    lat_tables = _dft_tables(n_lat)
    ctx_tables = _dft_tables(n_ctx)
    group = cw // CONV_GROUPS
    avg = jnp.asarray(np.kron(np.eye(CONV_GROUPS), np.full((group, group), 1.0 / group)), dtype=F32).astype(BF16)

    w_in_b = _permute_inproj(w_in, qk_w).astype(BF16)
    wf_bd = _block_diag(w_fourier).astype(BF16)
    w_pw_b, w_out_b = w_conv_out.astype(BF16), w_out.astype(BF16)
    w1_b, w3_b, w2_b = w_ffn1.astype(BF16), w_ffn3.astype(BF16), w_ffn2.astype(BF16)

    def per_layer_rows(a):
        return a.reshape(depth, 1, a.shape[-1])

    norm1, norm2, subln = per_layer_rows(norm1_g), per_layer_rows(norm2_g), per_layer_rows(subln_g)
    lam_vecs = [per_layer_rows(a) for a in (lam_q1, lam_k1, lam_q2, lam_k2)]
    cb, lg, lb = per_layer_rows(conv_b), per_layer_rows(conv_ln_g), per_layer_rows(conv_ln_b)
    fg = final_g[None, :]

    stream = (ctx, x)
    for l in range(depth):
        last = l == depth - 1
        lam_init = 0.8 - 0.6 * math.exp(-0.3 * l)
        q, k, vt, uf, uc = _inproj(stream, mod, norm1, w_in_b, cos_t, sin_t, l, n_ctx_tiles, nt, fw, 2 * cw)
        o = _attention(q, k, vt, lam_vecs, subln, l, lam_init, n_ctx, with_ctx=not last)
        if last:
            segments, tables, n_out = ((n_ctx, 0, n_lat),), lat_tables, n_lat
        else:
            segments, tables, n_out = ((n_ctx, n_ctx, n_lat), (0, 0, n_ctx)), lat_tables + ctx_tables, nt
        yf = _fourier(uf, chan_c, chan_s, wf_bd, l, tables, segments, n_out)
        yc = _conv(uc, conv_w, cb, lg, lb, avg, w_pw_b, l, segments, n_out)
        stream = (_out_ffn(stream, o, yf, yc, mod, norm2, w_out_b, w1_b, w3_b, w2_b, fg, l, n_ctx_tiles, nt,
                           final=last),)
    return stream[0]
```

```python
import functools
import math

import numpy as np
import jax
import jax.numpy as jnp
from jax import lax
from jax.experimental import pallas as pl
from jax.experimental.pallas import tpu as pltpu

F32 = jnp.float32
BF16 = jnp.bfloat16

GRID_W = 64
HEADS = 4
QK_DIM = 64
V_DIM = 2 * QK_DIM
FOURIER_GROUPS = 4
CONV_GROUPS = 4
CONV_K = 31
ROPE_BASE = 10000.0
EPS = 1e-6
LOG2_E = math.log2(math.e)

LANES = 128
SUBLANES = 8
TM = 256
ROW_SUBTILES = 2
ATT_KC = 256
CONV_HALO = 16
CONV_CHUNK = 128
ADA_TN = 1024
DFT_LO = 64
MIB = 1024 * 1024


def _params(sem, vmem_mib):
    return pltpu.CompilerParams(dimension_semantics=sem, vmem_limit_bytes=vmem_mib * MIB)


def _resident(arr, layer=None):
    if layer is None:
        idx = (0,) * arr.ndim
        return pl.BlockSpec(arr.shape, lambda *_: idx, pipeline_mode=pl.Buffered(1))
    idx = (layer,) + (0,) * (arr.ndim - 1)
    return pl.BlockSpec((None,) + arr.shape[1:], lambda *_: idx, pipeline_mode=pl.Buffered(1))


def _mod_spec(mod, layer, fixed_row):
    if fixed_row is None:
        return pl.BlockSpec((None, None) + mod.shape[2:], lambda i, j: (layer, i, 0, 0))
    return pl.BlockSpec((None, None) + mod.shape[2:], lambda i, j: (layer, fixed_row, 0, 0))


def _sigmoid(x):
    return 1.0 / (1.0 + jnp.exp(-x))


def _rms(x):
    return x * lax.rsqrt(jnp.mean(x * x, axis=-1, keepdims=True) + EPS)


def _dot(a, b):
    return jnp.dot(a, b, preferred_element_type=F32)


def _ada_kernel(c_ref, w_ref, b_ref, o_ref):
    c = c_ref[...]
    s = (c * _sigmoid(c)).astype(BF16)
    o_ref[...] = _dot(s, w_ref[...].astype(BF16)) + b_ref[...]


def _ada(c_rows, w_ada, b_ada):
    depth, d, n = w_ada.shape
    rows = c_rows.shape[0]
    return pl.pallas_call(
        _ada_kernel,
        grid=(depth, n // ADA_TN),
        in_specs=[
            pl.BlockSpec((rows, d), lambda l, j: (0, 0)),
            pl.BlockSpec((None, d, ADA_TN), lambda l, j: (l, 0, j)),
            pl.BlockSpec((None, 1, ADA_TN), lambda l, j: (l, 0, j)),
        ],
        out_specs=pl.BlockSpec((None, rows, ADA_TN), lambda l, j: (l, 0, j)),
        out_shape=jax.ShapeDtypeStruct((depth, rows, n), F32),
        compiler_params=_params(("arbitrary", "arbitrary"), 24),
        name="ada",
    )(c_rows, w_ada, b_ada.reshape(depth, 1, n))


def _inproj_kernel(*refs, use_rope, kv_only, n_sub):
    x_ref, mod_ref, g_ref, w_ref = refs[:4]
    refs = refs[4:]
    if use_rope:
        cos_ref, sin_ref = refs[:2]
        refs = refs[2:]
    hw = HEADS * LANES
    lane = lax.broadcasted_iota(jnp.int32, (TM, LANES), 1)
    first_half = (lane & (QK_DIM // 4)) == 0
    row = lax.broadcasted_iota(jnp.int32, (LANES, TM), 0)
    zero = jnp.zeros((LANES, TM), F32)

    for u in range(n_sub):
        rows = slice(u * TM, (u + 1) * TM)
        h = (_rms(x_ref[rows, :]) * g_ref[...] * (1.0 + mod_ref[1:2, :]) + mod_ref[0:1, :]).astype(BF16)

        def rope(t):
            if not use_rope:
                return t
            partner = jnp.where(first_half, pltpu.roll(t, LANES - QK_DIM // 4, 1), pltpu.roll(t, QK_DIM // 4, 1))
            return t * cos_ref[rows, :] + partner * sin_ref[rows, :]

        if kv_only:
            k_ref, vt_ref = refs
            r = _dot(h, w_ref[:, hw:3 * hw])
            for j in range(HEADS):
                k_ref[j, rows, :] = rope(r[:, j * LANES:(j + 1) * LANES]).astype(BF16)
                vt_ref[j, :, rows] = r[:, hw + j * LANES:hw + (j + 1) * LANES].T.astype(BF16)
            continue

        q_ref, k_ref, vt_ref, uf_ref, uc_ref = refs
        r = _dot(h, w_ref[...])
        for j in range(HEADS):
            qt = (rope(r[:, j * LANES:(j + 1) * LANES]) * (QK_DIM ** -0.5 * LOG2_E)).T
            q_ref[j, u, 0] = jnp.where(row < QK_DIM, qt, zero).astype(BF16)
            q_ref[j, u, 1] = jnp.where(row >= QK_DIM, qt, zero).astype(BF16)
            k_ref[j, rows, :] = rope(r[:, hw + j * LANES:hw + (j + 1) * LANES]).astype(BF16)
            vt_ref[j, :, rows] = r[:, 2 * hw + j * LANES:2 * hw + (j + 1) * LANES].T.astype(BF16)
        fw = uf_ref.shape[-1]
        uf_ref[rows, :] = r[:, 3 * hw:3 * hw + fw].astype(BF16)
        uc_ref[rows, :] = r[:, 3 * hw + fw:]


def _inproj(x, mod, g, w, rope_tables, layer, fixed_row, fw, cw2, kv_only=False):
    b, n, d = x.shape
    n_sub = ROW_SUBTILES
    bm = n_sub * TM
    assert n % bm == 0
    use_rope = rope_tables is not None
    in_specs = [
        pl.BlockSpec((None, bm, d), lambda i, j: (i, j, 0)),
        _mod_spec(mod, layer, fixed_row),
        _resident(g, layer),
        _resident(w, layer),
    ]
    operands = [x, mod, g, w]
    if use_rope:
        in_specs += [pl.BlockSpec((bm, LANES), lambda i, j: (j, 0))] * 2
        operands += list(rope_tables)
    out_specs = [
        pl.BlockSpec((None, HEADS, n_sub, 2, LANES, TM), lambda i, j: (i, 0, j, 0, 0, 0)),
        pl.BlockSpec((None, HEADS, bm, LANES), lambda i, j: (i, 0, j, 0)),
        pl.BlockSpec((None, HEADS, V_DIM, bm), lambda i, j: (i, 0, 0, j)),
        pl.BlockSpec((None, bm, fw), lambda i, j: (i, j, 0)),
        pl.BlockSpec((None, bm, cw2), lambda i, j: (i, j, 0)),
    ]
    out_shape = [
        jax.ShapeDtypeStruct((b, HEADS, n // TM, 2, LANES, TM), BF16),
        jax.ShapeDtypeStruct((b, HEADS, n, LANES), BF16),
        jax.ShapeDtypeStruct((b, HEADS, V_DIM, n), BF16),
        jax.ShapeDtypeStruct((b, n, fw), BF16),
        jax.ShapeDtypeStruct((b, n, cw2), F32),
    ]
    if kv_only:
        out_specs, out_shape = out_specs[1:3], out_shape[1:3]
    return pl.pallas_call(
        functools.partial(_inproj_kernel, use_rope=use_rope, kv_only=kv_only, n_sub=n_sub),
        grid=(b, n // bm),
        in_specs=in_specs,
        out_specs=out_specs,
        out_shape=out_shape,
        compiler_params=_params(("arbitrary", "arbitrary"), 48),
        name="inproj",
    )(*operands)


def _attn_kernel(*refs, lam_init, with_ctx):
    lq1_ref, lk1_ref, lq2_ref, lk2_ref, g_ref, kc_ref, vtc_ref, ql_ref, kl_ref, vtl_ref = refs[:10]
    refs = refs[10:]
    if with_ctx:
        qc_ref, oc_ref, ol_ref, s_ref = refs
    else:
        ol_ref, s_ref = refs
    lam = (jnp.exp(jnp.sum(lq1_ref[...] * lk1_ref[...], axis=-1, keepdims=True))
           - jnp.exp(jnp.sum(lq2_ref[...] * lk2_ref[...], axis=-1, keepdims=True)) + lam_init)

    def fold(x):
        return x.reshape(x.shape[0] // SUBLANES, SUBLANES, x.shape[1])

    def finish(o_ref, t, ot, l1):
        ot = ot * (1.0 / l1)
        ms = jnp.mean(ot * ot, axis=0, keepdims=True)
        on = (ot * lax.rsqrt(ms + EPS)).T * g_ref[...] * (1.0 - lam_init)
        o_ref[t * TM:(t + 1) * TM, :] = on.astype(BF16)

    def pipeline(q_ref, o_ref, key_refs):
        chunks = [(k_ref, vt_ref, c) for k_ref, vt_ref in key_refs for c in range(0, k_ref.shape[0], ATT_KC)]
        n_tiles = q_ref.shape[0]
        maxima, sums = {}, {}
        for u in range(n_tiles + 2):
            ta, tb, tc = u, u - 1, u - 2
            do_a, do_b, do_c = ta < n_tiles, 0 <= tb < n_tiles, 0 <= tc < n_tiles
            if do_a:
                qts = [q_ref[ta, mp] for mp in range(2)]
                m8 = [None, None]
            if do_b:
                mb = maxima.pop(tb)
                l8 = [jnp.zeros((SUBLANES, TM), F32) for _ in range(2)]
            if do_c:
                lc = sums.pop(tc)
                rho = lam * lc[0] / lc[1]
                ot = None
            for ci, (k_ref, vt_ref, c) in enumerate(chunks):
                rows = slice(ci * ATT_KC, (ci + 1) * ATT_KC)
                if do_a:
                    for mp in range(2):
                        s = _dot(k_ref[c:c + ATT_KC, :], qts[mp])
                        s_ref[ta % 3, mp, rows, :] = s
                        cm = jnp.max(fold(s), axis=0)
                        m8[mp] = cm if m8[mp] is None else jnp.maximum(m8[mp], cm)
                if do_b:
                    for mp in range(2):
                        e = jnp.exp2(s_ref[tb % 3, mp, rows, :] - mb[mp])
                        s_ref[tb % 3, mp, rows, :] = e
                        l8[mp] = l8[mp] + jnp.sum(fold(e), axis=0)
                if do_c:
                    p = (s_ref[tc % 3, 0, rows, :] - rho * s_ref[tc % 3, 1, rows, :]).astype(BF16)
                    part = _dot(vt_ref[:, c:c + ATT_KC], p)
                    ot = part if ot is None else ot + part
            if do_a:
                maxima[ta] = [jnp.max(m, axis=0, keepdims=True) for m in m8]
            if do_b:
                sums[tb] = [jnp.sum(l, axis=0, keepdims=True) for l in l8]
            if do_c:
                finish(o_ref, tc, ot, lc[0])

    if with_ctx:
        pipeline(qc_ref, oc_ref, [(kc_ref, vtc_ref)])
    pipeline(ql_ref, ol_ref, [(kc_ref, vtc_ref), (kl_ref, vtl_ref)])


def _attention(qkv_ctx, qkv_lat, lam_vecs, g, layer, lam_init, with_ctx):
    q_lat, k_lat, vt_lat = qkv_lat
    b, h, n_lat, _ = k_lat.shape
    n_ctx = qkv_ctx[-2].shape[2] // b
    lat_tiles, ctx_tiles = n_lat // TM, n_ctx // TM
    in_specs = [_resident(v, layer) for v in lam_vecs] + [
        _resident(g, layer),
        pl.BlockSpec((None, None, n_ctx, LANES), lambda i, j: (0, j, i, 0)),
        pl.BlockSpec((None, None, V_DIM, n_ctx), lambda i, j: (0, j, 0, i)),
        pl.BlockSpec((None, None, lat_tiles, 2, LANES, TM), lambda i, j: (i, j, 0, 0, 0, 0)),
        pl.BlockSpec((None, None, n_lat, LANES), lambda i, j: (i, j, 0, 0)),
        pl.BlockSpec((None, None, V_DIM, n_lat), lambda i, j: (i, j, 0, 0)),
    ]
    operands = list(lam_vecs) + [g, qkv_ctx[-2], qkv_ctx[-1], q_lat, k_lat, vt_lat]
    out_specs = [pl.BlockSpec((None, n_lat, V_DIM), lambda i, j: (i, 0, j))]
    out_shape = [jax.ShapeDtypeStruct((b, n_lat, h * V_DIM), BF16)]
    if with_ctx:
        in_specs.append(pl.BlockSpec((None, None, ctx_tiles, 2, LANES, TM), lambda i, j: (0, j, i, 0, 0, 0)))
        operands.append(qkv_ctx[0])
        out_specs.insert(0, pl.BlockSpec((None, n_ctx, V_DIM), lambda i, j: (i, 0, j)))
        out_shape.insert(0, jax.ShapeDtypeStruct((b, n_ctx, h * V_DIM), BF16))
    return pl.pallas_call(
        functools.partial(_attn_kernel, lam_init=lam_init, with_ctx=with_ctx),
        grid=(b, h),
        in_specs=in_specs,
        out_specs=out_specs,
        out_shape=out_shape,
        scratch_shapes=[pltpu.VMEM((3, 2, n_ctx + n_lat, TM), F32)],
        compiler_params=_params(("arbitrary", "arbitrary"), 48),
        name="attention",
    )(*operands)


def _fourier_kernel(u_ref, chan_c_ref, chan_s_ref, wf_ref, pos_c_ref, pos_s_ref, y_ref):
    n, width = u_ref.shape
    u = u_ref[...]
    a = _dot(pos_c_ref[...], u).astype(BF16)
    b = _dot(pos_s_ref[...], u).astype(BF16)
    f = (_dot(a, chan_c_ref[...]) - _dot(b, chan_s_ref[...])) * ((n * width // FOURIER_GROUPS) ** -0.5)
    y_ref[...] = _dot(f.astype(BF16), wf_ref[...]).astype(BF16)


def _fourier(uf, chan_c, chan_s, wf_bd, layer, pos_tables):
    b, n, fw = uf.shape
    row_spec = pl.BlockSpec((None, n, fw), lambda i: (i, 0, 0))
    return pl.pallas_call(
        _fourier_kernel,
        grid=(b,),
        in_specs=[row_spec, _resident(chan_c), _resident(chan_s), _resident(wf_bd, layer)]
                 + [_resident(t) for t in pos_tables],
        out_specs=row_spec,
        out_shape=jax.ShapeDtypeStruct((b, n, fw), BF16),
        compiler_params=_params(("arbitrary",), 48),
        name="fourier",
    )(uf, chan_c, chan_s, wf_bd, *pos_tables)


def _split_dot(x, m):
    hi = x.astype(BF16)
    lo = (x - hi.astype(F32)).astype(BF16)
    return _dot(hi, m) + _dot(lo, m)


def _conv_kernel(u_ref, w_ref, b_ref, lg_ref, lb_ref, avg_ref, pw_ref, y_ref, zpad_ref, acc_ref):
    n, cw = y_ref.shape
    halo = jnp.zeros((CONV_HALO, cw), F32)

    def taps(i, slot):
        r0 = pl.multiple_of(i * CONV_CHUNK, CONV_CHUNK)
        for lo in range(0, cw, LANES):
            win = zpad_ref[pl.ds(r0, CONV_CHUNK + 2 * CONV_HALO), lo:lo + LANES]
            part = jnp.zeros((CONV_CHUNK, LANES), F32) + b_ref[:, lo:lo + LANES]
            for shift in range(SUBLANES):
                rolled = win if shift == 0 else pltpu.roll(win, win.shape[0] - shift, 0)
                for aligned in range(0, 2 * CONV_HALO, SUBLANES):
                    tap = aligned + shift - (CONV_HALO - CONV_K // 2)
                    if 0 <= tap < CONV_K:
                        part = part + rolled[aligned:aligned + CONV_CHUNK] * w_ref[tap:tap + 1, lo:lo + LANES]
            acc_ref[slot, :, lo:lo + LANES] = part

    def project(i, slot):
        acc = acc_ref[slot]
        mu = _split_dot(acc, avg_ref[...])
        dev = acc - mu
        var = _split_dot(dev * dev, avg_ref[...])
        zn = dev * lax.rsqrt(var + EPS) * lg_ref[...] + lb_ref[...]
        act = (zn * _sigmoid(zn)).astype(BF16)
        r0 = pl.multiple_of(i * CONV_CHUNK, CONV_CHUNK)
        y_ref[pl.ds(r0, CONV_CHUNK), :] = _dot(act, pw_ref[...]).astype(BF16)

    zpad_ref[0:CONV_HALO, :] = halo
    zpad_ref[CONV_HALO:CONV_HALO + n, :] = u_ref[:, 0:cw] * _sigmoid(u_ref[:, cw:2 * cw])
    zpad_ref[CONV_HALO + n:2 * CONV_HALO + n, :] = halo

    chunks = n // CONV_CHUNK
    assert chunks % 2 == 0

    def pair(j, carry):
        taps(2 * j + 1, 1)
        project(2 * j, 0)
        taps(2 * j + 2, 0)
        project(2 * j + 1, 1)
        return carry

    taps(0, 0)
    lax.fori_loop(0, chunks // 2 - 1, pair, 0)
    taps(chunks - 1, 1)
    project(chunks - 2, 0)
    project(chunks - 1, 1)


def _conv(uc, conv_w, conv_b, ln_g, ln_b, avg, w_pw, layer):
    b, n, cw2 = uc.shape
    cw = cw2 // 2
    return pl.pallas_call(
        _conv_kernel,
        grid=(b,),
        in_specs=[
            pl.BlockSpec((None, n, cw2), lambda i: (i, 0, 0)),
            _resident(conv_w, layer), _resident(conv_b, layer), _resident(ln_g, layer), _resident(ln_b, layer),
            _resident(avg), _resident(w_pw, layer),
        ],
        out_specs=pl.BlockSpec((None, n, cw), lambda i: (i, 0, 0)),
        out_shape=jax.ShapeDtypeStruct((b, n, cw), BF16),
        scratch_shapes=[pltpu.VMEM((n + 2 * CONV_HALO, cw), F32), pltpu.VMEM((2, CONV_CHUNK, cw), F32)],
        compiler_params=_params(("arbitrary",), 40),
        name="conv",
    )(uc, conv_w, conv_b, ln_g, ln_b, avg, w_pw)


def _ffn_kernel(x_ref, o_ref, yf_ref, yc_ref, mod_ref, g_ref, wout_ref, w1_ref, w3_ref, w2_ref, fg_ref, out_ref,
                *, final, n_sub):
    aw = o_ref.shape[-1]
    fw = yf_ref.shape[-1]
    for u in range(n_sub):
        rows = slice(u * TM, (u + 1) * TM)
        y = (_dot(o_ref[rows, :], wout_ref[0:aw, :]) + _dot(yf_ref[rows, :], wout_ref[aw:aw + fw, :])
             + _dot(yc_ref[rows, :], wout_ref[aw + fw:, :]))
        x1 = x_ref[rows, :] + mod_ref[2:3, :] * y
        h = (_rms(x1) * g_ref[...] * (1.0 + mod_ref[4:5, :]) + mod_ref[3:4, :]).astype(BF16)
        a = _dot(h, w1_ref[...])
        gated = (a * _sigmoid(a) * _dot(h, w3_ref[...])).astype(BF16)
        x2 = x1 + mod_ref[5:6, :] * _dot(gated, w2_ref[...])
        if final:
            x2 = _rms(x2) * fg_ref[...]
        out_ref[rows, :] = x2


def _out_ffn(x, o, yf, yc, mod, g2, w_out, w1, w3, w2, final_g, layer, fixed_row, final):
    b, n, d = x.shape
    n_sub = ROW_SUBTILES
    bm = n_sub * TM
    assert n % bm == 0

    def rows(width):
        return pl.BlockSpec((None, bm, width), lambda i, j: (i, j, 0))

    return pl.pallas_call(
        functools.partial(_ffn_kernel, final=final, n_sub=n_sub),
        grid=(b, n // bm),
        in_specs=[
            rows(d), rows(o.shape[-1]), rows(yf.shape[-1]), rows(yc.shape[-1]),
            _mod_spec(mod, layer, fixed_row),
            _resident(g2, layer), _resident(w_out, layer), _resident(w1, layer), _resident(w3, layer),
            _resident(w2, layer), _resident(final_g),
        ],
        out_specs=rows(d),
        out_shape=jax.ShapeDtypeStruct((b, n, d), F32),
        compiler_params=_params(("arbitrary", "arbitrary"), 58),
        name="out_ffn",
    )(x, o, yf, yc, mod, g2, w_out, w1, w3, w2, final_g)


def _rope_tables(n_lat):
    n_freq = QK_DIM // 4
    tok = np.arange(n_lat)
    inv_freq = np.float32(ROPE_BASE) ** (-np.arange(n_freq, dtype=np.float32) / np.float32(n_freq))
    ang_r = ((tok // GRID_W).astype(np.float32)[:, None] * inv_freq).astype(np.float64)
    ang_c = ((tok % GRID_W).astype(np.float32)[:, None] * inv_freq).astype(np.float64)
    cos = np.concatenate([np.cos(ang_r)] * 2 + [np.cos(ang_c)] * 2, axis=-1)
    sin = np.concatenate([-np.sin(ang_r), np.sin(ang_r), -np.sin(ang_c), np.sin(ang_c)], axis=-1)
    reps = LANES // QK_DIM
    return (jnp.asarray(np.tile(cos, (1, reps)), dtype=F32), jnp.asarray(np.tile(sin, (1, reps)), dtype=F32))


def _dft_angles(rows, n):
    return 2.0 * np.pi * ((np.asarray(rows, np.int64)[:, None] * np.arange(n, dtype=np.int64)[None, :]) % n) / n


def _dft_tables_small(n):
    ang = _dft_angles(np.arange(n), n)
    return np.cos(ang).astype(np.float32), np.sin(ang).astype(np.float32)


def _dft_tables(n):
    if n <= 4 * DFT_LO:
        return [jnp.asarray(t).astype(BF16) for t in _dft_tables_small(n)]
    hi = n // DFT_LO
    ang_hi = _dft_angles(np.arange(hi) * DFT_LO, n)
    ang_lo = _dft_angles(np.arange(DFT_LO), n)
    ch, sh = (jnp.asarray(f(ang_hi), dtype=F32)[:, None, :] for f in (np.cos, np.sin))
    cl, sl = (jnp.asarray(f(ang_lo), dtype=F32)[None, :, :] for f in (np.cos, np.sin))
    return [(ch * cl - sh * sl).reshape(n, n).astype(BF16), (sh * cl + ch * sl).reshape(n, n).astype(BF16)]


def _block_diag(blocks):
    g, r, c = blocks.shape[-3:]
    eye = jnp.eye(g, dtype=blocks.dtype)
    out = eye[:, None, :, None] * blocks[..., :, :, None, :]
    return out.reshape(blocks.shape[:-3] + (g * r, g * c))


def _permute_inproj(w_in, qk_w):
    lead = w_in.shape[:-1]
    qk = w_in[..., :4 * qk_w].reshape(lead + (2, 2, HEADS, QK_DIM))
    qk = jnp.swapaxes(qk, -3, -2).reshape(lead + (4 * qk_w,))
    return jnp.concatenate([qk, w_in[..., 4 * qk_w:]], axis=-1)


def kernel(x, c, ctx, c_ctx, w_ada, b_ada, norm1_g, norm2_g, w_in, lam_q1, lam_k1, lam_q2, lam_k2, subln_g,
           w_fourier, conv_w, conv_b, conv_ln_g, conv_ln_b, w_conv_out, w_out, w_ffn1, w_ffn3, w_ffn2, final_g):
    b, n_lat, d = x.shape
    n_ctx = ctx.shape[1]
    depth = w_ada.shape[0]
    fw = w_fourier.shape[1] * w_fourier.shape[2]
    cw = conv_w.shape[-1]
    qk_w = HEADS * QK_DIM
    attn_w = HEADS * V_DIM
    assert n_lat % GRID_W == 0 and n_ctx % TM == 0
    assert w_in.shape[-1] == 4 * qk_w + attn_w + fw + 2 * cw

    pad = (-(b + 1)) % SUBLANES
    c_rows = jnp.concatenate([c, c_ctx[None, :], jnp.zeros((pad, d), c.dtype)], axis=0)
    mod = _ada(c_rows, w_ada, b_ada)
    mod = mod.reshape(depth, mod.shape[1], 6, d)
    ctx_row = b

    rope = _rope_tables(n_lat)
    cc, cs = _dft_tables_small(fw // FOURIER_GROUPS)
    eye = np.eye(FOURIER_GROUPS, dtype=np.float32)
    chan_c = jnp.asarray(np.kron(eye, cc)).astype(BF16)
    chan_s = jnp.asarray(np.kron(eye, cs)).astype(BF16)
    lat_tables = _dft_tables(n_lat)
    ctx_tables = _dft_tables(n_ctx)
    group = cw // CONV_GROUPS
    avg = jnp.asarray(np.kron(np.eye(CONV_GROUPS), np.full((group, group), 1.0 / group)), dtype=F32).astype(BF16)

    w_in_b = _permute_inproj(w_in, qk_w).astype(BF16)
    wf_bd = _block_diag(w_fourier).astype(BF16)
    w_pw_b, w_out_b = w_conv_out.astype(BF16), w_out.astype(BF16)
    w1_b, w3_b, w2_b = w_ffn1.astype(BF16), w_ffn3.astype(BF16), w_ffn2.astype(BF16)

    def per_layer_rows(a):
        return a.reshape(depth, 1, a.shape[-1])

    norm1, norm2, subln = per_layer_rows(norm1_g), per_layer_rows(norm2_g), per_layer_rows(subln_g)
    lam_vecs = [per_layer_rows(a) for a in (lam_q1, lam_k1, lam_q2, lam_k2)]
    cb, lg, lb = per_layer_rows(conv_b), per_layer_rows(conv_ln_g), per_layer_rows(conv_ln_b)
    fg = final_g[None, :]

    def mixers(u_f, u_c, tables, l):
        yf = _fourier(u_f, chan_c, chan_s, wf_bd, l, tables)
        yc = _conv(u_c, conv_w, cb, lg, lb, avg, w_pw_b, l)
        return yf, yc

    ctx_flat = ctx.reshape(1, b * n_ctx, d)
    for l in range(depth):
        last = l == depth - 1
        lam_init = 0.8 - 0.6 * math.exp(-0.3 * l)
        q, k, vt, uf, uc = _inproj(x, mod, norm1, w_in_b, rope, l, None, fw, 2 * cw)
        yf, yc = mixers(uf, uc, lat_tables, l)
        if last:
            kv_ctx = _inproj(ctx_flat, mod, norm1, w_in_b, None, l, ctx_row, fw, 2 * cw, kv_only=True)
            (o,) = _attention(kv_ctx, (q, k, vt), lam_vecs, subln, l, lam_init, with_ctx=False)
        else:
            qc, kc, vtc, ufc, ucc = _inproj(ctx_flat, mod, norm1, w_in_b, None, l, ctx_row, fw, 2 * cw)
            oc, o = _attention((qc, kc, vtc), (q, k, vt), lam_vecs, subln, l, lam_init, with_ctx=True)
            yfc, ycc = mixers(ufc.reshape(b, n_ctx, fw), ucc.reshape(b, n_ctx, 2 * cw), ctx_tables, l)
            ctx_flat = _out_ffn(ctx_flat, oc.reshape(1, b * n_ctx, attn_w), yfc.reshape(1, b * n_ctx, fw),
                                ycc.reshape(1, b * n_ctx, cw), mod, norm2, w_out_b, w1_b, w3_b, w2_b, fg, l,
                                ctx_row, final=False)
        x = _out_ffn(x, o, yf, yc, mod, norm2, w_out_b, w1_b, w3_b, w2_b, fg, l, None, final=last)
    return x
```

```python
import functools
import math

import numpy as np
import jax
import jax.numpy as jnp
from jax import lax
from jax.experimental import pallas as pl
from jax.experimental.pallas import tpu as pltpu

F32 = jnp.float32
BF16 = jnp.bfloat16

GRID_W = 64
HEADS = 4
QK_DIM = 64
V_DIM = 2 * QK_DIM
FOURIER_GROUPS = 4
CONV_GROUPS = 4
CONV_K = 31
ROPE_BASE = 10000.0
EPS = 1e-6
LOG2_E = math.log2(math.e)
BOUND_SLACK = 1.0 + 2.0 ** -6
MIN_SOFTMAX_DENOMINATOR = 2.0 ** -90

LANES = 128
SUBLANES = 8
TM = 256
ROW_SUBTILES = 2
ATT_KC = 256
CONV_HALO = 16
CONV_CHUNK = 128
ADA_TN = 1024
DFT_LO = 64
MIB = 1024 * 1024


def _params(sem, vmem_mib):
    return pltpu.CompilerParams(dimension_semantics=sem, vmem_limit_bytes=vmem_mib * MIB)


def _resident(arr, layer=None):
    if layer is None:
        idx = (0,) * arr.ndim
        return pl.BlockSpec(arr.shape, lambda *_: idx, pipeline_mode=pl.Buffered(1))
    idx = (layer,) + (0,) * (arr.ndim - 1)
    return pl.BlockSpec((None,) + arr.shape[1:], lambda *_: idx, pipeline_mode=pl.Buffered(1))


def _mod_spec(mod, layer, fixed_row):
    if fixed_row is None:
        return pl.BlockSpec((None, None) + mod.shape[2:], lambda i, j: (layer, i, 0, 0))
    return pl.BlockSpec((None, None) + mod.shape[2:], lambda i, j: (layer, fixed_row, 0, 0))


def _sigmoid(x):
    return 1.0 / (1.0 + jnp.exp(-x))


def _rms(x):
    return x * lax.rsqrt(jnp.mean(x * x, axis=-1, keepdims=True) + EPS)


def _dot(a, b):
    return jnp.dot(a, b, preferred_element_type=F32)


def _ada_kernel(c_ref, w_ref, b_ref, o_ref):
    c = c_ref[...]
    s = (c * _sigmoid(c)).astype(BF16)
    o_ref[...] = _dot(s, w_ref[...].astype(BF16)) + b_ref[...]


def _ada(c_rows, w_ada, b_ada):
    depth, d, n = w_ada.shape
    rows = c_rows.shape[0]
    return pl.pallas_call(
        _ada_kernel,
        grid=(depth, n // ADA_TN),
        in_specs=[
            pl.BlockSpec((rows, d), lambda l, j: (0, 0)),
            pl.BlockSpec((None, d, ADA_TN), lambda l, j: (l, 0, j)),
            pl.BlockSpec((None, 1, ADA_TN), lambda l, j: (l, 0, j)),
        ],
        out_specs=pl.BlockSpec((None, rows, ADA_TN), lambda l, j: (l, 0, j)),
        out_shape=jax.ShapeDtypeStruct((depth, rows, n), F32),
        compiler_params=_params(("arbitrary", "arbitrary"), 24),
        name="ada",
    )(c_rows, w_ada, b_ada.reshape(depth, 1, n))


def _inproj_kernel(*refs, use_rope, kv_only, n_sub):
    x_ref, mod_ref, g_ref, w_ref = refs[:4]
    refs = refs[4:]
    if use_rope:
        cos_ref, sin_ref = refs[:2]
        refs = refs[2:]
    hw = HEADS * LANES
    lane = lax.broadcasted_iota(jnp.int32, (TM, LANES), 1)
    first_half = (lane & (QK_DIM // 4)) == 0
    row = lax.broadcasted_iota(jnp.int32, (LANES, TM), 0)
    zero = jnp.zeros((LANES, TM), F32)
    dim = lax.broadcasted_iota(jnp.int32, (LANES, LANES), 0)
    col = lax.broadcasted_iota(jnp.int32, (LANES, LANES), 1)
    map_sum = jnp.where(col == jnp.where(dim < QK_DIM, 0, 1), 1.0, 0.0).astype(BF16)

    for u in range(n_sub):
        rows = slice(u * TM, (u + 1) * TM)
        h = (_rms(x_ref[rows, :]) * g_ref[...] * (1.0 + mod_ref[1:2, :]) + mod_ref[0:1, :]).astype(BF16)

        def rope(t):
            if not use_rope:
                return t
            partner = jnp.where(first_half, pltpu.roll(t, LANES - QK_DIM // 4, 1), pltpu.roll(t, QK_DIM // 4, 1))
            return t * cos_ref[rows, :] + partner * sin_ref[rows, :]

        def put_keys(j, kf):
            k_ref[j, rows, :] = kf.astype(BF16)
            norms = _dot((kf * kf).astype(BF16), map_sum)
            kn_ref[j, u] = jnp.max(norms.reshape(TM // SUBLANES, SUBLANES, LANES), axis=0)

        if kv_only:
            k_ref, vt_ref, kn_ref = refs
            r = _dot(h, w_ref[:, hw:3 * hw])
            for j in range(HEADS):
                put_keys(j, rope(r[:, j * LANES:(j + 1) * LANES]))
                vt_ref[j, :, rows] = r[:, hw + j * LANES:hw + (j + 1) * LANES].T.astype(BF16)
            continue

        q_ref, k_ref, vt_ref, kn_ref, uf_ref, uc_ref = refs
        r = _dot(h, w_ref[...])
        for j in range(HEADS):
            qt = (rope(r[:, j * LANES:(j + 1) * LANES]) * (QK_DIM ** -0.5 * LOG2_E)).T
            for mp in range(2):
                sel = (row < QK_DIM) == (mp == 0)
                qm = qt[mp * QK_DIM:(mp + 1) * QK_DIM, :]
                qn = jnp.sqrt(jnp.sum(qm * qm, axis=0, keepdims=True)) * BOUND_SLACK
                q_ref[j, u, mp, 0:LANES, :] = jnp.where(sel, qt, zero).astype(BF16)
                q_ref[j, u, mp, LANES:2 * LANES, :] = jnp.where(row == 0, qn, zero).astype(BF16)
            put_keys(j, rope(r[:, hw + j * LANES:hw + (j + 1) * LANES]))
            vt_ref[j, :, rows] = r[:, 2 * hw + j * LANES:2 * hw + (j + 1) * LANES].T.astype(BF16)
        fw = uf_ref.shape[-1]
        uf_ref[rows, :] = r[:, 3 * hw:3 * hw + fw].astype(BF16)
        uc_ref[rows, :] = r[:, 3 * hw + fw:]


def _inproj(x, mod, g, w, rope_tables, layer, fixed_row, fw, cw2, kv_only=False):
    b, n, d = x.shape
    n_sub = ROW_SUBTILES
    bm = n_sub * TM
    assert n % bm == 0
    use_rope = rope_tables is not None
    in_specs = [
        pl.BlockSpec((None, bm, d), lambda i, j: (i, j, 0)),
        _mod_spec(mod, layer, fixed_row),
        _resident(g, layer),
        _resident(w, layer),
    ]
    operands = [x, mod, g, w]
    if use_rope:
        in_specs += [pl.BlockSpec((bm, LANES), lambda i, j: (j, 0))] * 2
        operands += list(rope_tables)
    out_specs = [
        pl.BlockSpec((None, HEADS, n_sub, 2, 2 * LANES, TM), lambda i, j: (i, 0, j, 0, 0, 0)),
        pl.BlockSpec((None, HEADS, bm, LANES), lambda i, j: (i, 0, j, 0)),
        pl.BlockSpec((None, HEADS, V_DIM, bm), lambda i, j: (i, 0, 0, j)),
        pl.BlockSpec((None, HEADS, n_sub, SUBLANES, LANES), lambda i, j: (i, 0, j, 0, 0)),
        pl.BlockSpec((None, bm, fw), lambda i, j: (i, j, 0)),
        pl.BlockSpec((None, bm, cw2), lambda i, j: (i, j, 0)),
    ]
    out_shape = [
        jax.ShapeDtypeStruct((b, HEADS, n // TM, 2, 2 * LANES, TM), BF16),
        jax.ShapeDtypeStruct((b, HEADS, n, LANES), BF16),
        jax.ShapeDtypeStruct((b, HEADS, V_DIM, n), BF16),
        jax.ShapeDtypeStruct((b, HEADS, n // TM, SUBLANES, LANES), F32),
        jax.ShapeDtypeStruct((b, n, fw), BF16),
        jax.ShapeDtypeStruct((b, n, cw2), F32),
    ]
    if kv_only:
        out_specs, out_shape = out_specs[1:4], out_shape[1:4]
    return pl.pallas_call(
        functools.partial(_inproj_kernel, use_rope=use_rope, kv_only=kv_only, n_sub=n_sub),
        grid=(b, n // bm),
        in_specs=in_specs,
        out_specs=out_specs,
        out_shape=out_shape,
        compiler_params=_params(("arbitrary", "arbitrary"), 48),
        name="inproj",
    )(*operands)


def _attn_kernel(*refs, lam_init, with_ctx, bounded):
    lq1_ref, lk1_ref, lq2_ref, lk2_ref, g_ref, kc_ref, vtc_ref, knc_ref, ql_ref, kl_ref, vtl_ref, knl_ref = refs[:12]
    refs = list(refs[12:])
    s_ref = refs.pop()
    qc_ref = refs.pop(0) if with_ctx else None
    oc_ref = refs.pop(0) if with_ctx else None
    ol_ref = refs.pop(0)
    lmin_ref = refs.pop(0) if bounded else None
    lam = (jnp.exp(jnp.sum(lq1_ref[...] * lk1_ref[...], axis=-1, keepdims=True))
           - jnp.exp(jnp.sum(lq2_ref[...] * lk2_ref[...], axis=-1, keepdims=True)) + lam_init)

    def fold(x):
        return x.reshape(x.shape[0] // SUBLANES, SUBLANES, x.shape[1])

    def finish(o_ref, t, ot, l1):
        ot = ot * (1.0 / l1)
        ms = jnp.mean(ot * ot, axis=0, keepdims=True)
        on = (ot * lax.rsqrt(ms + EPS)).T * g_ref[...] * (1.0 - lam_init)
        o_ref[t * TM:(t + 1) * TM, :] = on.astype(BF16)

    def key_chunks(key_refs):
        return [(k_ref, vt_ref, c) for k_ref, vt_ref, _ in key_refs for c in range(0, k_ref.shape[0], ATT_KC)]

    def pipeline_bounded(q_ref, o_ref, key_refs):
        chunks = key_chunks(key_refs)
        n_tiles = q_ref.shape[0]
        lane = lax.broadcasted_iota(jnp.int32, (ATT_KC, LANES), 1)
        key_cols = []
        ksq = functools.reduce(jnp.maximum, [jnp.max(kn_ref[...], axis=0) for _, _, kn_ref in key_refs])
        kb = jnp.sqrt(jnp.max(ksq, axis=0, keepdims=True)) * BOUND_SLACK
        for mp in range(2):
            key_cols.append(jnp.where(lane == 0, -kb[:, mp:mp + 1], 0.0).astype(BF16))
        sums = {}
        lmin = None
        for u in range(n_tiles + 1):
            ta, tc = u, u - 1
            do_a, do_c = ta < n_tiles, 0 <= tc < n_tiles
            if do_a:
                qts = [q_ref[ta, mp] for mp in range(2)]
                l8 = [jnp.zeros((SUBLANES, TM), F32) for _ in range(2)]
            if do_c:
                lc = sums.pop(tc)
                rho = lam * lc[0] / lc[1]
                ot = None
            for ci, (k_ref, vt_ref, c) in enumerate(chunks):
                rows = slice(ci * ATT_KC, (ci + 1) * ATT_KC)
                if do_a:
                    for mp in range(2):
                        keys = jnp.concatenate([k_ref[c:c + ATT_KC, :], key_cols[mp]], axis=1)
                        e = jnp.exp2(_dot(keys, qts[mp]))
                        s_ref[ta % 2, mp, rows, :] = e
                        l8[mp] = l8[mp] + jnp.sum(fold(e), axis=0)
                if do_c:
                    p = (s_ref[tc % 2, 0, rows, :] - rho * s_ref[tc % 2, 1, rows, :]).astype(BF16)
                    part = _dot(vt_ref[:, c:c + ATT_KC], p)
                    ot = part if ot is None else ot + part
            if do_a:
                sums[ta] = [jnp.sum(l, axis=0, keepdims=True) for l in l8]
                low = jnp.minimum(sums[ta][0], sums[ta][1])
                lmin = low if lmin is None else jnp.minimum(lmin, low)
            if do_c:
                finish(o_ref, tc, ot, lc[0])
        return jnp.min(lmin, axis=1, keepdims=True)

    def pipeline(q_ref, o_ref, key_refs):
        chunks = key_chunks(key_refs)
        n_tiles = q_ref.shape[0]
        maxima, sums = {}, {}
        for u in range(n_tiles + 2):
            ta, tb, tc = u, u - 1, u - 2
            do_a, do_b, do_c = ta < n_tiles, 0 <= tb < n_tiles, 0 <= tc < n_tiles
            if do_a:
                qts = [q_ref[ta, mp, 0:LANES, :] for mp in range(2)]
                m8 = [None, None]
            if do_b:
                mb = maxima.pop(tb)
                l8 = [jnp.zeros((SUBLANES, TM), F32) for _ in range(2)]
            if do_c:
                lc = sums.pop(tc)
                rho = lam * lc[0] / lc[1]
                ot = None
            for ci, (k_ref, vt_ref, c) in enumerate(chunks):
                rows = slice(ci * ATT_KC, (ci + 1) * ATT_KC)
                if do_a:
                    for mp in range(2):
                        s = _dot(k_ref[c:c + ATT_KC, :], qts[mp])
                        s_ref[ta % 3, mp, rows, :] = s
                        cm = jnp.max(fold(s), axis=0)
                        m8[mp] = cm if m8[mp] is None else jnp.maximum(m8[mp], cm)
                if do_b:
                    for mp in range(2):
                        e = jnp.exp2(s_ref[tb % 3, mp, rows, :] - mb[mp])
                        s_ref[tb % 3, mp, rows, :] = e
                        l8[mp] = l8[mp] + jnp.sum(fold(e), axis=0)
                if do_c:
                    p = (s_ref[tc % 3, 0, rows, :] - rho * s_ref[tc % 3, 1, rows, :]).astype(BF16)
                    part = _dot(vt_ref[:, c:c + ATT_KC], p)
                    ot = part if ot is None else ot + part
            if do_a:
                maxima[ta] = [jnp.max(m, axis=0, keepdims=True) for m in m8]
            if do_b:
                sums[tb] = [jnp.sum(l, axis=0, keepdims=True) for l in l8]
            if do_c:
                finish(o_ref, tc, ot, lc[0])

    run = pipeline_bounded if bounded else pipeline
    ctx_keys, lat_keys = (kc_ref, vtc_ref, knc_ref), (kl_ref, vtl_ref, knl_ref)
    lmin = run(ql_ref, ol_ref, [ctx_keys, lat_keys])
    if with_ctx:
        lmin_ctx = run(qc_ref, oc_ref, [ctx_keys])
    if bounded:
        if with_ctx:
            lmin = jnp.minimum(lmin, lmin_ctx)
        lmin_ref[...] = jnp.broadcast_to(lmin, lmin_ref.shape)


def _attention_call(qkv_ctx, qkv_lat, lam_vecs, g, layer, lam_init, with_ctx, bounded):
    q_lat, k_lat, vt_lat, kn_lat = qkv_lat
    k_ctx, vt_ctx, kn_ctx = qkv_ctx[-3:]
    b, h, n_lat, _ = k_lat.shape
    n_ctx = k_ctx.shape[2] // b
    lat_tiles, ctx_tiles = n_lat // TM, n_ctx // TM

    def tiles_spec(tiles, tail, ctx):
        zeros = (0,) * len(tail)
        if ctx:
            return pl.BlockSpec((None, None, tiles) + tail, lambda i, j: (0, j, i) + zeros)
        return pl.BlockSpec((None, None, tiles) + tail, lambda i, j: (i, j, 0) + zeros)

    q_tile, kn_tile = (2, 2 * LANES, TM), (SUBLANES, LANES)
    in_specs = [_resident(v, layer) for v in lam_vecs] + [
        _resident(g, layer),
        pl.BlockSpec((None, None, n_ctx, LANES), lambda i, j: (0, j, i, 0)),
        pl.BlockSpec((None, None, V_DIM, n_ctx), lambda i, j: (0, j, 0, i)),
        tiles_spec(ctx_tiles, kn_tile, True),
        tiles_spec(lat_tiles, q_tile, False),
        pl.BlockSpec((None, None, n_lat, LANES), lambda i, j: (i, j, 0, 0)),
        pl.BlockSpec((None, None, V_DIM, n_lat), lambda i, j: (i, j, 0, 0)),
        tiles_spec(lat_tiles, kn_tile, False),
    ]
    operands = list(lam_vecs) + [g, k_ctx, vt_ctx, kn_ctx, q_lat, k_lat, vt_lat, kn_lat]
    out_specs = [pl.BlockSpec((None, n_lat, V_DIM), lambda i, j: (i, 0, j))]
    out_shape = [jax.ShapeDtypeStruct((b, n_lat, h * V_DIM), BF16)]
    if with_ctx:
        in_specs.append(tiles_spec(ctx_tiles, q_tile, True))
        operands.append(qkv_ctx[0])
        out_specs.insert(0, pl.BlockSpec((None, n_ctx, V_DIM), lambda i, j: (i, 0, j)))
        out_shape.insert(0, jax.ShapeDtypeStruct((b, n_ctx, h * V_DIM), BF16))
    if bounded:
        out_specs.append(pl.BlockSpec((None, None, SUBLANES, LANES), lambda i, j: (i, j, 0, 0)))
        out_shape.append(jax.ShapeDtypeStruct((b, h, SUBLANES, LANES), F32))
    return pl.pallas_call(
        functools.partial(_attn_kernel, lam_init=lam_init, with_ctx=with_ctx, bounded=bounded),
        grid=(b, h),
        in_specs=in_specs,
        out_specs=out_specs,
        out_shape=out_shape,
        scratch_shapes=[pltpu.VMEM((2 if bounded else 3, 2, n_ctx + n_lat, TM), F32)],
        compiler_params=_params(("arbitrary", "arbitrary"), 48),
        name="attention_bounded" if bounded else "attention_exact",
    )(*operands)


def _attention(qkv_ctx, qkv_lat, lam_vecs, g, layer, lam_init, with_ctx):
    args = (qkv_ctx, qkv_lat, lam_vecs, g, layer, lam_init, with_ctx)
    *outs, lmin = _attention_call(*args, bounded=True)
    safe = jnp.min(lmin) >= MIN_SOFTMAX_DENOMINATOR
    return lax.cond(safe, lambda: tuple(outs), lambda: tuple(_attention_call(*args, bounded=False)))


def _fourier_kernel(u_ref, chan_c_ref, chan_s_ref, wf_ref, pos_c_ref, pos_s_ref, y_ref):
    n, width = u_ref.shape
    u = u_ref[...]
    a = _dot(pos_c_ref[...], u).astype(BF16)
    b = _dot(pos_s_ref[...], u).astype(BF16)
    f = (_dot(a, chan_c_ref[...]) - _dot(b, chan_s_ref[...])) * ((n * width // FOURIER_GROUPS) ** -0.5)
    y_ref[...] = _dot(f.astype(BF16), wf_ref[...]).astype(BF16)


def _fourier(uf, chan_c, chan_s, wf_bd, layer, pos_tables):
    b, n, fw = uf.shape
    row_spec = pl.BlockSpec((None, n, fw), lambda i: (i, 0, 0))
    return pl.pallas_call(
        _fourier_kernel,
        grid=(b,),
        in_specs=[row_spec, _resident(chan_c), _resident(chan_s), _resident(wf_bd, layer)]
                 + [_resident(t) for t in pos_tables],
        out_specs=row_spec,
        out_shape=jax.ShapeDtypeStruct((b, n, fw), BF16),
        compiler_params=_params(("arbitrary",), 48),
        name="fourier",
    )(uf, chan_c, chan_s, wf_bd, *pos_tables)


def _split_dot(x, m):
    hi = x.astype(BF16)
    lo = (x - hi.astype(F32)).astype(BF16)
    return _dot(hi, m) + _dot(lo, m)


def _conv_kernel(u_ref, w_ref, b_ref, lg_ref, lb_ref, avg_ref, pw_ref, y_ref, zpad_ref, acc_ref):
    n, cw = y_ref.shape
    halo = jnp.zeros((CONV_HALO, cw), F32)

    def taps(i, slot):
        r0 = pl.multiple_of(i * CONV_CHUNK, CONV_CHUNK)
        for lo in range(0, cw, LANES):
            win = zpad_ref[pl.ds(r0, CONV_CHUNK + 2 * CONV_HALO), lo:lo + LANES]
            part = jnp.zeros((CONV_CHUNK, LANES), F32) + b_ref[:, lo:lo + LANES]
            for shift in range(SUBLANES):
                rolled = win if shift == 0 else pltpu.roll(win, win.shape[0] - shift, 0)
                for aligned in range(0, 2 * CONV_HALO, SUBLANES):
                    tap = aligned + shift - (CONV_HALO - CONV_K // 2)
                    if 0 <= tap < CONV_K:
                        part = part + rolled[aligned:aligned + CONV_CHUNK] * w_ref[tap:tap + 1, lo:lo + LANES]
            acc_ref[slot, :, lo:lo + LANES] = part

    def project(i, slot):
        acc = acc_ref[slot]
        mu = _split_dot(acc, avg_ref[...])
        dev = acc - mu
        var = _split_dot(dev * dev, avg_ref[...])
        zn = dev * lax.rsqrt(var + EPS) * lg_ref[...] + lb_ref[...]
        act = (zn * _sigmoid(zn)).astype(BF16)
        r0 = pl.multiple_of(i * CONV_CHUNK, CONV_CHUNK)
        y_ref[pl.ds(r0, CONV_CHUNK), :] = _dot(act, pw_ref[...]).astype(BF16)

    zpad_ref[0:CONV_HALO, :] = halo
    zpad_ref[CONV_HALO:CONV_HALO + n, :] = u_ref[:, 0:cw] * _sigmoid(u_ref[:, cw:2 * cw])
    zpad_ref[CONV_HALO + n:2 * CONV_HALO + n, :] = halo

    chunks = n // CONV_CHUNK
    assert chunks % 2 == 0

    def pair(j, carry):
        taps(2 * j + 1, 1)
        project(2 * j, 0)
        taps(2 * j + 2, 0)
        project(2 * j + 1, 1)
        return carry

    taps(0, 0)
    lax.fori_loop(0, chunks // 2 - 1, pair, 0)
    taps(chunks - 1, 1)
    project(chunks - 2, 0)
    project(chunks - 1, 1)


def _conv(uc, conv_w, conv_b, ln_g, ln_b, avg, w_pw, layer):
    b, n, cw2 = uc.shape
    cw = cw2 // 2
    return pl.pallas_call(
        _conv_kernel,
        grid=(b,),
        in_specs=[
            pl.BlockSpec((None, n, cw2), lambda i: (i, 0, 0)),
            _resident(conv_w, layer), _resident(conv_b, layer), _resident(ln_g, layer), _resident(ln_b, layer),
            _resident(avg), _resident(w_pw, layer),
        ],
        out_specs=pl.BlockSpec((None, n, cw), lambda i: (i, 0, 0)),
        out_shape=jax.ShapeDtypeStruct((b, n, cw), BF16),
        scratch_shapes=[pltpu.VMEM((n + 2 * CONV_HALO, cw), F32), pltpu.VMEM((2, CONV_CHUNK, cw), F32)],
        compiler_params=_params(("arbitrary",), 40),
        name="conv",
    )(uc, conv_w, conv_b, ln_g, ln_b, avg, w_pw)


def _ffn_kernel(x_ref, o_ref, yf_ref, yc_ref, mod_ref, g_ref, wout_ref, w1_ref, w3_ref, w2_ref, fg_ref, out_ref,
                *, final, n_sub):
    aw = o_ref.shape[-1]
    fw = yf_ref.shape[-1]
    for u in range(n_sub):
        rows = slice(u * TM, (u + 1) * TM)
        y = (_dot(o_ref[rows, :], wout_ref[0:aw, :]) + _dot(yf_ref[rows, :], wout_ref[aw:aw + fw, :])
             + _dot(yc_ref[rows, :], wout_ref[aw + fw:, :]))
        x1 = x_ref[rows, :] + mod_ref[2:3, :] * y
        h = (_rms(x1) * g_ref[...] * (1.0 + mod_ref[4:5, :]) + mod_ref[3:4, :]).astype(BF16)
        a = _dot(h, w1_ref[...])
        gated = (a * _sigmoid(a) * _dot(h, w3_ref[...])).astype(BF16)
        x2 = x1 + mod_ref[5:6, :] * _dot(gated, w2_ref[...])
        if final:
            x2 = _rms(x2) * fg_ref[...]
        out_ref[rows, :] = x2


def _out_ffn(x, o, yf, yc, mod, g2, w_out, w1, w3, w2, final_g, layer, fixed_row, final):
    b, n, d = x.shape
    n_sub = ROW_SUBTILES
    bm = n_sub * TM
    assert n % bm == 0

    def rows(width):
        return pl.BlockSpec((None, bm, width), lambda i, j: (i, j, 0))

    return pl.pallas_call(
        functools.partial(_ffn_kernel, final=final, n_sub=n_sub),
        grid=(b, n // bm),
        in_specs=[
            rows(d), rows(o.shape[-1]), rows(yf.shape[-1]), rows(yc.shape[-1]),
            _mod_spec(mod, layer, fixed_row),
            _resident(g2, layer), _resident(w_out, layer), _resident(w1, layer), _resident(w3, layer),
            _resident(w2, layer), _resident(final_g),
        ],
        out_specs=rows(d),
        out_shape=jax.ShapeDtypeStruct((b, n, d), F32),
        compiler_params=_params(("arbitrary", "arbitrary"), 58),
        name="out_ffn",
    )(x, o, yf, yc, mod, g2, w_out, w1, w3, w2, final_g)


def _rope_tables(n_lat):
    n_freq = QK_DIM // 4
    tok = np.arange(n_lat)
    inv_freq = np.float32(ROPE_BASE) ** (-np.arange(n_freq, dtype=np.float32) / np.float32(n_freq))
    ang_r = ((tok // GRID_W).astype(np.float32)[:, None] * inv_freq).astype(np.float64)
    ang_c = ((tok % GRID_W).astype(np.float32)[:, None] * inv_freq).astype(np.float64)
    cos = np.concatenate([np.cos(ang_r)] * 2 + [np.cos(ang_c)] * 2, axis=-1)
    sin = np.concatenate([-np.sin(ang_r), np.sin(ang_r), -np.sin(ang_c), np.sin(ang_c)], axis=-1)
    reps = LANES // QK_DIM
    return (jnp.asarray(np.tile(cos, (1, reps)), dtype=F32), jnp.asarray(np.tile(sin, (1, reps)), dtype=F32))


def _dft_angles(rows, n):
    return 2.0 * np.pi * ((np.asarray(rows, np.int64)[:, None] * np.arange(n, dtype=np.int64)[None, :]) % n) / n


def _dft_tables_small(n):
    ang = _dft_angles(np.arange(n), n)
    return np.cos(ang).astype(np.float32), np.sin(ang).astype(np.float32)


def _dft_tables(n):
    if n <= 4 * DFT_LO:
        return [jnp.asarray(t).astype(BF16) for t in _dft_tables_small(n)]
    hi = n // DFT_LO
    ang_hi = _dft_angles(np.arange(hi) * DFT_LO, n)
    ang_lo = _dft_angles(np.arange(DFT_LO), n)
    ch, sh = (jnp.asarray(f(ang_hi), dtype=F32)[:, None, :] for f in (np.cos, np.sin))
    cl, sl = (jnp.asarray(f(ang_lo), dtype=F32)[None, :, :] for f in (np.cos, np.sin))
    return [(ch * cl - sh * sl).reshape(n, n).astype(BF16), (sh * cl + ch * sl).reshape(n, n).astype(BF16)]


def _block_diag(blocks):
    g, r, c = blocks.shape[-3:]
    eye = jnp.eye(g, dtype=blocks.dtype)
    out = eye[:, None, :, None] * blocks[..., :, :, None, :]
    return out.reshape(blocks.shape[:-3] + (g * r, g * c))


def _permute_inproj(w_in, qk_w):
    lead = w_in.shape[:-1]
    qk = w_in[..., :4 * qk_w].reshape(lead + (2, 2, HEADS, QK_DIM))
    qk = jnp.swapaxes(qk, -3, -2).reshape(lead + (4 * qk_w,))
    return jnp.concatenate([qk, w_in[..., 4 * qk_w:]], axis=-1)


def kernel(x, c, ctx, c_ctx, w_ada, b_ada, norm1_g, norm2_g, w_in, lam_q1, lam_k1, lam_q2, lam_k2, subln_g,
           w_fourier, conv_w, conv_b, conv_ln_g, conv_ln_b, w_conv_out, w_out, w_ffn1, w_ffn3, w_ffn2, final_g):
    b, n_lat, d = x.shape
    n_ctx = ctx.shape[1]
    depth = w_ada.shape[0]
    fw = w_fourier.shape[1] * w_fourier.shape[2]
    cw = conv_w.shape[-1]
    qk_w = HEADS * QK_DIM
    attn_w = HEADS * V_DIM
    assert n_lat % GRID_W == 0 and n_ctx % TM == 0
    assert w_in.shape[-1] == 4 * qk_w + attn_w + fw + 2 * cw

    pad = (-(b + 1)) % SUBLANES
    c_rows = jnp.concatenate([c, c_ctx[None, :], jnp.zeros((pad, d), c.dtype)], axis=0)
    mod = _ada(c_rows, w_ada, b_ada)
    mod = mod.reshape(depth, mod.shape[1], 6, d)
    ctx_row = b

    rope = _rope_tables(n_lat)
    cc, cs = _dft_tables_small(fw // FOURIER_GROUPS)
    eye = np.eye(FOURIER_GROUPS, dtype=np.float32)
    chan_c = jnp.asarray(np.kron(eye, cc)).astype(BF16)
    chan_s = jnp.asarray(np.kron(eye, cs)).astype(BF16)
    lat_tables = _dft_tables(n_lat)
    ctx_tables = _dft_tables(n_ctx)
    group = cw // CONV_GROUPS
    avg = jnp.asarray(np.kron(np.eye(CONV_GROUPS), np.full((group, group), 1.0 / group)), dtype=F32).astype(BF16)

    w_in_b = _permute_inproj(w_in, qk_w).astype(BF16)
    wf_bd = _block_diag(w_fourier).astype(BF16)
    w_pw_b, w_out_b = w_conv_out.astype(BF16), w_out.astype(BF16)
    w1_b, w3_b, w2_b = w_ffn1.astype(BF16), w_ffn3.astype(BF16), w_ffn2.astype(BF16)

    def per_layer_rows(a):
        return a.reshape(depth, 1, a.shape[-1])

    norm1, norm2, subln = per_layer_rows(norm1_g), per_layer_rows(norm2_g), per_layer_rows(subln_g)
    lam_vecs = [per_layer_rows(a) for a in (lam_q1, lam_k1, lam_q2, lam_k2)]
    cb, lg, lb = per_layer_rows(conv_b), per_layer_rows(conv_ln_g), per_layer_rows(conv_ln_b)
    fg = final_g[None, :]

    def mixers(u_f, u_c, tables, l):
        yf = _fourier(u_f, chan_c, chan_s, wf_bd, l, tables)
        yc = _conv(u_c, conv_w, cb, lg, lb, avg, w_pw_b, l)
        return yf, yc

    ctx_flat = ctx.reshape(1, b * n_ctx, d)
    for l in range(depth):
        last = l == depth - 1
        lam_init = 0.8 - 0.6 * math.exp(-0.3 * l)
        q, k, vt, kn, uf, uc = _inproj(x, mod, norm1, w_in_b, rope, l, None, fw, 2 * cw)
        yf, yc = mixers(uf, uc, lat_tables, l)
        if last:
            kv_ctx = _inproj(ctx_flat, mod, norm1, w_in_b, None, l, ctx_row, fw, 2 * cw, kv_only=True)
            (o,) = _attention(kv_ctx, (q, k, vt, kn), lam_vecs, subln, l, lam_init, with_ctx=False)
        else:
            qc, kc, vtc, knc, ufc, ucc = _inproj(ctx_flat, mod, norm1, w_in_b, None, l, ctx_row, fw, 2 * cw)
            oc, o = _attention((qc, kc, vtc, knc), (q, k, vt, kn), lam_vecs, subln, l, lam_init, with_ctx=True)
            yfc, ycc = mixers(ufc.reshape(b, n_ctx, fw), ucc.reshape(b, n_ctx, 2 * cw), ctx_tables, l)
            ctx_flat = _out_ffn(ctx_flat, oc.reshape(1, b * n_ctx, attn_w), yfc.reshape(1, b * n_ctx, fw),
                                ycc.reshape(1, b * n_ctx, cw), mod, norm2, w_out_b, w1_b, w3_b, w2_b, fg, l,
                                ctx_row, final=False)
        x = _out_ffn(x, o, yf, yc, mod, norm2, w_out_b, w1_b, w3_b, w2_b, fg, l, None, final=last)
    return x
```

```python
import functools
import math

import numpy as np
import jax
import jax.numpy as jnp
from jax import lax
from jax.experimental import pallas as pl
from jax.experimental.pallas import tpu as pltpu

F32 = jnp.float32
BF16 = jnp.bfloat16

GRID_W = 64
HEADS = 4
QK_DIM = 64
V_DIM = 2 * QK_DIM
FOURIER_GROUPS = 4
CONV_GROUPS = 4
CONV_K = 31
ROPE_BASE = 10000.0
EPS = 1e-6
LOG2_E = math.log2(math.e)
BOUND_SLACK = 1.0 + 2.0 ** -6
MIN_SOFTMAX_DENOMINATOR = 2.0 ** -90

LANES = 128
SUBLANES = 8
TM = 256
ROW_SUBTILES = 2
ATT_KC = 256
CONV_HALO = 16
CONV_CHUNK = 128
ADA_TN = 1024
DFT_LO = 64
MIB = 1024 * 1024


def _params(sem, vmem_mib):
    return pltpu.CompilerParams(dimension_semantics=sem, vmem_limit_bytes=vmem_mib * MIB)


def _resident(arr, layer=None):
    if layer is None:
        idx = (0,) * arr.ndim
        return pl.BlockSpec(arr.shape, lambda *_: idx, pipeline_mode=pl.Buffered(1))
    idx = (layer,) + (0,) * (arr.ndim - 1)
    return pl.BlockSpec((None,) + arr.shape[1:], lambda *_: idx, pipeline_mode=pl.Buffered(1))


def _mod_spec(mod, layer, fixed_row):
    if fixed_row is None:
        return pl.BlockSpec((None, None) + mod.shape[2:], lambda i, j: (layer, i, 0, 0))
    return pl.BlockSpec((None, None) + mod.shape[2:], lambda i, j: (layer, fixed_row, 0, 0))


def _sigmoid(x):
    return 1.0 / (1.0 + jnp.exp(-x))


def _rms(x):
    return x * lax.rsqrt(jnp.mean(x * x, axis=-1, keepdims=True) + EPS)


def _dot(a, b):
    return jnp.dot(a, b, preferred_element_type=F32)


def _ada_kernel(c_ref, w_ref, b_ref, o_ref):
    c = c_ref[...]
    s = (c * _sigmoid(c)).astype(BF16)
    o_ref[...] = _dot(s, w_ref[...].astype(BF16)) + b_ref[...]


def _ada(c_rows, w_ada, b_ada):
    depth, d, n = w_ada.shape
    rows = c_rows.shape[0]
    return pl.pallas_call(
        _ada_kernel,
        grid=(depth, n // ADA_TN),
        in_specs=[
            pl.BlockSpec((rows, d), lambda l, j: (0, 0)),
            pl.BlockSpec((None, d, ADA_TN), lambda l, j: (l, 0, j)),
            pl.BlockSpec((None, 1, ADA_TN), lambda l, j: (l, 0, j)),
        ],
        out_specs=pl.BlockSpec((None, rows, ADA_TN), lambda l, j: (l, 0, j)),
        out_shape=jax.ShapeDtypeStruct((depth, rows, n), F32),
        compiler_params=_params(("arbitrary", "arbitrary"), 24),
        name="ada",
    )(c_rows, w_ada, b_ada.reshape(depth, 1, n))


def _inproj_kernel(*refs, use_rope, kv_only, n_sub):
    x_ref, mod_ref, g_ref, w_ref = refs[:4]
    refs = refs[4:]
    if use_rope:
        cos_ref, sin_ref = refs[:2]
        refs = refs[2:]
    hw = HEADS * LANES
    lane = lax.broadcasted_iota(jnp.int32, (TM, LANES), 1)
    first_half = (lane & (QK_DIM // 4)) == 0
    row = lax.broadcasted_iota(jnp.int32, (LANES, TM), 0)
    zero = jnp.zeros((LANES, TM), F32)
    dim = lax.broadcasted_iota(jnp.int32, (LANES, LANES), 0)
    col = lax.broadcasted_iota(jnp.int32, (LANES, LANES), 1)
    map_sum = jnp.where(col == jnp.where(dim < QK_DIM, 0, 1), 1.0, 0.0).astype(BF16)

    for u in range(n_sub):
        rows = slice(u * TM, (u + 1) * TM)
        h = (_rms(x_ref[rows, :]) * g_ref[...] * (1.0 + mod_ref[1:2, :]) + mod_ref[0:1, :]).astype(BF16)

        def rope(t):
            if not use_rope:
                return t
            partner = jnp.where(first_half, pltpu.roll(t, LANES - QK_DIM // 4, 1), pltpu.roll(t, QK_DIM // 4, 1))
            return t * cos_ref[rows, :] + partner * sin_ref[rows, :]

        def put_keys(j, kf):
            k_ref[j, rows, :] = kf.astype(BF16)
            norms = _dot((kf * kf).astype(BF16), map_sum)
            kn_ref[j, u] = jnp.max(norms.reshape(TM // SUBLANES, SUBLANES, LANES), axis=0)

        if kv_only:
            k_ref, vt_ref, kn_ref = refs
            r = _dot(h, w_ref[:, hw:3 * hw])
            for j in range(HEADS):
                put_keys(j, rope(r[:, j * LANES:(j + 1) * LANES]))
                vt_ref[j, :, rows] = r[:, hw + j * LANES:hw + (j + 1) * LANES].T.astype(BF16)
            continue

        q_ref, k_ref, vt_ref, kn_ref, uf_ref, uc_ref = refs
        r = _dot(h, w_ref[...])
        for j in range(HEADS):
            qt = (rope(r[:, j * LANES:(j + 1) * LANES]) * (QK_DIM ** -0.5 * LOG2_E)).T
            for mp in range(2):
                sel = (row < QK_DIM) == (mp == 0)
                qm = qt[mp * QK_DIM:(mp + 1) * QK_DIM, :]
                qn = jnp.sqrt(jnp.sum(qm * qm, axis=0, keepdims=True)) * BOUND_SLACK
                q_ref[j, u, mp, 0:LANES, :] = jnp.where(sel, qt, zero).astype(BF16)
                q_ref[j, u, mp, LANES:2 * LANES, :] = jnp.where(row == 0, qn, zero).astype(BF16)
            put_keys(j, rope(r[:, hw + j * LANES:hw + (j + 1) * LANES]))
            vt_ref[j, :, rows] = r[:, 2 * hw + j * LANES:2 * hw + (j + 1) * LANES].T.astype(BF16)
        fw = uf_ref.shape[-1]
        uf_ref[rows, :] = r[:, 3 * hw:3 * hw + fw].astype(BF16)
        uc_ref[rows, :] = r[:, 3 * hw + fw:]


def _inproj(x, mod, g, w, rope_tables, layer, fixed_row, fw, cw2, kv_only=False):
    b, n, d = x.shape
    n_sub = ROW_SUBTILES
    bm = n_sub * TM
    assert n % bm == 0
    use_rope = rope_tables is not None
    in_specs = [
        pl.BlockSpec((None, bm, d), lambda i, j: (i, j, 0)),
        _mod_spec(mod, layer, fixed_row),
        _resident(g, layer),
        _resident(w, layer),
    ]
    operands = [x, mod, g, w]
    if use_rope:
        in_specs += [pl.BlockSpec((bm, LANES), lambda i, j: (j, 0))] * 2
        operands += list(rope_tables)
    out_specs = [
        pl.BlockSpec((None, HEADS, n_sub, 2, 2 * LANES, TM), lambda i, j: (i, 0, j, 0, 0, 0)),
        pl.BlockSpec((None, HEADS, bm, LANES), lambda i, j: (i, 0, j, 0)),
        pl.BlockSpec((None, HEADS, V_DIM, bm), lambda i, j: (i, 0, 0, j)),
        pl.BlockSpec((None, HEADS, n_sub, SUBLANES, LANES), lambda i, j: (i, 0, j, 0, 0)),
        pl.BlockSpec((None, bm, fw), lambda i, j: (i, j, 0)),
        pl.BlockSpec((None, bm, cw2), lambda i, j: (i, j, 0)),
    ]
    out_shape = [
        jax.ShapeDtypeStruct((b, HEADS, n // TM, 2, 2 * LANES, TM), BF16),
        jax.ShapeDtypeStruct((b, HEADS, n, LANES), BF16),
        jax.ShapeDtypeStruct((b, HEADS, V_DIM, n), BF16),
        jax.ShapeDtypeStruct((b, HEADS, n // TM, SUBLANES, LANES), F32),
        jax.ShapeDtypeStruct((b, n, fw), BF16),
        jax.ShapeDtypeStruct((b, n, cw2), F32),
    ]
    if kv_only:
        out_specs, out_shape = out_specs[1:4], out_shape[1:4]
    return pl.pallas_call(
        functools.partial(_inproj_kernel, use_rope=use_rope, kv_only=kv_only, n_sub=n_sub),
        grid=(b, n // bm),
        in_specs=in_specs,
        out_specs=out_specs,
        out_shape=out_shape,
        compiler_params=_params(("arbitrary", "arbitrary"), 48),
        name="inproj",
    )(*operands)


def _attn_kernel(*refs, lam_init, with_ctx, bounded):
    lq1_ref, lk1_ref, lq2_ref, lk2_ref, g_ref, kc_ref, vtc_ref, knc_ref, ql_ref, kl_ref, vtl_ref, knl_ref = refs[:12]
    refs = list(refs[12:])
    s_ref = refs.pop()
    qc_ref = refs.pop(0) if with_ctx else None
    oc_ref = refs.pop(0) if with_ctx else None
    ol_ref = refs.pop(0)
    lmin_ref = refs.pop(0) if bounded else None
    lam = (jnp.exp(jnp.sum(lq1_ref[...] * lk1_ref[...], axis=-1, keepdims=True))
           - jnp.exp(jnp.sum(lq2_ref[...] * lk2_ref[...], axis=-1, keepdims=True)) + lam_init)

    def fold(x):
        return x.reshape(x.shape[0] // SUBLANES, SUBLANES, x.shape[1])

    def finish(o_ref, t, ot, l1):
        ot = ot * (1.0 / l1)
        ms = jnp.mean(ot * ot, axis=0, keepdims=True)
        on = (ot * lax.rsqrt(ms + EPS)).T * g_ref[...] * (1.0 - lam_init)
        o_ref[t * TM:(t + 1) * TM, :] = on.astype(BF16)

    def key_chunks(key_refs):
        return [(k_ref, vt_ref, c) for k_ref, vt_ref, _ in key_refs for c in range(0, k_ref.shape[0], ATT_KC)]

    def pipeline_bounded(q_ref, o_ref, key_refs):
        chunks = key_chunks(key_refs)
        n_tiles = q_ref.shape[0]
        lane = lax.broadcasted_iota(jnp.int32, (ATT_KC, LANES), 1)
        key_cols = []
        ksq = functools.reduce(jnp.maximum, [jnp.max(kn_ref[...], axis=0) for _, _, kn_ref in key_refs])
        kb = jnp.sqrt(jnp.max(ksq, axis=0, keepdims=True)) * BOUND_SLACK
        for mp in range(2):
            key_cols.append(jnp.where(lane == 0, -kb[:, mp:mp + 1], 0.0).astype(BF16))
        sums = {}
        lmin = None
        for u in range(n_tiles + 1):
            ta, tc = u, u - 1
            do_a, do_c = ta < n_tiles, 0 <= tc < n_tiles
            if do_a:
                qts = [q_ref[ta, mp] for mp in range(2)]
                l8 = [jnp.zeros((SUBLANES, TM), F32) for _ in range(2)]
            if do_c:
                lc = sums.pop(tc)
                rho = lam * lc[0] / lc[1]
                ot = None
            for ci, (k_ref, vt_ref, c) in enumerate(chunks):
                rows = slice(ci * ATT_KC, (ci + 1) * ATT_KC)
                if do_a:
                    for mp in range(2):
                        keys = jnp.concatenate([k_ref[c:c + ATT_KC, :], key_cols[mp]], axis=1)
                        e = jnp.exp2(_dot(keys, qts[mp]))
                        s_ref[ta % 2, mp, rows, :] = e
                        l8[mp] = l8[mp] + jnp.sum(fold(e), axis=0)
                if do_c:
                    p = (s_ref[tc % 2, 0, rows, :] - rho * s_ref[tc % 2, 1, rows, :]).astype(BF16)
                    part = _dot(vt_ref[:, c:c + ATT_KC], p)
                    ot = part if ot is None else ot + part
            if do_a:
                sums[ta] = [jnp.sum(l, axis=0, keepdims=True) for l in l8]
                low = jnp.minimum(sums[ta][0], sums[ta][1])
                lmin = low if lmin is None else jnp.minimum(lmin, low)
            if do_c:
                finish(o_ref, tc, ot, lc[0])
        return jnp.min(lmin, axis=1, keepdims=True)

    def pipeline(q_ref, o_ref, key_refs):
        chunks = key_chunks(key_refs)
        n_tiles = q_ref.shape[0]
        maxima, sums = {}, {}
        for u in range(n_tiles + 2):
            ta, tb, tc = u, u - 1, u - 2
            do_a, do_b, do_c = ta < n_tiles, 0 <= tb < n_tiles, 0 <= tc < n_tiles
            if do_a:
                qts = [q_ref[ta, mp, 0:LANES, :] for mp in range(2)]
                m8 = [None, None]
            if do_b:
                mb = maxima.pop(tb)
                l8 = [jnp.zeros((SUBLANES, TM), F32) for _ in range(2)]
            if do_c:
                lc = sums.pop(tc)
                rho = lam * lc[0] / lc[1]
                ot = None
            for ci, (k_ref, vt_ref, c) in enumerate(chunks):
                rows = slice(ci * ATT_KC, (ci + 1) * ATT_KC)
                if do_a:
                    for mp in range(2):
                        s = _dot(k_ref[c:c + ATT_KC, :], qts[mp])
                        s_ref[ta % 3, mp, rows, :] = s
                        cm = jnp.max(fold(s), axis=0)
                        m8[mp] = cm if m8[mp] is None else jnp.maximum(m8[mp], cm)
                if do_b:
                    for mp in range(2):
                        e = jnp.exp2(s_ref[tb % 3, mp, rows, :] - mb[mp])
                        s_ref[tb % 3, mp, rows, :] = e
                        l8[mp] = l8[mp] + jnp.sum(fold(e), axis=0)
                if do_c:
                    p = (s_ref[tc % 3, 0, rows, :] - rho * s_ref[tc % 3, 1, rows, :]).astype(BF16)
                    part = _dot(vt_ref[:, c:c + ATT_KC], p)
                    ot = part if ot is None else ot + part
            if do_a:
                maxima[ta] = [jnp.max(m, axis=0, keepdims=True) for m in m8]
            if do_b:
                sums[tb] = [jnp.sum(l, axis=0, keepdims=True) for l in l8]
            if do_c:
                finish(o_ref, tc, ot, lc[0])

    run = pipeline_bounded if bounded else pipeline
    ctx_keys, lat_keys = (kc_ref, vtc_ref, knc_ref), (kl_ref, vtl_ref, knl_ref)
    lmin = run(ql_ref, ol_ref, [ctx_keys, lat_keys])
    if with_ctx:
        lmin_ctx = run(qc_ref, oc_ref, [ctx_keys])
    if bounded:
        if with_ctx:
            lmin = jnp.minimum(lmin, lmin_ctx)
        lmin_ref[...] = jnp.broadcast_to(lmin, lmin_ref.shape)


def _attention_call(qkv_ctx, qkv_lat, lam_vecs, g, layer, lam_init, with_ctx, bounded):
    q_lat, k_lat, vt_lat, kn_lat = qkv_lat
    k_ctx, vt_ctx, kn_ctx = qkv_ctx[-3:]
    b, h, n_lat, _ = k_lat.shape
    n_ctx = k_ctx.shape[2] // b
    lat_tiles, ctx_tiles = n_lat // TM, n_ctx // TM

    def tiles_spec(tiles, tail, ctx):
        zeros = (0,) * len(tail)
        if ctx:
            return pl.BlockSpec((None, None, tiles) + tail, lambda i, j: (0, j, i) + zeros)
        return pl.BlockSpec((None, None, tiles) + tail, lambda i, j: (i, j, 0) + zeros)

    q_tile, kn_tile = (2, 2 * LANES, TM), (SUBLANES, LANES)
    in_specs = [_resident(v, layer) for v in lam_vecs] + [
        _resident(g, layer),
        pl.BlockSpec((None, None, n_ctx, LANES), lambda i, j: (0, j, i, 0)),
        pl.BlockSpec((None, None, V_DIM, n_ctx), lambda i, j: (0, j, 0, i)),
        tiles_spec(ctx_tiles, kn_tile, True),
        tiles_spec(lat_tiles, q_tile, False),
        pl.BlockSpec((None, None, n_lat, LANES), lambda i, j: (i, j, 0, 0)),
        pl.BlockSpec((None, None, V_DIM, n_lat), lambda i, j: (i, j, 0, 0)),
        tiles_spec(lat_tiles, kn_tile, False),
    ]
    operands = list(lam_vecs) + [g, k_ctx, vt_ctx, kn_ctx, q_lat, k_lat, vt_lat, kn_lat]
    out_specs = [pl.BlockSpec((None, n_lat, V_DIM), lambda i, j: (i, 0, j))]
    out_shape = [jax.ShapeDtypeStruct((b, n_lat, h * V_DIM), BF16)]
    if with_ctx:
        in_specs.append(tiles_spec(ctx_tiles, q_tile, True))
        operands.append(qkv_ctx[0])
        out_specs.insert(0, pl.BlockSpec((None, n_ctx, V_DIM), lambda i, j: (i, 0, j)))
        out_shape.insert(0, jax.ShapeDtypeStruct((b, n_ctx, h * V_DIM), BF16))
    if bounded:
        out_specs.append(pl.BlockSpec((None, None, SUBLANES, LANES), lambda i, j: (i, j, 0, 0)))
        out_shape.append(jax.ShapeDtypeStruct((b, h, SUBLANES, LANES), F32))
    return pl.pallas_call(
        functools.partial(_attn_kernel, lam_init=lam_init, with_ctx=with_ctx, bounded=bounded),
        grid=(b, h),
        in_specs=in_specs,
        out_specs=out_specs,
        out_shape=out_shape,
        scratch_shapes=[pltpu.VMEM((2 if bounded else 3, 2, n_ctx + n_lat, TM), F32)],
        compiler_params=_params(("arbitrary", "arbitrary"), 48),
        name="attention_bounded" if bounded else "attention_exact",
    )(*operands)


def _attention(qkv_ctx, qkv_lat, lam_vecs, g, layer, lam_init, with_ctx):
    args = (qkv_ctx, qkv_lat, lam_vecs, g, layer, lam_init, with_ctx)
    *outs, lmin = _attention_call(*args, bounded=True)
    safe = jnp.min(lmin) >= MIN_SOFTMAX_DENOMINATOR
    return lax.cond(safe, lambda: tuple(outs), lambda: tuple(_attention_call(*args, bounded=False)))


def _fourier_kernel(u_ref, chan_c_ref, chan_s_ref, wf_ref, pos_c_ref, pos_s_ref, y_ref):
    n, width = u_ref.shape
    u = u_ref[...]
    a = _dot(pos_c_ref[...], u).astype(BF16)
    b = _dot(pos_s_ref[...], u).astype(BF16)
    f = (_dot(a, chan_c_ref[...]) - _dot(b, chan_s_ref[...])) * ((n * width // FOURIER_GROUPS) ** -0.5)
    y_ref[...] = _dot(f.astype(BF16), wf_ref[...]).astype(BF16)


def _fourier(uf, chan_c, chan_s, wf_bd, layer, pos_tables):
    b, n, fw = uf.shape
    row_spec = pl.BlockSpec((None, n, fw), lambda i: (i, 0, 0))
    return pl.pallas_call(
        _fourier_kernel,
        grid=(b,),
        in_specs=[row_spec, _resident(chan_c), _resident(chan_s), _resident(wf_bd, layer)]
                 + [_resident(t) for t in pos_tables],
        out_specs=row_spec,
        out_shape=jax.ShapeDtypeStruct((b, n, fw), BF16),
        compiler_params=_params(("arbitrary",), 48),
        name="fourier",
    )(uf, chan_c, chan_s, wf_bd, *pos_tables)


def _split_dot(x, m):
    hi = x.astype(BF16)
    lo = (x - hi.astype(F32)).astype(BF16)
    return _dot(hi, m) + _dot(lo, m)


def _conv_kernel(u_ref, w_ref, b_ref, lg_ref, lb_ref, avg_ref, pw_ref, y_ref, zpad_ref, acc_a_ref, acc_b_ref):
    n, cw = y_ref.shape
    halo = jnp.zeros((CONV_HALO, cw), F32)

    def taps(i, acc_ref, slot):
        r0 = pl.multiple_of(i * CONV_CHUNK, CONV_CHUNK)
        for lo in range(0, cw, LANES):
            win = zpad_ref[pl.ds(r0, CONV_CHUNK + 2 * CONV_HALO), lo:lo + LANES]
            part = jnp.zeros((CONV_CHUNK, LANES), F32) + b_ref[:, lo:lo + LANES]
            for shift in range(SUBLANES):
                rolled = win if shift == 0 else pltpu.roll(win, win.shape[0] - shift, 0)
                for aligned in range(0, 2 * CONV_HALO, SUBLANES):
                    tap = aligned + shift - (CONV_HALO - CONV_K // 2)
                    if 0 <= tap < CONV_K:
                        part = part + rolled[aligned:aligned + CONV_CHUNK] * w_ref[tap:tap + 1, lo:lo + LANES]
            acc_ref[slot, :, lo:lo + LANES] = part

    def project(acc_ref, slot):
        acc = acc_ref[slot]
        mu = _split_dot(acc, avg_ref[...])
        dev = acc - mu
        var = _split_dot(dev * dev, avg_ref[...])
        zn = dev * lax.rsqrt(var + EPS) * lg_ref[...] + lb_ref[...]
        act = (zn * _sigmoid(zn)).astype(BF16)
        return _dot(act, pw_ref[...]).astype(BF16)

    def round_(first_tap, tap_ref, ready_ref):
        if first_tap is not None:
            taps(first_tap, tap_ref, 0)
            taps(first_tap + 1, tap_ref, 1)
        return [project(ready_ref, 0), project(ready_ref, 1)]

    def store(first_chunk, ys):
        r0 = pl.multiple_of(first_chunk * CONV_CHUNK, 2 * CONV_CHUNK)
        y_ref[pl.ds(r0, len(ys) * CONV_CHUNK), :] = jnp.concatenate(ys, axis=0)

    zpad_ref[0:CONV_HALO, :] = halo
    zpad_ref[CONV_HALO:CONV_HALO + n, :] = u_ref[:, 0:cw] * _sigmoid(u_ref[:, cw:2 * cw])
    zpad_ref[CONV_HALO + n:2 * CONV_HALO + n, :] = halo

    chunks = n // CONV_CHUNK
    assert chunks == 2 or chunks % 4 == 0

    def step(j, carry):
        c0 = 4 * j
        ys = round_(c0 + 2, acc_b_ref, acc_a_ref) + round_(c0 + 4, acc_a_ref, acc_b_ref)
        store(c0, ys)
        return carry

    taps(0, acc_a_ref, 0)
    taps(1, acc_a_ref, 1)
    if chunks == 2:
        store(0, round_(None, None, acc_a_ref))
    else:
        lax.fori_loop(0, chunks // 4 - 1, step, 0)
        store(chunks - 4, round_(chunks - 2, acc_b_ref, acc_a_ref) + round_(None, None, acc_b_ref))


def _conv(uc, conv_w, conv_b, ln_g, ln_b, avg, w_pw, layer):
    b, n, cw2 = uc.shape
    cw = cw2 // 2
    return pl.pallas_call(
        _conv_kernel,
        grid=(b,),
        in_specs=[
            pl.BlockSpec((None, n, cw2), lambda i: (i, 0, 0)),
            _resident(conv_w, layer), _resident(conv_b, layer), _resident(ln_g, layer), _resident(ln_b, layer),
            _resident(avg), _resident(w_pw, layer),
        ],
        out_specs=pl.BlockSpec((None, n, cw), lambda i: (i, 0, 0)),
        out_shape=jax.ShapeDtypeStruct((b, n, cw), BF16),
        scratch_shapes=[pltpu.VMEM((n + 2 * CONV_HALO, cw), F32), pltpu.VMEM((2, CONV_CHUNK, cw), F32),
                        pltpu.VMEM((2, CONV_CHUNK, cw), F32)],
        compiler_params=_params(("arbitrary",), 40),
        name="conv",
    )(uc, conv_w, conv_b, ln_g, ln_b, avg, w_pw)


def _ffn_kernel(x_ref, o_ref, yf_ref, yc_ref, mod_ref, g_ref, wout_ref, w1_ref, w3_ref, w2_ref, fg_ref, out_ref,
                *, final, n_sub):
    aw = o_ref.shape[-1]
    fw = yf_ref.shape[-1]
    for u in range(n_sub):
        rows = slice(u * TM, (u + 1) * TM)
        y = (_dot(o_ref[rows, :], wout_ref[0:aw, :]) + _dot(yf_ref[rows, :], wout_ref[aw:aw + fw, :])
             + _dot(yc_ref[rows, :], wout_ref[aw + fw:, :]))
        x1 = x_ref[rows, :] + mod_ref[2:3, :] * y
        h = (_rms(x1) * g_ref[...] * (1.0 + mod_ref[4:5, :]) + mod_ref[3:4, :]).astype(BF16)
        a = _dot(h, w1_ref[...])
        gated = (a * _sigmoid(a) * _dot(h, w3_ref[...])).astype(BF16)
        x2 = x1 + mod_ref[5:6, :] * _dot(gated, w2_ref[...])
        if final:
            x2 = _rms(x2) * fg_ref[...]
        out_ref[rows, :] = x2


def _out_ffn(x, o, yf, yc, mod, g2, w_out, w1, w3, w2, final_g, layer, fixed_row, final):
    b, n, d = x.shape
    n_sub = ROW_SUBTILES
    bm = n_sub * TM
    assert n % bm == 0

    def rows(width):
        return pl.BlockSpec((None, bm, width), lambda i, j: (i, j, 0))

    return pl.pallas_call(
        functools.partial(_ffn_kernel, final=final, n_sub=n_sub),
        grid=(b, n // bm),
        in_specs=[
            rows(d), rows(o.shape[-1]), rows(yf.shape[-1]), rows(yc.shape[-1]),
            _mod_spec(mod, layer, fixed_row),
            _resident(g2, layer), _resident(w_out, layer), _resident(w1, layer), _resident(w3, layer),
            _resident(w2, layer), _resident(final_g),
        ],
        out_specs=rows(d),
        out_shape=jax.ShapeDtypeStruct((b, n, d), F32),
        compiler_params=_params(("arbitrary", "arbitrary"), 58),
        name="out_ffn",
    )(x, o, yf, yc, mod, g2, w_out, w1, w3, w2, final_g)


def _rope_tables(n_lat):
    n_freq = QK_DIM // 4
    tok = np.arange(n_lat)
    inv_freq = np.float32(ROPE_BASE) ** (-np.arange(n_freq, dtype=np.float32) / np.float32(n_freq))
    ang_r = ((tok // GRID_W).astype(np.float32)[:, None] * inv_freq).astype(np.float64)
    ang_c = ((tok % GRID_W).astype(np.float32)[:, None] * inv_freq).astype(np.float64)
    cos = np.concatenate([np.cos(ang_r)] * 2 + [np.cos(ang_c)] * 2, axis=-1)
    sin = np.concatenate([-np.sin(ang_r), np.sin(ang_r), -np.sin(ang_c), np.sin(ang_c)], axis=-1)
    reps = LANES // QK_DIM
    return (jnp.asarray(np.tile(cos, (1, reps)), dtype=F32), jnp.asarray(np.tile(sin, (1, reps)), dtype=F32))


def _dft_angles(rows, n):
    return 2.0 * np.pi * ((np.asarray(rows, np.int64)[:, None] * np.arange(n, dtype=np.int64)[None, :]) % n) / n


def _dft_tables_small(n):
    ang = _dft_angles(np.arange(n), n)
    return np.cos(ang).astype(np.float32), np.sin(ang).astype(np.float32)


def _dft_tables(n):
    if n <= 4 * DFT_LO:
        return [jnp.asarray(t).astype(BF16) for t in _dft_tables_small(n)]
    hi = n // DFT_LO
    ang_hi = _dft_angles(np.arange(hi) * DFT_LO, n)
    ang_lo = _dft_angles(np.arange(DFT_LO), n)
    ch, sh = (jnp.asarray(f(ang_hi), dtype=F32)[:, None, :] for f in (np.cos, np.sin))
    cl, sl = (jnp.asarray(f(ang_lo), dtype=F32)[None, :, :] for f in (np.cos, np.sin))
    return [(ch * cl - sh * sl).reshape(n, n).astype(BF16), (sh * cl + ch * sl).reshape(n, n).astype(BF16)]


def _block_diag(blocks):
    g, r, c = blocks.shape[-3:]
    eye = jnp.eye(g, dtype=blocks.dtype)
    out = eye[:, None, :, None] * blocks[..., :, :, None, :]
    return out.reshape(blocks.shape[:-3] + (g * r, g * c))


def _permute_inproj(w_in, qk_w):
    lead = w_in.shape[:-1]
    qk = w_in[..., :4 * qk_w].reshape(lead + (2, 2, HEADS, QK_DIM))
    qk = jnp.swapaxes(qk, -3, -2).reshape(lead + (4 * qk_w,))
    return jnp.concatenate([qk, w_in[..., 4 * qk_w:]], axis=-1)


def kernel(x, c, ctx, c_ctx, w_ada, b_ada, norm1_g, norm2_g, w_in, lam_q1, lam_k1, lam_q2, lam_k2, subln_g,
           w_fourier, conv_w, conv_b, conv_ln_g, conv_ln_b, w_conv_out, w_out, w_ffn1, w_ffn3, w_ffn2, final_g):
    b, n_lat, d = x.shape
    n_ctx = ctx.shape[1]
    depth = w_ada.shape[0]
    fw = w_fourier.shape[1] * w_fourier.shape[2]
    cw = conv_w.shape[-1]
    qk_w = HEADS * QK_DIM
    attn_w = HEADS * V_DIM
    assert n_lat % GRID_W == 0 and n_ctx % TM == 0
    assert w_in.shape[-1] == 4 * qk_w + attn_w + fw + 2 * cw

    pad = (-(b + 1)) % SUBLANES
    c_rows = jnp.concatenate([c, c_ctx[None, :], jnp.zeros((pad, d), c.dtype)], axis=0)
    mod = _ada(c_rows, w_ada, b_ada)
    mod = mod.reshape(depth, mod.shape[1], 6, d)
    ctx_row = b

    rope = _rope_tables(n_lat)
    cc, cs = _dft_tables_small(fw // FOURIER_GROUPS)
    eye = np.eye(FOURIER_GROUPS, dtype=np.float32)
    chan_c = jnp.asarray(np.kron(eye, cc)).astype(BF16)
    chan_s = jnp.asarray(np.kron(eye, cs)).astype(BF16)
    lat_tables = _dft_tables(n_lat)
    ctx_tables = _dft_tables(n_ctx)
    group = cw // CONV_GROUPS
    avg = jnp.asarray(np.kron(np.eye(CONV_GROUPS), np.full((group, group), 1.0 / group)), dtype=F32).astype(BF16)

    w_in_b = _permute_inproj(w_in, qk_w).astype(BF16)
    wf_bd = _block_diag(w_fourier).astype(BF16)
    w_pw_b, w_out_b = w_conv_out.astype(BF16), w_out.astype(BF16)
    w1_b, w3_b, w2_b = w_ffn1.astype(BF16), w_ffn3.astype(BF16), w_ffn2.astype(BF16)

    def per_layer_rows(a):
        return a.reshape(depth, 1, a.shape[-1])

    norm1, norm2, subln = per_layer_rows(norm1_g), per_layer_rows(norm2_g), per_layer_rows(subln_g)
    lam_vecs = [per_layer_rows(a) for a in (lam_q1, lam_k1, lam_q2, lam_k2)]
    cb, lg, lb = per_layer_rows(conv_b), per_layer_rows(conv_ln_g), per_layer_rows(conv_ln_b)
    fg = final_g[None, :]

    def mixers(u_f, u_c, tables, l):
        yf = _fourier(u_f, chan_c, chan_s, wf_bd, l, tables)
        yc = _conv(u_c, conv_w, cb, lg, lb, avg, w_pw_b, l)
        return yf, yc

    ctx_flat = ctx.reshape(1, b * n_ctx, d)
    for l in range(depth):
        last = l == depth - 1
        lam_init = 0.8 - 0.6 * math.exp(-0.3 * l)
        q, k, vt, kn, uf, uc = _inproj(x, mod, norm1, w_in_b, rope, l, None, fw, 2 * cw)
        yf, yc = mixers(uf, uc, lat_tables, l)
        if last:
            kv_ctx = _inproj(ctx_flat, mod, norm1, w_in_b, None, l, ctx_row, fw, 2 * cw, kv_only=True)
            (o,) = _attention(kv_ctx, (q, k, vt, kn), lam_vecs, subln, l, lam_init, with_ctx=False)
        else:
            qc, kc, vtc, knc, ufc, ucc = _inproj(ctx_flat, mod, norm1, w_in_b, None, l, ctx_row, fw, 2 * cw)
            oc, o = _attention((qc, kc, vtc, knc), (q, k, vt, kn), lam_vecs, subln, l, lam_init, with_ctx=True)
            yfc, ycc = mixers(ufc.reshape(b, n_ctx, fw), ucc.reshape(b, n_ctx, 2 * cw), ctx_tables, l)
            ctx_flat = _out_ffn(ctx_flat, oc.reshape(1, b * n_ctx, attn_w), yfc.reshape(1, b * n_ctx, fw),
                                ycc.reshape(1, b * n_ctx, cw), mod, norm2, w_out_b, w1_b, w3_b, w2_b, fg, l,
                                ctx_row, final=False)
        x = _out_ffn(x, o, yf, yc, mod, norm2, w_out_b, w1_b, w3_b, w2_b, fg, l, None, final=last)
    return x
```

```python
import functools
import math

import numpy as np
import jax
import jax.numpy as jnp
from jax import lax
from jax.experimental import pallas as pl
from jax.experimental.pallas import tpu as pltpu

F32 = jnp.float32
BF16 = jnp.bfloat16

GRID_W = 64
HEADS = 4
QK_DIM = 64
V_DIM = 2 * QK_DIM
FOURIER_GROUPS = 4
CONV_GROUPS = 4
CONV_K = 31
ROPE_BASE = 10000.0
EPS = 1e-6
LOG2_E = math.log2(math.e)
BOUND_SLACK = 1.0 + 2.0 ** -6
MIN_SOFTMAX_DENOMINATOR = 2.0 ** -90

LANES = 128
SUBLANES = 8
TM = 256
ROW_SUBTILES = 2
ATT_KC = 256
CONV_HALO = 16
CONV_CHUNK = 128
ADA_TN = 1024
DFT_LO = 64
MIB = 1024 * 1024


def _params(sem, vmem_mib):
    return pltpu.CompilerParams(dimension_semantics=sem, vmem_limit_bytes=vmem_mib * MIB)


def _resident(arr, layer=None):
    if layer is None:
        idx = (0,) * arr.ndim
        return pl.BlockSpec(arr.shape, lambda *_: idx, pipeline_mode=pl.Buffered(1))
    idx = (layer,) + (0,) * (arr.ndim - 1)
    return pl.BlockSpec((None,) + arr.shape[1:], lambda *_: idx, pipeline_mode=pl.Buffered(1))


def _mod_spec(mod, layer, fixed_row):
    if fixed_row is None:
        return pl.BlockSpec((None, None) + mod.shape[2:], lambda i, j: (layer, i, 0, 0))
    return pl.BlockSpec((None, None) + mod.shape[2:], lambda i, j: (layer, fixed_row, 0, 0))


def _sigmoid(x):
    return 1.0 / (1.0 + jnp.exp(-x))


def _rms(x):
    return x * lax.rsqrt(jnp.mean(x * x, axis=-1, keepdims=True) + EPS)


def _dot(a, b):
    return jnp.dot(a, b, preferred_element_type=F32)


def _ada_kernel(c_ref, w_ref, b_ref, o_ref):
    c = c_ref[...]
    s = (c * _sigmoid(c)).astype(BF16)
    o_ref[...] = _dot(s, w_ref[...].astype(BF16)) + b_ref[...]


def _ada(c_rows, w_ada, b_ada):
    depth, d, n = w_ada.shape
    rows = c_rows.shape[0]
    return pl.pallas_call(
        _ada_kernel,
        grid=(depth, n // ADA_TN),
        in_specs=[
            pl.BlockSpec((rows, d), lambda l, j: (0, 0)),
            pl.BlockSpec((None, d, ADA_TN), lambda l, j: (l, 0, j)),
            pl.BlockSpec((None, 1, ADA_TN), lambda l, j: (l, 0, j)),
        ],
        out_specs=pl.BlockSpec((None, rows, ADA_TN), lambda l, j: (l, 0, j)),
        out_shape=jax.ShapeDtypeStruct((depth, rows, n), F32),
        compiler_params=_params(("arbitrary", "arbitrary"), 24),
        name="ada",
    )(c_rows, w_ada, b_ada.reshape(depth, 1, n))


def _inproj_kernel(*refs, use_rope, kv_only, n_sub):
    x_ref, mod_ref, g_ref, w_ref = refs[:4]
    refs = refs[4:]
    if use_rope:
        cos_ref, sin_ref = refs[:2]
        refs = refs[2:]
    hw = HEADS * LANES
    lane = lax.broadcasted_iota(jnp.int32, (TM, LANES), 1)
    first_half = (lane & (QK_DIM // 4)) == 0
    row = lax.broadcasted_iota(jnp.int32, (LANES, TM), 0)
    zero = jnp.zeros((LANES, TM), F32)
    dim = lax.broadcasted_iota(jnp.int32, (LANES, LANES), 0)
    col = lax.broadcasted_iota(jnp.int32, (LANES, LANES), 1)
    map_sum = jnp.where(col == jnp.where(dim < QK_DIM, 0, 1), 1.0, 0.0).astype(BF16)

    for u in range(n_sub):
        rows = slice(u * TM, (u + 1) * TM)
        h = (_rms(x_ref[rows, :]) * g_ref[...] * (1.0 + mod_ref[1:2, :]) + mod_ref[0:1, :]).astype(BF16)

        def rope(t):
            if not use_rope:
                return t
            partner = jnp.where(first_half, pltpu.roll(t, LANES - QK_DIM // 4, 1), pltpu.roll(t, QK_DIM // 4, 1))
            return t * cos_ref[rows, :] + partner * sin_ref[rows, :]

        def put_keys(j, kf):
            k_ref[j, rows, :] = kf.astype(BF16)
            norms = _dot((kf * kf).astype(BF16), map_sum)
            kn_ref[j, u] = jnp.max(norms.reshape(TM // SUBLANES, SUBLANES, LANES), axis=0)

        if kv_only:
            k_ref, vt_ref, kn_ref = refs
            r = _dot(h, w_ref[:, hw:3 * hw])
            for j in range(HEADS):
                put_keys(j, rope(r[:, j * LANES:(j + 1) * LANES]))
                vt_ref[j, :, rows] = r[:, hw + j * LANES:hw + (j + 1) * LANES].T.astype(BF16)
            continue

        q_ref, k_ref, vt_ref, kn_ref, uf_ref, uc_ref = refs
        r = _dot(h, w_ref[...])
        for j in range(HEADS):
            qt = (rope(r[:, j * LANES:(j + 1) * LANES]) * (QK_DIM ** -0.5 * LOG2_E)).T
            for mp in range(2):
                sel = (row < QK_DIM) == (mp == 0)
                qm = qt[mp * QK_DIM:(mp + 1) * QK_DIM, :]
                qn = jnp.sqrt(jnp.sum(qm * qm, axis=0, keepdims=True)) * BOUND_SLACK
                q_ref[j, u, mp, 0:LANES, :] = jnp.where(sel, qt, zero).astype(BF16)
                q_ref[j, u, mp, LANES:2 * LANES, :] = jnp.where(row == 0, qn, zero).astype(BF16)
            put_keys(j, rope(r[:, hw + j * LANES:hw + (j + 1) * LANES]))
            vt_ref[j, :, rows] = r[:, 2 * hw + j * LANES:2 * hw + (j + 1) * LANES].T.astype(BF16)
        fw = uf_ref.shape[-1]
        uf_ref[rows, :] = r[:, 3 * hw:3 * hw + fw].astype(BF16)
        uc_ref[rows, :] = r[:, 3 * hw + fw:]


def _inproj(x, mod, g, w, rope_tables, layer, fixed_row, fw, cw2, kv_only=False):
    b, n, d = x.shape
    n_sub = ROW_SUBTILES
    bm = n_sub * TM
    assert n % bm == 0
    use_rope = rope_tables is not None
    in_specs = [
        pl.BlockSpec((None, bm, d), lambda i, j: (i, j, 0)),
        _mod_spec(mod, layer, fixed_row),
        _resident(g, layer),
        _resident(w, layer),
    ]
    operands = [x, mod, g, w]
    if use_rope:
        in_specs += [pl.BlockSpec((bm, LANES), lambda i, j: (j, 0))] * 2
        operands += list(rope_tables)
    out_specs = [
        pl.BlockSpec((None, HEADS, n_sub, 2, 2 * LANES, TM), lambda i, j: (i, 0, j, 0, 0, 0)),
        pl.BlockSpec((None, HEADS, bm, LANES), lambda i, j: (i, 0, j, 0)),
        pl.BlockSpec((None, HEADS, V_DIM, bm), lambda i, j: (i, 0, 0, j)),
        pl.BlockSpec((None, HEADS, n_sub, SUBLANES, LANES), lambda i, j: (i, 0, j, 0, 0)),
        pl.BlockSpec((None, bm, fw), lambda i, j: (i, j, 0)),
        pl.BlockSpec((None, bm, cw2), lambda i, j: (i, j, 0)),
    ]
    out_shape = [
        jax.ShapeDtypeStruct((b, HEADS, n // TM, 2, 2 * LANES, TM), BF16),
        jax.ShapeDtypeStruct((b, HEADS, n, LANES), BF16),
        jax.ShapeDtypeStruct((b, HEADS, V_DIM, n), BF16),
        jax.ShapeDtypeStruct((b, HEADS, n // TM, SUBLANES, LANES), F32),
        jax.ShapeDtypeStruct((b, n, fw), BF16),
        jax.ShapeDtypeStruct((b, n, cw2), F32),
    ]
    if kv_only:
        out_specs, out_shape = out_specs[1:4], out_shape[1:4]
    return pl.pallas_call(
        functools.partial(_inproj_kernel, use_rope=use_rope, kv_only=kv_only, n_sub=n_sub),
        grid=(b, n // bm),
        in_specs=in_specs,
        out_specs=out_specs,
        out_shape=out_shape,
        compiler_params=_params(("arbitrary", "arbitrary"), 48),
        name="inproj",
    )(*operands)


def _attn_kernel(*refs, lam_init, with_ctx, bounded):
    lq1_ref, lk1_ref, lq2_ref, lk2_ref, g_ref, kc_ref, vtc_ref, knc_ref, ql_ref, kl_ref, vtl_ref, knl_ref = refs[:12]
    refs = list(refs[12:])
    s_ref = refs.pop()
    qc_ref = refs.pop(0) if with_ctx else None
    oc_ref = refs.pop(0) if with_ctx else None
    ol_ref = refs.pop(0)
    lmin_ref = refs.pop(0) if bounded else None
    lam = (jnp.exp(jnp.sum(lq1_ref[...] * lk1_ref[...], axis=-1, keepdims=True))
           - jnp.exp(jnp.sum(lq2_ref[...] * lk2_ref[...], axis=-1, keepdims=True)) + lam_init)

    def fold(x):
        return x.reshape(x.shape[0] // SUBLANES, SUBLANES, x.shape[1])

    def finish(o_ref, t, ot, l1):
        ot = ot * (1.0 / l1)
        ms = jnp.mean(ot * ot, axis=0, keepdims=True)
        on = (ot * lax.rsqrt(ms + EPS)).T * g_ref[...] * (1.0 - lam_init)
        o_ref[t * TM:(t + 1) * TM, :] = on.astype(BF16)

    def key_chunks(key_refs):
        return [(k_ref, vt_ref, c) for k_ref, vt_ref, _ in key_refs for c in range(0, k_ref.shape[0], ATT_KC)]

    def pipeline_bounded(q_ref, o_ref, key_refs):
        chunks = key_chunks(key_refs)
        n_tiles = q_ref.shape[0]
        lane = lax.broadcasted_iota(jnp.int32, (ATT_KC, LANES), 1)
        key_cols = []
        ksq = functools.reduce(jnp.maximum, [jnp.max(kn_ref[...], axis=0) for _, _, kn_ref in key_refs])
        kb = jnp.sqrt(jnp.max(ksq, axis=0, keepdims=True)) * BOUND_SLACK
        for mp in range(2):
            key_cols.append(jnp.where(lane == 0, -kb[:, mp:mp + 1], 0.0).astype(BF16))
        sums = {}
        lmin = None
        for u in range(n_tiles + 1):
            ta, tc = u, u - 1
            do_a, do_c = ta < n_tiles, 0 <= tc < n_tiles
            if do_a:
                qts = [q_ref[ta, mp] for mp in range(2)]
                l8 = [jnp.zeros((SUBLANES, TM), F32) for _ in range(2)]
            if do_c:
                lc = sums.pop(tc)
                rho = lam * lc[0] / lc[1]
                ot = None
            for ci, (k_ref, vt_ref, c) in enumerate(chunks):
                rows = slice(ci * ATT_KC, (ci + 1) * ATT_KC)
                if do_a:
                    for mp in range(2):
                        keys = jnp.concatenate([k_ref[c:c + ATT_KC, :], key_cols[mp]], axis=1)
                        e = jnp.exp2(_dot(keys, qts[mp]))
                        s_ref[ta % 2, mp, rows, :] = e
                        l8[mp] = l8[mp] + jnp.sum(fold(e), axis=0)
                if do_c:
                    p = (s_ref[tc % 2, 0, rows, :] - rho * s_ref[tc % 2, 1, rows, :]).astype(BF16)
                    part = _dot(vt_ref[:, c:c + ATT_KC], p)
                    ot = part if ot is None else ot + part
            if do_a:
                sums[ta] = [jnp.sum(l, axis=0, keepdims=True) for l in l8]
                low = jnp.minimum(sums[ta][0], sums[ta][1])
                lmin = low if lmin is None else jnp.minimum(lmin, low)
            if do_c:
                finish(o_ref, tc, ot, lc[0])
        return jnp.min(lmin, axis=1, keepdims=True)

    def pipeline(q_ref, o_ref, key_refs):
        chunks = key_chunks(key_refs)
        n_tiles = q_ref.shape[0]
        maxima, sums = {}, {}
        for u in range(n_tiles + 2):
            ta, tb, tc = u, u - 1, u - 2
            do_a, do_b, do_c = ta < n_tiles, 0 <= tb < n_tiles, 0 <= tc < n_tiles
            if do_a:
                qts = [q_ref[ta, mp, 0:LANES, :] for mp in range(2)]
                m8 = [None, None]
            if do_b:
                mb = maxima.pop(tb)
                l8 = [jnp.zeros((SUBLANES, TM), F32) for _ in range(2)]
            if do_c:
                lc = sums.pop(tc)
                rho = lam * lc[0] / lc[1]
                ot = None
            for ci, (k_ref, vt_ref, c) in enumerate(chunks):
                rows = slice(ci * ATT_KC, (ci + 1) * ATT_KC)
                if do_a:
                    for mp in range(2):
                        s = _dot(k_ref[c:c + ATT_KC, :], qts[mp])
                        s_ref[ta % 3, mp, rows, :] = s
                        cm = jnp.max(fold(s), axis=0)
                        m8[mp] = cm if m8[mp] is None else jnp.maximum(m8[mp], cm)
                if do_b:
                    for mp in range(2):
                        e = jnp.exp2(s_ref[tb % 3, mp, rows, :] - mb[mp])
                        s_ref[tb % 3, mp, rows, :] = e
                        l8[mp] = l8[mp] + jnp.sum(fold(e), axis=0)
                if do_c:
                    p = (s_ref[tc % 3, 0, rows, :] - rho * s_ref[tc % 3, 1, rows, :]).astype(BF16)
                    part = _dot(vt_ref[:, c:c + ATT_KC], p)
                    ot = part if ot is None else ot + part
            if do_a:
                maxima[ta] = [jnp.max(m, axis=0, keepdims=True) for m in m8]
            if do_b:
                sums[tb] = [jnp.sum(l, axis=0, keepdims=True) for l in l8]
            if do_c:
                finish(o_ref, tc, ot, lc[0])

    run = pipeline_bounded if bounded else pipeline
    ctx_keys, lat_keys = (kc_ref, vtc_ref, knc_ref), (kl_ref, vtl_ref, knl_ref)
    lmin = run(ql_ref, ol_ref, [ctx_keys, lat_keys])
    if with_ctx:
        lmin_ctx = run(qc_ref, oc_ref, [ctx_keys])
    if bounded:
        if with_ctx:
            lmin = jnp.minimum(lmin, lmin_ctx)
        lmin_ref[...] = jnp.broadcast_to(lmin, lmin_ref.shape)


def _attention_call(qkv_ctx, qkv_lat, lam_vecs, g, layer, lam_init, with_ctx, bounded):
    q_lat, k_lat, vt_lat, kn_lat = qkv_lat
    k_ctx, vt_ctx, kn_ctx = qkv_ctx[-3:]
    b, h, n_lat, _ = k_lat.shape
    n_ctx = k_ctx.shape[2] // b
    lat_tiles, ctx_tiles = n_lat // TM, n_ctx // TM

    def tiles_spec(tiles, tail, ctx):
        zeros = (0,) * len(tail)
        if ctx:
            return pl.BlockSpec((None, None, tiles) + tail, lambda i, j: (0, j, i) + zeros)
        return pl.BlockSpec((None, None, tiles) + tail, lambda i, j: (i, j, 0) + zeros)

    q_tile, kn_tile = (2, 2 * LANES, TM), (SUBLANES, LANES)
    in_specs = [_resident(v, layer) for v in lam_vecs] + [
        _resident(g, layer),
        pl.BlockSpec((None, None, n_ctx, LANES), lambda i, j: (0, j, i, 0)),
        pl.BlockSpec((None, None, V_DIM, n_ctx), lambda i, j: (0, j, 0, i)),
        tiles_spec(ctx_tiles, kn_tile, True),
        tiles_spec(lat_tiles, q_tile, False),
        pl.BlockSpec((None, None, n_lat, LANES), lambda i, j: (i, j, 0, 0)),
        pl.BlockSpec((None, None, V_DIM, n_lat), lambda i, j: (i, j, 0, 0)),
        tiles_spec(lat_tiles, kn_tile, False),
    ]
    operands = list(lam_vecs) + [g, k_ctx, vt_ctx, kn_ctx, q_lat, k_lat, vt_lat, kn_lat]
    out_specs = [pl.BlockSpec((None, n_lat, V_DIM), lambda i, j: (i, 0, j))]
    out_shape = [jax.ShapeDtypeStruct((b, n_lat, h * V_DIM), BF16)]
    if with_ctx:
        in_specs.append(tiles_spec(ctx_tiles, q_tile, True))
        operands.append(qkv_ctx[0])
        out_specs.insert(0, pl.BlockSpec((None, n_ctx, V_DIM), lambda i, j: (i, 0, j)))
        out_shape.insert(0, jax.ShapeDtypeStruct((b, n_ctx, h * V_DIM), BF16))
    if bounded:
        out_specs.append(pl.BlockSpec((None, None, SUBLANES, LANES), lambda i, j: (i, j, 0, 0)))
        out_shape.append(jax.ShapeDtypeStruct((b, h, SUBLANES, LANES), F32))
    return pl.pallas_call(
        functools.partial(_attn_kernel, lam_init=lam_init, with_ctx=with_ctx, bounded=bounded),
        grid=(b, h),
        in_specs=in_specs,
        out_specs=out_specs,
        out_shape=out_shape,
        scratch_shapes=[pltpu.VMEM((2 if bounded else 3, 2, n_ctx + n_lat, TM), F32)],
        compiler_params=_params(("arbitrary", "arbitrary"), 48),
        name="attention_bounded" if bounded else "attention_exact",
    )(*operands)


def _attention(qkv_ctx, qkv_lat, lam_vecs, g, layer, lam_init, with_ctx):
    args = (qkv_ctx, qkv_lat, lam_vecs, g, layer, lam_init, with_ctx)
    *outs, lmin = _attention_call(*args, bounded=True)
    safe = jnp.min(lmin) >= MIN_SOFTMAX_DENOMINATOR
    return lax.cond(safe, lambda: tuple(outs), lambda: tuple(_attention_call(*args, bounded=False)))


def _fourier_kernel(u_ref, chan_c_ref, chan_s_ref, wf_ref, pos_c_ref, pos_s_ref, mid_ref, flip_ref, y_ref):
    n, width = u_ref.shape
    half = n // 2
    scale = (n * width // FOURIER_GROUPS) ** -0.5
    u = u_ref[...]
    a = _dot(pos_c_ref[...], u).astype(BF16)
    b = _dot(pos_s_ref[...], u).astype(BF16)
    p = _dot(a, chan_c_ref[...])
    q = _dot(b, chan_s_ref[...])
    y_ref[0:half, :] = _dot(((p - q) * scale).astype(BF16), wf_ref[...]).astype(BF16)
    mirrored = _dot(flip_ref[...], ((p + q) * scale).astype(BF16))
    mid = _dot(_dot(mid_ref[...], u).astype(BF16), chan_c_ref[...]) * scale
    rows = mid.shape[0]
    upper = jnp.concatenate([mirrored[0:rows] + mid, mirrored[rows:]], axis=0)
    y_ref[half:, :] = _dot(upper.astype(BF16), wf_ref[...]).astype(BF16)


def _fourier(uf, chan_c, chan_s, wf_bd, layer, pos_tables):
    b, n, fw = uf.shape
    half = n // 2
    idx = jnp.arange(half)
    flip = (idx[:, None] + idx[None, :] == half).astype(BF16)
    mid = np.zeros((SUBLANES, n), np.float32)
    mid[0] = 1.0 - 2.0 * (np.arange(n) % 2)
    mid = jnp.asarray(mid).astype(BF16)
    row_spec = pl.BlockSpec((None, n, fw), lambda i: (i, 0, 0))
    return pl.pallas_call(
        _fourier_kernel,
        grid=(b,),
        in_specs=[row_spec, _resident(chan_c), _resident(chan_s), _resident(wf_bd, layer)]
                 + [_resident(t) for t in pos_tables] + [_resident(mid), _resident(flip)],
        out_specs=row_spec,
        out_shape=jax.ShapeDtypeStruct((b, n, fw), BF16),
        compiler_params=_params(("arbitrary",), 48),
        name="fourier",
    )(uf, chan_c, chan_s, wf_bd, *pos_tables, mid, flip)


def _split_dot(x, m):
    hi = x.astype(BF16)
    lo = (x - hi.astype(F32)).astype(BF16)
    return _dot(hi, m) + _dot(lo, m)


def _conv_kernel(u_ref, w_ref, b_ref, lg_ref, lb_ref, avg_ref, pw_ref, y_ref, zpad_ref, acc_a_ref, acc_b_ref):
    n, cw = y_ref.shape
    halo = jnp.zeros((CONV_HALO, cw), F32)

    def taps(i, acc_ref, slot):
        r0 = pl.multiple_of(i * CONV_CHUNK, CONV_CHUNK)
        for lo in range(0, cw, LANES):
            win = zpad_ref[pl.ds(r0, CONV_CHUNK + 2 * CONV_HALO), lo:lo + LANES]
            part = jnp.zeros((CONV_CHUNK, LANES), F32) + b_ref[:, lo:lo + LANES]
            for shift in range(SUBLANES):
                rolled = win if shift == 0 else pltpu.roll(win, win.shape[0] - shift, 0)
                for aligned in range(0, 2 * CONV_HALO, SUBLANES):
                    tap = aligned + shift - (CONV_HALO - CONV_K // 2)
                    if 0 <= tap < CONV_K:
                        part = part + rolled[aligned:aligned + CONV_CHUNK] * w_ref[tap:tap + 1, lo:lo + LANES]
            acc_ref[slot, :, lo:lo + LANES] = part

    def project(acc_ref, slot):
        acc = acc_ref[slot]
        mu = _split_dot(acc, avg_ref[...])
        dev = acc - mu
        var = _split_dot(dev * dev, avg_ref[...])
        zn = dev * lax.rsqrt(var + EPS) * lg_ref[...] + lb_ref[...]
        act = (zn * _sigmoid(zn)).astype(BF16)
        return _dot(act, pw_ref[...]).astype(BF16)

    def round_(first_tap, tap_ref, ready_ref):
        if first_tap is not None:
            taps(first_tap, tap_ref, 0)
            taps(first_tap + 1, tap_ref, 1)
        return [project(ready_ref, 0), project(ready_ref, 1)]

    def store(first_chunk, ys):
        r0 = pl.multiple_of(first_chunk * CONV_CHUNK, 2 * CONV_CHUNK)
        y_ref[pl.ds(r0, len(ys) * CONV_CHUNK), :] = jnp.concatenate(ys, axis=0)

    zpad_ref[0:CONV_HALO, :] = halo
    zpad_ref[CONV_HALO:CONV_HALO + n, :] = u_ref[:, 0:cw] * _sigmoid(u_ref[:, cw:2 * cw])
    zpad_ref[CONV_HALO + n:2 * CONV_HALO + n, :] = halo

    chunks = n // CONV_CHUNK
    assert chunks == 2 or chunks % 4 == 0

    def step(j, carry):
        c0 = 4 * j
        ys = round_(c0 + 2, acc_b_ref, acc_a_ref) + round_(c0 + 4, acc_a_ref, acc_b_ref)
        store(c0, ys)
        return carry

    taps(0, acc_a_ref, 0)
    taps(1, acc_a_ref, 1)
    if chunks == 2:
        store(0, round_(None, None, acc_a_ref))
    else:
        lax.fori_loop(0, chunks // 4 - 1, step, 0)
        store(chunks - 4, round_(chunks - 2, acc_b_ref, acc_a_ref) + round_(None, None, acc_b_ref))


def _conv(uc, conv_w, conv_b, ln_g, ln_b, avg, w_pw, layer):
    b, n, cw2 = uc.shape
    cw = cw2 // 2
    return pl.pallas_call(
        _conv_kernel,
        grid=(b,),
        in_specs=[
            pl.BlockSpec((None, n, cw2), lambda i: (i, 0, 0)),
            _resident(conv_w, layer), _resident(conv_b, layer), _resident(ln_g, layer), _resident(ln_b, layer),
            _resident(avg), _resident(w_pw, layer),
        ],
        out_specs=pl.BlockSpec((None, n, cw), lambda i: (i, 0, 0)),
        out_shape=jax.ShapeDtypeStruct((b, n, cw), BF16),
        scratch_shapes=[pltpu.VMEM((n + 2 * CONV_HALO, cw), F32), pltpu.VMEM((2, CONV_CHUNK, cw), F32),
                        pltpu.VMEM((2, CONV_CHUNK, cw), F32)],
        compiler_params=_params(("arbitrary",), 40),
        name="conv",
    )(uc, conv_w, conv_b, ln_g, ln_b, avg, w_pw)


def _ffn_kernel(x_ref, o_ref, yf_ref, yc_ref, mod_ref, g_ref, wout_ref, w1_ref, w3_ref, w2_ref, fg_ref, out_ref,
                *, final, n_sub):
    aw = o_ref.shape[-1]
    fw = yf_ref.shape[-1]
    for u in range(n_sub):
        rows = slice(u * TM, (u + 1) * TM)
        y = (_dot(o_ref[rows, :], wout_ref[0:aw, :]) + _dot(yf_ref[rows, :], wout_ref[aw:aw + fw, :])
             + _dot(yc_ref[rows, :], wout_ref[aw + fw:, :]))
        x1 = x_ref[rows, :] + mod_ref[2:3, :] * y
        h = (_rms(x1) * g_ref[...] * (1.0 + mod_ref[4:5, :]) + mod_ref[3:4, :]).astype(BF16)
        a = _dot(h, w1_ref[...])
        gated = (a * _sigmoid(a) * _dot(h, w3_ref[...])).astype(BF16)
        x2 = x1 + mod_ref[5:6, :] * _dot(gated, w2_ref[...])
        if final:
            x2 = _rms(x2) * fg_ref[...]
        out_ref[rows, :] = x2


def _out_ffn(x, o, yf, yc, mod, g2, w_out, w1, w3, w2, final_g, layer, fixed_row, final):
    b, n, d = x.shape
    n_sub = ROW_SUBTILES
    bm = n_sub * TM
    assert n % bm == 0

    def rows(width):
        return pl.BlockSpec((None, bm, width), lambda i, j: (i, j, 0))

    return pl.pallas_call(
        functools.partial(_ffn_kernel, final=final, n_sub=n_sub),
        grid=(b, n // bm),
        in_specs=[
            rows(d), rows(o.shape[-1]), rows(yf.shape[-1]), rows(yc.shape[-1]),
            _mod_spec(mod, layer, fixed_row),
            _resident(g2, layer), _resident(w_out, layer), _resident(w1, layer), _resident(w3, layer),
            _resident(w2, layer), _resident(final_g),
        ],
        out_specs=rows(d),
        out_shape=jax.ShapeDtypeStruct((b, n, d), F32),
        compiler_params=_params(("arbitrary", "arbitrary"), 58),
        name="out_ffn",
    )(x, o, yf, yc, mod, g2, w_out, w1, w3, w2, final_g)


def _rope_tables(n_lat):
    n_freq = QK_DIM // 4
    tok = np.arange(n_lat)
    inv_freq = np.float32(ROPE_BASE) ** (-np.arange(n_freq, dtype=np.float32) / np.float32(n_freq))
    ang_r = ((tok // GRID_W).astype(np.float32)[:, None] * inv_freq).astype(np.float64)
    ang_c = ((tok % GRID_W).astype(np.float32)[:, None] * inv_freq).astype(np.float64)
    cos = np.concatenate([np.cos(ang_r)] * 2 + [np.cos(ang_c)] * 2, axis=-1)
    sin = np.concatenate([-np.sin(ang_r), np.sin(ang_r), -np.sin(ang_c), np.sin(ang_c)], axis=-1)
    reps = LANES // QK_DIM
    return (jnp.asarray(np.tile(cos, (1, reps)), dtype=F32), jnp.asarray(np.tile(sin, (1, reps)), dtype=F32))


def _dft_angles(rows, n):
    return 2.0 * np.pi * ((np.asarray(rows, np.int64)[:, None] * np.arange(n, dtype=np.int64)[None, :]) % n) / n


def _dft_tables_small(n):
    ang = _dft_angles(np.arange(n), n)
    return np.cos(ang).astype(np.float32), np.sin(ang).astype(np.float32)


def _dft_tables(n):
    rows = n // 2
    if n <= 4 * DFT_LO:
        return [jnp.asarray(t[:rows]).astype(BF16) for t in _dft_tables_small(n)]
    hi = rows // DFT_LO
    ang_hi = _dft_angles(np.arange(hi) * DFT_LO, n)
    ang_lo = _dft_angles(np.arange(DFT_LO), n)
    ch, sh = (jnp.asarray(f(ang_hi), dtype=F32)[:, None, :] for f in (np.cos, np.sin))
    cl, sl = (jnp.asarray(f(ang_lo), dtype=F32)[None, :, :] for f in (np.cos, np.sin))
    return [(ch * cl - sh * sl).reshape(rows, n).astype(BF16), (sh * cl + ch * sl).reshape(rows, n).astype(BF16)]


def _block_diag(blocks):
    g, r, c = blocks.shape[-3:]
    eye = jnp.eye(g, dtype=blocks.dtype)
    out = eye[:, None, :, None] * blocks[..., :, :, None, :]
    return out.reshape(blocks.shape[:-3] + (g * r, g * c))


def _permute_inproj(w_in, qk_w):
    lead = w_in.shape[:-1]
    qk = w_in[..., :4 * qk_w].reshape(lead + (2, 2, HEADS, QK_DIM))
    qk = jnp.swapaxes(qk, -3, -2).reshape(lead + (4 * qk_w,))
    return jnp.concatenate([qk, w_in[..., 4 * qk_w:]], axis=-1)


def kernel(x, c, ctx, c_ctx, w_ada, b_ada, norm1_g, norm2_g, w_in, lam_q1, lam_k1, lam_q2, lam_k2, subln_g,
           w_fourier, conv_w, conv_b, conv_ln_g, conv_ln_b, w_conv_out, w_out, w_ffn1, w_ffn3, w_ffn2, final_g):
    b, n_lat, d = x.shape
    n_ctx = ctx.shape[1]
    depth = w_ada.shape[0]
    fw = w_fourier.shape[1] * w_fourier.shape[2]
    cw = conv_w.shape[-1]
    qk_w = HEADS * QK_DIM
    attn_w = HEADS * V_DIM
    assert n_lat % GRID_W == 0 and n_ctx % TM == 0
    assert w_in.shape[-1] == 4 * qk_w + attn_w + fw + 2 * cw

    pad = (-(b + 1)) % SUBLANES
    c_rows = jnp.concatenate([c, c_ctx[None, :], jnp.zeros((pad, d), c.dtype)], axis=0)
    mod = _ada(c_rows, w_ada, b_ada)
    mod = mod.reshape(depth, mod.shape[1], 6, d)
    ctx_row = b

    rope = _rope_tables(n_lat)
    cc, cs = _dft_tables_small(fw // FOURIER_GROUPS)
    eye = np.eye(FOURIER_GROUPS, dtype=np.float32)
    chan_c = jnp.asarray(np.kron(eye, cc)).astype(BF16)
    chan_s = jnp.asarray(np.kron(eye, cs)).astype(BF16)
    lat_tables = _dft_tables(n_lat)
    ctx_tables = _dft_tables(n_ctx)
    group = cw // CONV_GROUPS
    avg = jnp.asarray(np.kron(np.eye(CONV_GROUPS), np.full((group, group), 1.0 / group)), dtype=F32).astype(BF16)

    w_in_b = _permute_inproj(w_in, qk_w).astype(BF16)
    wf_bd = _block_diag(w_fourier).astype(BF16)
    w_pw_b, w_out_b = w_conv_out.astype(BF16), w_out.astype(BF16)
    w1_b, w3_b, w2_b = w_ffn1.astype(BF16), w_ffn3.astype(BF16), w_ffn2.astype(BF16)

    def per_layer_rows(a):
        return a.reshape(depth, 1, a.shape[-1])

    norm1, norm2, subln = per_layer_rows(norm1_g), per_layer_rows(norm2_g), per_layer_rows(subln_g)
    lam_vecs = [per_layer_rows(a) for a in (lam_q1, lam_k1, lam_q2, lam_k2)]
    cb, lg, lb = per_layer_rows(conv_b), per_layer_rows(conv_ln_g), per_layer_rows(conv_ln_b)
    fg = final_g[None, :]

    def mixers(u_f, u_c, tables, l):
        yf = _fourier(u_f, chan_c, chan_s, wf_bd, l, tables)
        yc = _conv(u_c, conv_w, cb, lg, lb, avg, w_pw_b, l)
        return yf, yc

    ctx_flat = ctx.reshape(1, b * n_ctx, d)
    for l in range(depth):
        last = l == depth - 1
        lam_init = 0.8 - 0.6 * math.exp(-0.3 * l)
        q, k, vt, kn, uf, uc = _inproj(x, mod, norm1, w_in_b, rope, l, None, fw, 2 * cw)
        yf, yc = mixers(uf, uc, lat_tables, l)
        if last:
            kv_ctx = _inproj(ctx_flat, mod, norm1, w_in_b, None, l, ctx_row, fw, 2 * cw, kv_only=True)
            (o,) = _attention(kv_ctx, (q, k, vt, kn), lam_vecs, subln, l, lam_init, with_ctx=False)
        else:
            qc, kc, vtc, knc, ufc, ucc = _inproj(ctx_flat, mod, norm1, w_in_b, None, l, ctx_row, fw, 2 * cw)
            oc, o = _attention((qc, kc, vtc, knc), (q, k, vt, kn), lam_vecs, subln, l, lam_init, with_ctx=True)
            yfc, ycc = mixers(ufc.reshape(b, n_ctx, fw), ucc.reshape(b, n_ctx, 2 * cw), ctx_tables, l)
            ctx_flat = _out_ffn(ctx_flat, oc.reshape(1, b * n_ctx, attn_w), yfc.reshape(1, b * n_ctx, fw),
                                ycc.reshape(1, b * n_ctx, cw), mod, norm2, w_out_b, w1_b, w3_b, w2_b, fg, l,
                                ctx_row, final=False)
        x = _out_ffn(x, o, yf, yc, mod, norm2, w_out_b, w1_b, w3_b, w2_b, fg, l, None, final=last)
    return x
```

```python
import functools
import math

import numpy as np
import jax
import jax.numpy as jnp
from jax import lax
from jax.experimental import pallas as pl
from jax.experimental.pallas import tpu as pltpu

F32 = jnp.float32
BF16 = jnp.bfloat16

GRID_W = 64
HEADS = 4
QK_DIM = 64
V_DIM = 2 * QK_DIM
FOURIER_GROUPS = 4
CONV_GROUPS = 4
CONV_K = 31
ROPE_BASE = 10000.0
EPS = 1e-6
LOG2_E = math.log2(math.e)
BOUND_SLACK = 1.0 + 2.0 ** -6
MIN_SOFTMAX_DENOMINATOR = 2.0 ** -90

LANES = 128
SUBLANES = 8
TM = 256
ROW_SUBTILES = 2
ATT_KC = 256
CONV_HALO = 16
CONV_CHUNK = 128
ADA_TN = 1024
DFT_LO = 64
MIB = 1024 * 1024


def _params(sem, vmem_mib):
    return pltpu.CompilerParams(dimension_semantics=sem, vmem_limit_bytes=vmem_mib * MIB)


def _resident(arr, layer=None):
    if layer is None:
        idx = (0,) * arr.ndim
        return pl.BlockSpec(arr.shape, lambda *_: idx, pipeline_mode=pl.Buffered(1))
    idx = (layer,) + (0,) * (arr.ndim - 1)
    return pl.BlockSpec((None,) + arr.shape[1:], lambda *_: idx, pipeline_mode=pl.Buffered(1))


def _mod_spec(mod, layer, fixed_row):
    if fixed_row is None:
        return pl.BlockSpec((None, None) + mod.shape[2:], lambda i, j: (layer, i, 0, 0))
    return pl.BlockSpec((None, None) + mod.shape[2:], lambda i, j: (layer, fixed_row, 0, 0))


def _sigmoid(x):
    return 1.0 / (1.0 + jnp.exp(-x))


def _rms(x):
    return x * lax.rsqrt(jnp.mean(x * x, axis=-1, keepdims=True) + EPS)


def _dot(a, b):
    return jnp.dot(a, b, preferred_element_type=F32)


def _ada_kernel(c_ref, w_ref, b_ref, o_ref):
    c = c_ref[...]
    s = (c * _sigmoid(c)).astype(BF16)
    o_ref[...] = _dot(s, w_ref[...].astype(BF16)) + b_ref[...]


def _ada(c_rows, w_ada, b_ada):
    depth, d, n = w_ada.shape
    rows = c_rows.shape[0]
    return pl.pallas_call(
        _ada_kernel,
        grid=(depth, n // ADA_TN),
        in_specs=[
            pl.BlockSpec((rows, d), lambda l, j: (0, 0)),
            pl.BlockSpec((None, d, ADA_TN), lambda l, j: (l, 0, j)),
            pl.BlockSpec((None, 1, ADA_TN), lambda l, j: (l, 0, j)),
        ],
        out_specs=pl.BlockSpec((None, rows, ADA_TN), lambda l, j: (l, 0, j)),
        out_shape=jax.ShapeDtypeStruct((depth, rows, n), F32),
        compiler_params=_params(("arbitrary", "arbitrary"), 24),
        name="ada",
    )(c_rows, w_ada, b_ada.reshape(depth, 1, n))


def _inproj_kernel(*refs, use_rope, kv_only, n_sub, n_cast):
    x_ref, mod_ref, g_ref, w_ref = refs[:4]
    refs = refs[4:]
    if use_rope:
        cos_ref, sin_ref = refs[:2]
        refs = refs[2:]
    if n_cast:
        for src_ref, dst_ref in zip(refs[:n_cast], refs[-n_cast:]):
            dst_ref[...] = src_ref[...].astype(BF16)
        refs = refs[n_cast:-n_cast]
    hw = HEADS * LANES
    lane = lax.broadcasted_iota(jnp.int32, (TM, LANES), 1)
    first_half = (lane & (QK_DIM // 4)) == 0
    row = lax.broadcasted_iota(jnp.int32, (LANES, TM), 0)
    zero = jnp.zeros((LANES, TM), F32)
    dim = lax.broadcasted_iota(jnp.int32, (LANES, LANES), 0)
    col = lax.broadcasted_iota(jnp.int32, (LANES, LANES), 1)
    map_sum = jnp.where(col == jnp.where(dim < QK_DIM, 0, 1), 1.0, 0.0).astype(BF16)

    for u in range(n_sub):
        rows = slice(u * TM, (u + 1) * TM)
        h = (_rms(x_ref[rows, :]) * g_ref[...] * (1.0 + mod_ref[1:2, :]) + mod_ref[0:1, :]).astype(BF16)

        def rope(t):
            if not use_rope:
                return t
            partner = jnp.where(first_half, pltpu.roll(t, LANES - QK_DIM // 4, 1), pltpu.roll(t, QK_DIM // 4, 1))
            return t * cos_ref[rows, :] + partner * sin_ref[rows, :]

        def put_keys(j, kf):
            k_ref[j, rows, :] = kf.astype(BF16)
            norms = _dot((kf * kf).astype(BF16), map_sum)
            kn_ref[j, u] = jnp.max(norms.reshape(TM // SUBLANES, SUBLANES, LANES), axis=0)

        if kv_only:
            k_ref, vt_ref, kn_ref = refs
            r = _dot(h, w_ref[:, hw:3 * hw])
            for j in range(HEADS):
                put_keys(j, rope(r[:, j * LANES:(j + 1) * LANES]))
                vt_ref[j, :, rows] = r[:, hw + j * LANES:hw + (j + 1) * LANES].T.astype(BF16)
            continue

        q_ref, k_ref, vt_ref, kn_ref, uf_ref, uc_ref = refs
        r = _dot(h, w_ref[...])
        for j in range(HEADS):
            qt = (rope(r[:, j * LANES:(j + 1) * LANES]) * (QK_DIM ** -0.5 * LOG2_E)).T
            for mp in range(2):
                sel = (row < QK_DIM) == (mp == 0)
                qm = qt[mp * QK_DIM:(mp + 1) * QK_DIM, :]
                qn = jnp.sqrt(jnp.sum(qm * qm, axis=0, keepdims=True)) * BOUND_SLACK
                q_ref[j, u, mp, 0:LANES, :] = jnp.where(sel, qt, zero).astype(BF16)
                q_ref[j, u, mp, LANES:2 * LANES, :] = jnp.where(row == 0, qn, zero).astype(BF16)
            put_keys(j, rope(r[:, hw + j * LANES:hw + (j + 1) * LANES]))
            vt_ref[j, :, rows] = r[:, 2 * hw + j * LANES:2 * hw + (j + 1) * LANES].T.astype(BF16)
        fw = uf_ref.shape[-1]
        uf_ref[rows, :] = r[:, 3 * hw:3 * hw + fw].astype(BF16)
        uc_ref[rows, :] = r[:, 3 * hw + fw:]


def _inproj(x, mod, g, w, rope_tables, layer, fixed_row, fw, cw2, kv_only=False, cast=()):
    b, n, d = x.shape
    n_sub = ROW_SUBTILES
    bm = n_sub * TM
    assert n % bm == 0
    steps = n // bm
    use_rope = rope_tables is not None
    in_specs = [
        pl.BlockSpec((None, bm, d), lambda i, j: (i, j, 0)),
        _mod_spec(mod, layer, fixed_row),
        _resident(g, layer),
        _resident(w, layer),
    ]
    operands = [x, mod, g, w]
    if use_rope:
        in_specs += [pl.BlockSpec((bm, LANES), lambda i, j: (j, 0))] * 2
        operands += list(rope_tables)
    out_specs = [
        pl.BlockSpec((None, HEADS, n_sub, 2, 2 * LANES, TM), lambda i, j: (i, 0, j, 0, 0, 0)),
        pl.BlockSpec((None, HEADS, bm, LANES), lambda i, j: (i, 0, j, 0)),
        pl.BlockSpec((None, HEADS, V_DIM, bm), lambda i, j: (i, 0, 0, j)),
        pl.BlockSpec((None, HEADS, n_sub, SUBLANES, LANES), lambda i, j: (i, 0, j, 0, 0)),
        pl.BlockSpec((None, bm, fw), lambda i, j: (i, j, 0)),
        pl.BlockSpec((None, bm, cw2), lambda i, j: (i, j, 0)),
    ]
    out_shape = [
        jax.ShapeDtypeStruct((b, HEADS, n // TM, 2, 2 * LANES, TM), BF16),
        jax.ShapeDtypeStruct((b, HEADS, n, LANES), BF16),
        jax.ShapeDtypeStruct((b, HEADS, V_DIM, n), BF16),
        jax.ShapeDtypeStruct((b, HEADS, n // TM, SUBLANES, LANES), F32),
        jax.ShapeDtypeStruct((b, n, fw), BF16),
        jax.ShapeDtypeStruct((b, n, cw2), F32),
    ]
    if kv_only:
        out_specs, out_shape = out_specs[1:4], out_shape[1:4]
    for wt in cast:
        flat = wt.reshape(-1, wt.shape[-1])
        slab = flat.shape[0] // (b * steps)
        assert slab * b * steps == flat.shape[0] and slab % (2 * SUBLANES) == 0
        spec = pl.BlockSpec((slab, flat.shape[1]), lambda i, j: (i * steps + j, 0))
        in_specs.append(spec)
        operands.append(flat)
        out_specs.append(spec)
        out_shape.append(jax.ShapeDtypeStruct(flat.shape, BF16))
    outs = pl.pallas_call(
        functools.partial(_inproj_kernel, use_rope=use_rope, kv_only=kv_only, n_sub=n_sub, n_cast=len(cast)),
        grid=(b, steps),
        in_specs=in_specs,
        out_specs=out_specs,
        out_shape=out_shape,
        compiler_params=_params(("arbitrary", "arbitrary"), 56),
        name="inproj",
    )(*operands)
    n_main = len(outs) - len(cast)
    return list(outs[:n_main]) + [o.reshape(wt.shape) for o, wt in zip(outs[n_main:], cast)]


def _attn_kernel(*refs, lam_init, with_ctx, bounded):
    lq1_ref, lk1_ref, lq2_ref, lk2_ref, g_ref, kc_ref, vtc_ref, knc_ref, ql_ref, kl_ref, vtl_ref, knl_ref = refs[:12]
    refs = list(refs[12:])
    s_ref = refs.pop()
    qc_ref = refs.pop(0) if with_ctx else None
    oc_ref = refs.pop(0) if with_ctx else None
    ol_ref = refs.pop(0)
    lmin_ref = refs.pop(0) if bounded else None
    lam = (jnp.exp(jnp.sum(lq1_ref[...] * lk1_ref[...], axis=-1, keepdims=True))
           - jnp.exp(jnp.sum(lq2_ref[...] * lk2_ref[...], axis=-1, keepdims=True)) + lam_init)

    def fold(x):
        return x.reshape(x.shape[0] // SUBLANES, SUBLANES, x.shape[1])

    def finish(o_ref, t, ot, l1):
        ot = ot * (1.0 / l1)
        ms = jnp.mean(ot * ot, axis=0, keepdims=True)
        on = (ot * lax.rsqrt(ms + EPS)).T * g_ref[...] * (1.0 - lam_init)
        o_ref[t * TM:(t + 1) * TM, :] = on.astype(BF16)

    def key_chunks(key_refs):
        return [(k_ref, vt_ref, c) for k_ref, vt_ref, _ in key_refs for c in range(0, k_ref.shape[0], ATT_KC)]

    def pipeline_bounded(q_ref, o_ref, key_refs):
        chunks = key_chunks(key_refs)
        n_tiles = q_ref.shape[0]
        lane = lax.broadcasted_iota(jnp.int32, (ATT_KC, LANES), 1)
        key_cols = []
        ksq = functools.reduce(jnp.maximum, [jnp.max(kn_ref[...], axis=0) for _, _, kn_ref in key_refs])
        kb = jnp.sqrt(jnp.max(ksq, axis=0, keepdims=True)) * BOUND_SLACK
        for mp in range(2):
            key_cols.append(jnp.where(lane == 0, -kb[:, mp:mp + 1], 0.0).astype(BF16))
        sums = {}
        lmin = None
        for u in range(n_tiles + 1):
            ta, tc = u, u - 1
            do_a, do_c = ta < n_tiles, 0 <= tc < n_tiles
            if do_a:
                qts = [q_ref[ta, mp] for mp in range(2)]
                l8 = [jnp.zeros((SUBLANES, TM), F32) for _ in range(2)]
            if do_c:
                lc = sums.pop(tc)
                rho = lam * lc[0] / lc[1]
                ot = None
            for ci, (k_ref, vt_ref, c) in enumerate(chunks):
                rows = slice(ci * ATT_KC, (ci + 1) * ATT_KC)
                if do_a:
                    for mp in range(2):
                        keys = jnp.concatenate([k_ref[c:c + ATT_KC, :], key_cols[mp]], axis=1)
                        e = jnp.exp2(_dot(keys, qts[mp]))
                        s_ref[ta % 2, mp, rows, :] = e
                        l8[mp] = l8[mp] + jnp.sum(fold(e), axis=0)
                if do_c:
                    p = (s_ref[tc % 2, 0, rows, :] - rho * s_ref[tc % 2, 1, rows, :]).astype(BF16)
                    part = _dot(vt_ref[:, c:c + ATT_KC], p)
                    ot = part if ot is None else ot + part
            if do_a:
                sums[ta] = [jnp.sum(l, axis=0, keepdims=True) for l in l8]
                low = jnp.minimum(sums[ta][0], sums[ta][1])
                lmin = low if lmin is None else jnp.minimum(lmin, low)
            if do_c:
                finish(o_ref, tc, ot, lc[0])
        return jnp.min(lmin, axis=1, keepdims=True)

    def pipeline(q_ref, o_ref, key_refs):
        chunks = key_chunks(key_refs)
        n_tiles = q_ref.shape[0]
        maxima, sums = {}, {}
        for u in range(n_tiles + 2):
            ta, tb, tc = u, u - 1, u - 2
            do_a, do_b, do_c = ta < n_tiles, 0 <= tb < n_tiles, 0 <= tc < n_tiles
            if do_a:
                qts = [q_ref[ta, mp, 0:LANES, :] for mp in range(2)]
                m8 = [None, None]
            if do_b:
                mb = maxima.pop(tb)
                l8 = [jnp.zeros((SUBLANES, TM), F32) for _ in range(2)]
            if do_c:
                lc = sums.pop(tc)
                rho = lam * lc[0] / lc[1]
                ot = None
            for ci, (k_ref, vt_ref, c) in enumerate(chunks):
                rows = slice(ci * ATT_KC, (ci + 1) * ATT_KC)
                if do_a:
                    for mp in range(2):
                        s = _dot(k_ref[c:c + ATT_KC, :], qts[mp])
                        s_ref[ta % 3, mp, rows, :] = s
                        cm = jnp.max(fold(s), axis=0)
                        m8[mp] = cm if m8[mp] is None else jnp.maximum(m8[mp], cm)
                if do_b:
                    for mp in range(2):
                        e = jnp.exp2(s_ref[tb % 3, mp, rows, :] - mb[mp])
                        s_ref[tb % 3, mp, rows, :] = e
                        l8[mp] = l8[mp] + jnp.sum(fold(e), axis=0)
                if do_c:
                    p = (s_ref[tc % 3, 0, rows, :] - rho * s_ref[tc % 3, 1, rows, :]).astype(BF16)
                    part = _dot(vt_ref[:, c:c + ATT_KC], p)
                    ot = part if ot is None else ot + part
            if do_a:
                maxima[ta] = [jnp.max(m, axis=0, keepdims=True) for m in m8]
            if do_b:
                sums[tb] = [jnp.sum(l, axis=0, keepdims=True) for l in l8]
            if do_c:
                finish(o_ref, tc, ot, lc[0])

    run = pipeline_bounded if bounded else pipeline
    ctx_keys, lat_keys = (kc_ref, vtc_ref, knc_ref), (kl_ref, vtl_ref, knl_ref)
    lmin = run(ql_ref, ol_ref, [ctx_keys, lat_keys])
    if with_ctx:
        lmin_ctx = run(qc_ref, oc_ref, [ctx_keys])
    if bounded:
        if with_ctx:
            lmin = jnp.minimum(lmin, lmin_ctx)
        lmin_ref[...] = jnp.broadcast_to(lmin, lmin_ref.shape)


def _attention_call(qkv_ctx, qkv_lat, lam_vecs, g, layer, lam_init, with_ctx, bounded):
    q_lat, k_lat, vt_lat, kn_lat = qkv_lat
    k_ctx, vt_ctx, kn_ctx = qkv_ctx[-3:]
    b, h, n_lat, _ = k_lat.shape
    n_ctx = k_ctx.shape[2] // b
    lat_tiles, ctx_tiles = n_lat // TM, n_ctx // TM

    def tiles_spec(tiles, tail, ctx):
        zeros = (0,) * len(tail)
        if ctx:
            return pl.BlockSpec((None, None, tiles) + tail, lambda i, j: (0, j, i) + zeros)
        return pl.BlockSpec((None, None, tiles) + tail, lambda i, j: (i, j, 0) + zeros)

    q_tile, kn_tile = (2, 2 * LANES, TM), (SUBLANES, LANES)
    in_specs = [_resident(v, layer) for v in lam_vecs] + [
        _resident(g, layer),
        pl.BlockSpec((None, None, n_ctx, LANES), lambda i, j: (0, j, i, 0)),
        pl.BlockSpec((None, None, V_DIM, n_ctx), lambda i, j: (0, j, 0, i)),
        tiles_spec(ctx_tiles, kn_tile, True),
        tiles_spec(lat_tiles, q_tile, False),
        pl.BlockSpec((None, None, n_lat, LANES), lambda i, j: (i, j, 0, 0)),
        pl.BlockSpec((None, None, V_DIM, n_lat), lambda i, j: (i, j, 0, 0)),
        tiles_spec(lat_tiles, kn_tile, False),
    ]
    operands = list(lam_vecs) + [g, k_ctx, vt_ctx, kn_ctx, q_lat, k_lat, vt_lat, kn_lat]
    out_specs = [pl.BlockSpec((None, n_lat, V_DIM), lambda i, j: (i, 0, j))]
    out_shape = [jax.ShapeDtypeStruct((b, n_lat, h * V_DIM), BF16)]
    if with_ctx:
        in_specs.append(tiles_spec(ctx_tiles, q_tile, True))
        operands.append(qkv_ctx[0])
        out_specs.insert(0, pl.BlockSpec((None, n_ctx, V_DIM), lambda i, j: (i, 0, j)))
        out_shape.insert(0, jax.ShapeDtypeStruct((b, n_ctx, h * V_DIM), BF16))
    if bounded:
        out_specs.append(pl.BlockSpec((None, None, SUBLANES, LANES), lambda i, j: (i, j, 0, 0)))
        out_shape.append(jax.ShapeDtypeStruct((b, h, SUBLANES, LANES), F32))
    return pl.pallas_call(
        functools.partial(_attn_kernel, lam_init=lam_init, with_ctx=with_ctx, bounded=bounded),
        grid=(b, h),
        in_specs=in_specs,
        out_specs=out_specs,
        out_shape=out_shape,
        scratch_shapes=[pltpu.VMEM((2 if bounded else 3, 2, n_ctx + n_lat, TM), F32)],
        compiler_params=_params(("arbitrary", "arbitrary"), 48),
        name="attention_bounded" if bounded else "attention_exact",
    )(*operands)


def _attention(qkv_ctx, qkv_lat, lam_vecs, g, layer, lam_init, with_ctx):
    args = (qkv_ctx, qkv_lat, lam_vecs, g, layer, lam_init, with_ctx)
    *outs, lmin = _attention_call(*args, bounded=True)
    safe = jnp.min(lmin) >= MIN_SOFTMAX_DENOMINATOR
    return lax.cond(safe, lambda: tuple(outs), lambda: tuple(_attention_call(*args, bounded=False)))


def _fourier_kernel(u_ref, chan_c_ref, chan_s_ref, wf_ref, pos_c_ref, pos_s_ref, mid_ref, flip_ref, y_ref):
    n, width = u_ref.shape
    half = n // 2
    scale = (n * width // FOURIER_GROUPS) ** -0.5
    u = u_ref[...]
    a = _dot(pos_c_ref[...], u).astype(BF16)
    b = _dot(pos_s_ref[...], u).astype(BF16)
    p = _dot(a, chan_c_ref[...])
    q = _dot(b, chan_s_ref[...])
    y_ref[0:half, :] = _dot(((p - q) * scale).astype(BF16), wf_ref[...]).astype(BF16)
    mirrored = _dot(flip_ref[...], ((p + q) * scale).astype(BF16))
    mid = _dot(_dot(mid_ref[...], u).astype(BF16), chan_c_ref[...]) * scale
    rows = mid.shape[0]
    upper = jnp.concatenate([mirrored[0:rows] + mid, mirrored[rows:]], axis=0)
    y_ref[half:, :] = _dot(upper.astype(BF16), wf_ref[...]).astype(BF16)


def _fourier(uf, chan_c, chan_s, wf_bd, layer, pos_tables):
    b, n, fw = uf.shape
    half = n // 2
    idx = jnp.arange(half)
    flip = (idx[:, None] + idx[None, :] == half).astype(BF16)
    mid = np.zeros((SUBLANES, n), np.float32)
    mid[0] = 1.0 - 2.0 * (np.arange(n) % 2)
    mid = jnp.asarray(mid).astype(BF16)
    row_spec = pl.BlockSpec((None, n, fw), lambda i: (i, 0, 0))
    return pl.pallas_call(
        _fourier_kernel,
        grid=(b,),
        in_specs=[row_spec, _resident(chan_c), _resident(chan_s), _resident(wf_bd, layer)]
                 + [_resident(t) for t in pos_tables] + [_resident(mid), _resident(flip)],
        out_specs=row_spec,
        out_shape=jax.ShapeDtypeStruct((b, n, fw), BF16),
        compiler_params=_params(("arbitrary",), 48),
        name="fourier",
    )(uf, chan_c, chan_s, wf_bd, *pos_tables, mid, flip)


def _split_dot(x, m):
    hi = x.astype(BF16)
    lo = (x - hi.astype(F32)).astype(BF16)
    return _dot(hi, m) + _dot(lo, m)


def _conv_kernel(u_ref, w_ref, b_ref, lg_ref, lb_ref, avg_ref, pw_ref, y_ref, zpad_ref, acc_a_ref, acc_b_ref):
    n, cw = y_ref.shape
    halo = jnp.zeros((CONV_HALO, cw), F32)

    def taps(i, acc_ref, slot):
        r0 = pl.multiple_of(i * CONV_CHUNK, CONV_CHUNK)
        for lo in range(0, cw, LANES):
            win = zpad_ref[pl.ds(r0, CONV_CHUNK + 2 * CONV_HALO), lo:lo + LANES]
            part = jnp.zeros((CONV_CHUNK, LANES), F32) + b_ref[:, lo:lo + LANES]
            for shift in range(SUBLANES):
                rolled = win if shift == 0 else pltpu.roll(win, win.shape[0] - shift, 0)
                for aligned in range(0, 2 * CONV_HALO, SUBLANES):
                    tap = aligned + shift - (CONV_HALO - CONV_K // 2)
                    if 0 <= tap < CONV_K:
                        part = part + rolled[aligned:aligned + CONV_CHUNK] * w_ref[tap:tap + 1, lo:lo + LANES]
            acc_ref[slot, :, lo:lo + LANES] = part

    def project(acc_ref, slot):
        acc = acc_ref[slot]
        mu = _split_dot(acc, avg_ref[...])
        dev = acc - mu
        var = _split_dot(dev * dev, avg_ref[...])
        zn = dev * lax.rsqrt(var + EPS) * lg_ref[...] + lb_ref[...]
        act = (zn * _sigmoid(zn)).astype(BF16)
        return _dot(act, pw_ref[...]).astype(BF16)

    def round_(first_tap, tap_ref, ready_ref):
        if first_tap is not None:
            taps(first_tap, tap_ref, 0)
            taps(first_tap + 1, tap_ref, 1)
        return [project(ready_ref, 0), project(ready_ref, 1)]

    def store(first_chunk, ys):
        r0 = pl.multiple_of(first_chunk * CONV_CHUNK, 2 * CONV_CHUNK)
        y_ref[pl.ds(r0, len(ys) * CONV_CHUNK), :] = jnp.concatenate(ys, axis=0)

    zpad_ref[0:CONV_HALO, :] = halo
    zpad_ref[CONV_HALO:CONV_HALO + n, :] = u_ref[:, 0:cw] * _sigmoid(u_ref[:, cw:2 * cw])
    zpad_ref[CONV_HALO + n:2 * CONV_HALO + n, :] = halo

    chunks = n // CONV_CHUNK
    assert chunks == 2 or chunks % 4 == 0

    def step(j, carry):
        c0 = 4 * j
        ys = round_(c0 + 2, acc_b_ref, acc_a_ref) + round_(c0 + 4, acc_a_ref, acc_b_ref)
        store(c0, ys)
        return carry

    taps(0, acc_a_ref, 0)
    taps(1, acc_a_ref, 1)
    if chunks == 2:
        store(0, round_(None, None, acc_a_ref))
    else:
        lax.fori_loop(0, chunks // 4 - 1, step, 0)
        store(chunks - 4, round_(chunks - 2, acc_b_ref, acc_a_ref) + round_(None, None, acc_b_ref))


def _conv(uc, conv_w, conv_b, ln_g, ln_b, avg, w_pw, layer):
    b, n, cw2 = uc.shape
    cw = cw2 // 2
    return pl.pallas_call(
        _conv_kernel,
        grid=(b,),
        in_specs=[
            pl.BlockSpec((None, n, cw2), lambda i: (i, 0, 0)),
            _resident(conv_w, layer), _resident(conv_b, layer), _resident(ln_g, layer), _resident(ln_b, layer),
            _resident(avg), _resident(w_pw, layer),
        ],
        out_specs=pl.BlockSpec((None, n, cw), lambda i: (i, 0, 0)),
        out_shape=jax.ShapeDtypeStruct((b, n, cw), BF16),
        scratch_shapes=[pltpu.VMEM((n + 2 * CONV_HALO, cw), F32), pltpu.VMEM((2, CONV_CHUNK, cw), F32),
                        pltpu.VMEM((2, CONV_CHUNK, cw), F32)],
        compiler_params=_params(("arbitrary",), 40),
        name="conv",
    )(uc, conv_w, conv_b, ln_g, ln_b, avg, w_pw)


def _ffn_kernel(x_ref, o_ref, yf_ref, yc_ref, mod_ref, g_ref, wout_ref, w1_ref, w3_ref, w2_ref, fg_ref, out_ref,
                *, final, n_sub):
    aw = o_ref.shape[-1]
    fw = yf_ref.shape[-1]
    for u in range(n_sub):
        rows = slice(u * TM, (u + 1) * TM)
        y = (_dot(o_ref[rows, :], wout_ref[0:aw, :]) + _dot(yf_ref[rows, :], wout_ref[aw:aw + fw, :])
             + _dot(yc_ref[rows, :], wout_ref[aw + fw:, :]))
        x1 = x_ref[rows, :] + mod_ref[2:3, :] * y
        h = (_rms(x1) * g_ref[...] * (1.0 + mod_ref[4:5, :]) + mod_ref[3:4, :]).astype(BF16)
        a = _dot(h, w1_ref[...])
        gated = (a * _sigmoid(a) * _dot(h, w3_ref[...])).astype(BF16)
        x2 = x1 + mod_ref[5:6, :] * _dot(gated, w2_ref[...])
        if final:
            x2 = _rms(x2) * fg_ref[...]
        out_ref[rows, :] = x2


def _out_ffn(x, o, yf, yc, mod, g2, w_out, w1, w3, w2, final_g, layer, fixed_row, final):
    b, n, d = x.shape
    n_sub = ROW_SUBTILES
    bm = n_sub * TM
    assert n % bm == 0

    def rows(width):
        return pl.BlockSpec((None, bm, width), lambda i, j: (i, j, 0))

    return pl.pallas_call(
        functools.partial(_ffn_kernel, final=final, n_sub=n_sub),
        grid=(b, n // bm),
        in_specs=[
            rows(d), rows(o.shape[-1]), rows(yf.shape[-1]), rows(yc.shape[-1]),
            _mod_spec(mod, layer, fixed_row),
            _resident(g2, layer), _resident(w_out, layer), _resident(w1, layer), _resident(w3, layer),
            _resident(w2, layer), _resident(final_g),
        ],
        out_specs=rows(d),
        out_shape=jax.ShapeDtypeStruct((b, n, d), F32),
        compiler_params=_params(("arbitrary", "arbitrary"), 58),
        name="out_ffn",
    )(x, o, yf, yc, mod, g2, w_out, w1, w3, w2, final_g)


def _rope_tables(n_lat):
    n_freq = QK_DIM // 4
    tok = np.arange(n_lat)
    inv_freq = np.float32(ROPE_BASE) ** (-np.arange(n_freq, dtype=np.float32) / np.float32(n_freq))
    ang_r = ((tok // GRID_W).astype(np.float32)[:, None] * inv_freq).astype(np.float64)
    ang_c = ((tok % GRID_W).astype(np.float32)[:, None] * inv_freq).astype(np.float64)
    cos = np.concatenate([np.cos(ang_r)] * 2 + [np.cos(ang_c)] * 2, axis=-1)
    sin = np.concatenate([-np.sin(ang_r), np.sin(ang_r), -np.sin(ang_c), np.sin(ang_c)], axis=-1)
    reps = LANES // QK_DIM
    return (jnp.asarray(np.tile(cos, (1, reps)), dtype=F32), jnp.asarray(np.tile(sin, (1, reps)), dtype=F32))


def _dft_angles(rows, n):
    return 2.0 * np.pi * ((np.asarray(rows, np.int64)[:, None] * np.arange(n, dtype=np.int64)[None, :]) % n) / n


def _dft_tables_small(n):
    ang = _dft_angles(np.arange(n), n)
    return np.cos(ang).astype(np.float32), np.sin(ang).astype(np.float32)


def _dft_tables(n):
    rows = n // 2
    if n <= 4 * DFT_LO:
        return [jnp.asarray(t[:rows]).astype(BF16) for t in _dft_tables_small(n)]
    hi = rows // DFT_LO
    ang_hi = _dft_angles(np.arange(hi) * DFT_LO, n)
    ang_lo = _dft_angles(np.arange(DFT_LO), n)
    ch, sh = (jnp.asarray(f(ang_hi), dtype=F32)[:, None, :] for f in (np.cos, np.sin))
    cl, sl = (jnp.asarray(f(ang_lo), dtype=F32)[None, :, :] for f in (np.cos, np.sin))
    return [(ch * cl - sh * sl).reshape(rows, n).astype(BF16), (sh * cl + ch * sl).reshape(rows, n).astype(BF16)]


def _block_diag(blocks):
    g, r, c = blocks.shape[-3:]
    eye = jnp.eye(g, dtype=blocks.dtype)
    out = eye[:, None, :, None] * blocks[..., :, :, None, :]
    return out.reshape(blocks.shape[:-3] + (g * r, g * c))


def _permute_inproj(w_in, qk_w):
    lead = w_in.shape[:-1]
    qk = w_in[..., :4 * qk_w].reshape(lead + (2, 2, HEADS, QK_DIM))
    qk = jnp.swapaxes(qk, -3, -2).reshape(lead + (4 * qk_w,))
    return jnp.concatenate([qk, w_in[..., 4 * qk_w:]], axis=-1)


def kernel(x, c, ctx, c_ctx, w_ada, b_ada, norm1_g, norm2_g, w_in, lam_q1, lam_k1, lam_q2, lam_k2, subln_g,
           w_fourier, conv_w, conv_b, conv_ln_g, conv_ln_b, w_conv_out, w_out, w_ffn1, w_ffn3, w_ffn2, final_g):
    b, n_lat, d = x.shape
    n_ctx = ctx.shape[1]
    depth = w_ada.shape[0]
    fw = w_fourier.shape[1] * w_fourier.shape[2]
    cw = conv_w.shape[-1]
    qk_w = HEADS * QK_DIM
    attn_w = HEADS * V_DIM
    assert n_lat % GRID_W == 0 and n_ctx % TM == 0
    assert w_in.shape[-1] == 4 * qk_w + attn_w + fw + 2 * cw

    pad = (-(b + 1)) % SUBLANES
    c_rows = jnp.concatenate([c, c_ctx[None, :], jnp.zeros((pad, d), c.dtype)], axis=0)
    mod = _ada(c_rows, w_ada, b_ada)
    mod = mod.reshape(depth, mod.shape[1], 6, d)
    ctx_row = b

    rope = _rope_tables(n_lat)
    cc, cs = _dft_tables_small(fw // FOURIER_GROUPS)
    eye = np.eye(FOURIER_GROUPS, dtype=np.float32)
    chan_c = jnp.asarray(np.kron(eye, cc)).astype(BF16)
    chan_s = jnp.asarray(np.kron(eye, cs)).astype(BF16)
    lat_tables = _dft_tables(n_lat)
    ctx_tables = _dft_tables(n_ctx)
    group = cw // CONV_GROUPS
    avg = jnp.asarray(np.kron(np.eye(CONV_GROUPS), np.full((group, group), 1.0 / group)), dtype=F32).astype(BF16)

    w_in_b = _permute_inproj(w_in, qk_w).astype(BF16)
    wf_bd = _block_diag(w_fourier).astype(BF16)
    w_pw_b = w_conv_out.astype(BF16)
    w_out_b = w1_b = w3_b = w2_b = None

    def per_layer_rows(a):
        return a.reshape(depth, 1, a.shape[-1])

    norm1, norm2, subln = per_layer_rows(norm1_g), per_layer_rows(norm2_g), per_layer_rows(subln_g)
    lam_vecs = [per_layer_rows(a) for a in (lam_q1, lam_k1, lam_q2, lam_k2)]
    cb, lg, lb = per_layer_rows(conv_b), per_layer_rows(conv_ln_g), per_layer_rows(conv_ln_b)
    fg = final_g[None, :]

    def mixers(u_f, u_c, tables, l):
        yf = _fourier(u_f, chan_c, chan_s, wf_bd, l, tables)
        yc = _conv(u_c, conv_w, cb, lg, lb, avg, w_pw_b, l)
        return yf, yc

    ctx_flat = ctx.reshape(1, b * n_ctx, d)
    for l in range(depth):
        last = l == depth - 1
        lam_init = 0.8 - 0.6 * math.exp(-0.3 * l)
        if l == 0:
            q, k, vt, kn, uf, uc, w_out_b, w1_b, w3_b, w2_b = _inproj(
                x, mod, norm1, w_in_b, rope, l, None, fw, 2 * cw, cast=(w_out, w_ffn1, w_ffn3, w_ffn2))
        else:
            q, k, vt, kn, uf, uc = _inproj(x, mod, norm1, w_in_b, rope, l, None, fw, 2 * cw)
        yf, yc = mixers(uf, uc, lat_tables, l)
        if last:
            kv_ctx = _inproj(ctx_flat, mod, norm1, w_in_b, None, l, ctx_row, fw, 2 * cw, kv_only=True)
            (o,) = _attention(kv_ctx, (q, k, vt, kn), lam_vecs, subln, l, lam_init, with_ctx=False)
        else:
            qc, kc, vtc, knc, ufc, ucc = _inproj(ctx_flat, mod, norm1, w_in_b, None, l, ctx_row, fw, 2 * cw)
            oc, o = _attention((qc, kc, vtc, knc), (q, k, vt, kn), lam_vecs, subln, l, lam_init, with_ctx=True)
            yfc, ycc = mixers(ufc.reshape(b, n_ctx, fw), ucc.reshape(b, n_ctx, 2 * cw), ctx_tables, l)
            ctx_flat = _out_ffn(ctx_flat, oc.reshape(1, b * n_ctx, attn_w), yfc.reshape(1, b * n_ctx, fw),
                                ycc.reshape(1, b * n_ctx, cw), mod, norm2, w_out_b, w1_b, w3_b, w2_b, fg, l,
                                ctx_row, final=False)
        x = _out_ffn(x, o, yf, yc, mod, norm2, w_out_b, w1_b, w3_b, w2_b, fg, l, None, final=last)
    return x
```

```python
import functools
import math

import numpy as np
import jax
import jax.numpy as jnp
from jax import lax
from jax.experimental import pallas as pl
from jax.experimental.pallas import tpu as pltpu

F32 = jnp.float32
BF16 = jnp.bfloat16

GRID_W = 64
HEADS = 4
QK_DIM = 64
V_DIM = 2 * QK_DIM
FOURIER_GROUPS = 4
CONV_GROUPS = 4
CONV_K = 31
ROPE_BASE = 10000.0
EPS = 1e-6
LOG2_E = math.log2(math.e)
BOUND_SLACK = 1.0 + 2.0 ** -6
MIN_SOFTMAX_DENOMINATOR = 2.0 ** -90

LANES = 128
SUBLANES = 8
TM = 256
ROW_SUBTILES = 2
ATT_KC = 256
CONV_HALO = 16
CONV_CHUNK = 128
ADA_TN = 1024
DFT_LO = 64
MIB = 1024 * 1024


def _params(sem, vmem_mib):
    return pltpu.CompilerParams(dimension_semantics=sem, vmem_limit_bytes=vmem_mib * MIB)


def _resident(arr, layer=None):
    if layer is None:
        idx = (0,) * arr.ndim
        return pl.BlockSpec(arr.shape, lambda *_: idx, pipeline_mode=pl.Buffered(1))
    idx = (layer,) + (0,) * (arr.ndim - 1)
    return pl.BlockSpec((None,) + arr.shape[1:], lambda *_: idx, pipeline_mode=pl.Buffered(1))


def _mod_spec(mod, layer, fixed_row):
    if fixed_row is None:
        return pl.BlockSpec((None, None) + mod.shape[2:], lambda i, j: (layer, i, 0, 0))
    return pl.BlockSpec((None, None) + mod.shape[2:], lambda i, j: (layer, fixed_row, 0, 0))


def _cast_slabs(weights, grid):
    steps = grid[0] * grid[1]
    operands, specs, shapes = [], [], []
    for wt in weights:
        flat = wt.reshape(-1, wt.shape[-1])
        slab = flat.shape[0] // steps
        assert slab * steps == flat.shape[0] and slab % (2 * SUBLANES) == 0
        operands.append(flat)
        specs.append(pl.BlockSpec((slab, flat.shape[1]), lambda i, j: (i * grid[1] + j, 0)))
        shapes.append(jax.ShapeDtypeStruct(flat.shape, BF16))
    return operands, specs, shapes


def _sigmoid(x):
    return 1.0 / (1.0 + jnp.exp(-x))


def _rms(x):
    return x * lax.rsqrt(jnp.mean(x * x, axis=-1, keepdims=True) + EPS)


def _dot(a, b):
    return jnp.dot(a, b, preferred_element_type=F32)


def _ada_kernel(c_ref, w_ref, b_ref, o_ref):
    c = c_ref[...]
    s = (c * _sigmoid(c)).astype(BF16)
    o_ref[...] = _dot(s, w_ref[...].astype(BF16)) + b_ref[...]


def _ada(c_rows, w_ada, b_ada):
    depth, d, n = w_ada.shape
    rows = c_rows.shape[0]
    return pl.pallas_call(
        _ada_kernel,
        grid=(depth, n // ADA_TN),
        in_specs=[
            pl.BlockSpec((rows, d), lambda l, j: (0, 0)),
            pl.BlockSpec((None, d, ADA_TN), lambda l, j: (l, 0, j)),
            pl.BlockSpec((None, 1, ADA_TN), lambda l, j: (l, 0, j)),
        ],
        out_specs=pl.BlockSpec((None, rows, ADA_TN), lambda l, j: (l, 0, j)),
        out_shape=jax.ShapeDtypeStruct((depth, rows, n), F32),
        compiler_params=_params(("arbitrary", "arbitrary"), 24),
        name="ada",
    )(c_rows, w_ada, b_ada.reshape(depth, 1, n))


def _inproj_kernel(*refs, use_rope, kv_only, n_sub):
    x_ref, mod_ref, g_ref, w_ref = refs[:4]
    refs = refs[4:]
    if use_rope:
        cos_ref, sin_ref = refs[:2]
        refs = refs[2:]
    hw = HEADS * LANES
    lane = lax.broadcasted_iota(jnp.int32, (TM, LANES), 1)
    first_half = (lane & (QK_DIM // 4)) == 0
    row = lax.broadcasted_iota(jnp.int32, (LANES, TM), 0)
    zero = jnp.zeros((LANES, TM), F32)
    dim = lax.broadcasted_iota(jnp.int32, (LANES, LANES), 0)
    col = lax.broadcasted_iota(jnp.int32, (LANES, LANES), 1)
    map_sum = jnp.where(col == jnp.where(dim < QK_DIM, 0, 1), 1.0, 0.0).astype(BF16)

    for u in range(n_sub):
        rows = slice(u * TM, (u + 1) * TM)
        h = (_rms(x_ref[rows, :]) * g_ref[...] * (1.0 + mod_ref[1:2, :]) + mod_ref[0:1, :]).astype(BF16)

        def rope(t):
            if not use_rope:
                return t
            partner = jnp.where(first_half, pltpu.roll(t, LANES - QK_DIM // 4, 1), pltpu.roll(t, QK_DIM // 4, 1))
            return t * cos_ref[rows, :] + partner * sin_ref[rows, :]

        def put_keys(j, kf):
            k_ref[j, rows, :] = kf.astype(BF16)
            norms = _dot((kf * kf).astype(BF16), map_sum)
            kn_ref[j, u] = jnp.max(norms.reshape(TM // SUBLANES, SUBLANES, LANES), axis=0)

        if kv_only:
            k_ref, vt_ref, kn_ref = refs
            r = _dot(h, w_ref[:, hw:3 * hw])
            for j in range(HEADS):
                put_keys(j, rope(r[:, j * LANES:(j + 1) * LANES]))
                vt_ref[j, :, rows] = r[:, hw + j * LANES:hw + (j + 1) * LANES].T.astype(BF16)
            continue

        q_ref, k_ref, vt_ref, kn_ref, uf_ref, uc_ref = refs
        r = _dot(h, w_ref[...])
        for j in range(HEADS):
            qt = (rope(r[:, j * LANES:(j + 1) * LANES]) * (QK_DIM ** -0.5 * LOG2_E)).T
            for mp in range(2):
                sel = (row < QK_DIM) == (mp == 0)
                qm = qt[mp * QK_DIM:(mp + 1) * QK_DIM, :]
                qn = jnp.sqrt(jnp.sum(qm * qm, axis=0, keepdims=True)) * BOUND_SLACK
                q_ref[j, u, mp, 0:LANES, :] = jnp.where(sel, qt, zero).astype(BF16)
                q_ref[j, u, mp, LANES:2 * LANES, :] = jnp.where(row == 0, qn, zero).astype(BF16)
            put_keys(j, rope(r[:, hw + j * LANES:hw + (j + 1) * LANES]))
            vt_ref[j, :, rows] = r[:, 2 * hw + j * LANES:2 * hw + (j + 1) * LANES].T.astype(BF16)
        fw = uf_ref.shape[-1]
        uf_ref[rows, :] = r[:, 3 * hw:3 * hw + fw].astype(BF16)
        uc_ref[rows, :] = r[:, 3 * hw + fw:]


def _inproj(x, mod, g, w, rope_tables, layer, fixed_row, fw, cw2, kv_only=False):
    b, n, d = x.shape
    n_sub = ROW_SUBTILES
    bm = n_sub * TM
    assert n % bm == 0
    use_rope = rope_tables is not None
    in_specs = [
        pl.BlockSpec((None, bm, d), lambda i, j: (i, j, 0)),
        _mod_spec(mod, layer, fixed_row),
        _resident(g, layer),
        _resident(w, layer),
    ]
    operands = [x, mod, g, w]
    if use_rope:
        in_specs += [pl.BlockSpec((bm, LANES), lambda i, j: (j, 0))] * 2
        operands += list(rope_tables)
    out_specs = [
        pl.BlockSpec((None, HEADS, n_sub, 2, 2 * LANES, TM), lambda i, j: (i, 0, j, 0, 0, 0)),
        pl.BlockSpec((None, HEADS, bm, LANES), lambda i, j: (i, 0, j, 0)),
        pl.BlockSpec((None, HEADS, V_DIM, bm), lambda i, j: (i, 0, 0, j)),
        pl.BlockSpec((None, HEADS, n_sub, SUBLANES, LANES), lambda i, j: (i, 0, j, 0, 0)),
        pl.BlockSpec((None, bm, fw), lambda i, j: (i, j, 0)),
        pl.BlockSpec((None, bm, cw2), lambda i, j: (i, j, 0)),
    ]
    out_shape = [
        jax.ShapeDtypeStruct((b, HEADS, n // TM, 2, 2 * LANES, TM), BF16),
        jax.ShapeDtypeStruct((b, HEADS, n, LANES), BF16),
        jax.ShapeDtypeStruct((b, HEADS, V_DIM, n), BF16),
        jax.ShapeDtypeStruct((b, HEADS, n // TM, SUBLANES, LANES), F32),
        jax.ShapeDtypeStruct((b, n, fw), BF16),
        jax.ShapeDtypeStruct((b, n, cw2), F32),
    ]
    if kv_only:
        out_specs, out_shape = out_specs[1:4], out_shape[1:4]
    return pl.pallas_call(
        functools.partial(_inproj_kernel, use_rope=use_rope, kv_only=kv_only, n_sub=n_sub),
        grid=(b, n // bm),
        in_specs=in_specs,
        out_specs=out_specs,
        out_shape=out_shape,
        compiler_params=_params(("arbitrary", "arbitrary"), 48),
        name="inproj",
    )(*operands)


def _attn_kernel(*refs, lam_init, with_ctx, bounded, n_cast):
    lq1_ref, lk1_ref, lq2_ref, lk2_ref, g_ref, kc_ref, vtc_ref, knc_ref, ql_ref, kl_ref, vtl_ref, knl_ref = refs[:12]
    refs = list(refs[12:])
    s_ref = refs.pop()
    qc_ref = refs.pop(0) if with_ctx else None
    cast_src = [refs.pop(0) for _ in range(n_cast)]
    oc_ref = refs.pop(0) if with_ctx else None
    ol_ref = refs.pop(0)
    lmin_ref = refs.pop(0) if bounded else None
    for src_ref, dst_ref in zip(cast_src, refs):
        dst_ref[...] = src_ref[...].astype(BF16)
    lam = (jnp.exp(jnp.sum(lq1_ref[...] * lk1_ref[...], axis=-1, keepdims=True))
           - jnp.exp(jnp.sum(lq2_ref[...] * lk2_ref[...], axis=-1, keepdims=True)) + lam_init)

    def fold(x):
        return x.reshape(x.shape[0] // SUBLANES, SUBLANES, x.shape[1])

    def finish(o_ref, t, ot, l1):
        ot = ot * (1.0 / l1)
        ms = jnp.mean(ot * ot, axis=0, keepdims=True)
        on = (ot * lax.rsqrt(ms + EPS)).T * g_ref[...] * (1.0 - lam_init)
        o_ref[t * TM:(t + 1) * TM, :] = on.astype(BF16)

    def key_chunks(key_refs):
        return [(k_ref, vt_ref, c) for k_ref, vt_ref, _ in key_refs for c in range(0, k_ref.shape[0], ATT_KC)]

    def pipeline_bounded(q_ref, o_ref, key_refs):
        chunks = key_chunks(key_refs)
        n_tiles = q_ref.shape[0]
        lane = lax.broadcasted_iota(jnp.int32, (ATT_KC, LANES), 1)
        key_cols = []
        ksq = functools.reduce(jnp.maximum, [jnp.max(kn_ref[...], axis=0) for _, _, kn_ref in key_refs])
        kb = jnp.sqrt(jnp.max(ksq, axis=0, keepdims=True)) * BOUND_SLACK
        for mp in range(2):
            key_cols.append(jnp.where(lane == 0, -kb[:, mp:mp + 1], 0.0).astype(BF16))
        sums = {}
        lmin = None
        for u in range(n_tiles + 1):
            ta, tc = u, u - 1
            do_a, do_c = ta < n_tiles, 0 <= tc < n_tiles
            if do_a:
                qts = [q_ref[ta, mp] for mp in range(2)]
                l8 = [jnp.zeros((SUBLANES, TM), F32) for _ in range(2)]
            if do_c:
                lc = sums.pop(tc)
                rho = lam * lc[0] / lc[1]
                ot = None
            for ci, (k_ref, vt_ref, c) in enumerate(chunks):
                rows = slice(ci * ATT_KC, (ci + 1) * ATT_KC)
                if do_a:
                    for mp in range(2):
                        keys = jnp.concatenate([k_ref[c:c + ATT_KC, :], key_cols[mp]], axis=1)
                        e = jnp.exp2(_dot(keys, qts[mp]))
                        s_ref[ta % 2, mp, rows, :] = e
                        l8[mp] = l8[mp] + jnp.sum(fold(e), axis=0)
                if do_c:
                    p = (s_ref[tc % 2, 0, rows, :] - rho * s_ref[tc % 2, 1, rows, :]).astype(BF16)
                    part = _dot(vt_ref[:, c:c + ATT_KC], p)
                    ot = part if ot is None else ot + part
            if do_a:
                sums[ta] = [jnp.sum(l, axis=0, keepdims=True) for l in l8]
                low = jnp.minimum(sums[ta][0], sums[ta][1])
                lmin = low if lmin is None else jnp.minimum(lmin, low)
            if do_c:
                finish(o_ref, tc, ot, lc[0])
        return jnp.min(lmin, axis=1, keepdims=True)

    def pipeline(q_ref, o_ref, key_refs):
        chunks = key_chunks(key_refs)
        n_tiles = q_ref.shape[0]
        maxima, sums = {}, {}
        for u in range(n_tiles + 2):
            ta, tb, tc = u, u - 1, u - 2
            do_a, do_b, do_c = ta < n_tiles, 0 <= tb < n_tiles, 0 <= tc < n_tiles
            if do_a:
                qts = [q_ref[ta, mp, 0:LANES, :] for mp in range(2)]
                m8 = [None, None]
            if do_b:
                mb = maxima.pop(tb)
                l8 = [jnp.zeros((SUBLANES, TM), F32) for _ in range(2)]
            if do_c:
                lc = sums.pop(tc)
                rho = lam * lc[0] / lc[1]
                ot = None
            for ci, (k_ref, vt_ref, c) in enumerate(chunks):
                rows = slice(ci * ATT_KC, (ci + 1) * ATT_KC)
                if do_a:
                    for mp in range(2):
                        s = _dot(k_ref[c:c + ATT_KC, :], qts[mp])
                        s_ref[ta % 3, mp, rows, :] = s
                        cm = jnp.max(fold(s), axis=0)
                        m8[mp] = cm if m8[mp] is None else jnp.maximum(m8[mp], cm)
                if do_b:
                    for mp in range(2):
                        e = jnp.exp2(s_ref[tb % 3, mp, rows, :] - mb[mp])
                        s_ref[tb % 3, mp, rows, :] = e
                        l8[mp] = l8[mp] + jnp.sum(fold(e), axis=0)
                if do_c:
                    p = (s_ref[tc % 3, 0, rows, :] - rho * s_ref[tc % 3, 1, rows, :]).astype(BF16)
                    part = _dot(vt_ref[:, c:c + ATT_KC], p)
                    ot = part if ot is None else ot + part
            if do_a:
                maxima[ta] = [jnp.max(m, axis=0, keepdims=True) for m in m8]
            if do_b:
                sums[tb] = [jnp.sum(l, axis=0, keepdims=True) for l in l8]
            if do_c:
                finish(o_ref, tc, ot, lc[0])

    run = pipeline_bounded if bounded else pipeline
    ctx_keys, lat_keys = (kc_ref, vtc_ref, knc_ref), (kl_ref, vtl_ref, knl_ref)
    lmin = run(ql_ref, ol_ref, [ctx_keys, lat_keys])
    if with_ctx:
        lmin_ctx = run(qc_ref, oc_ref, [ctx_keys])
    if bounded:
        if with_ctx:
            lmin = jnp.minimum(lmin, lmin_ctx)
        lmin_ref[...] = jnp.broadcast_to(lmin, lmin_ref.shape)


def _attention_call(qkv_ctx, qkv_lat, lam_vecs, g, layer, lam_init, with_ctx, bounded, cast=()):
    q_lat, k_lat, vt_lat, kn_lat = qkv_lat
    k_ctx, vt_ctx, kn_ctx = qkv_ctx[-3:]
    b, h, n_lat, _ = k_lat.shape
    n_ctx = k_ctx.shape[2] // b
    lat_tiles, ctx_tiles = n_lat // TM, n_ctx // TM

    def tiles_spec(tiles, tail, ctx):
        zeros = (0,) * len(tail)
        if ctx:
            return pl.BlockSpec((None, None, tiles) + tail, lambda i, j: (0, j, i) + zeros)
        return pl.BlockSpec((None, None, tiles) + tail, lambda i, j: (i, j, 0) + zeros)

    q_tile, kn_tile = (2, 2 * LANES, TM), (SUBLANES, LANES)
    in_specs = [_resident(v, layer) for v in lam_vecs] + [
        _resident(g, layer),
        pl.BlockSpec((None, None, n_ctx, LANES), lambda i, j: (0, j, i, 0)),
        pl.BlockSpec((None, None, V_DIM, n_ctx), lambda i, j: (0, j, 0, i)),
        tiles_spec(ctx_tiles, kn_tile, True),
        tiles_spec(lat_tiles, q_tile, False),
        pl.BlockSpec((None, None, n_lat, LANES), lambda i, j: (i, j, 0, 0)),
        pl.BlockSpec((None, None, V_DIM, n_lat), lambda i, j: (i, j, 0, 0)),
        tiles_spec(lat_tiles, kn_tile, False),
    ]
    operands = list(lam_vecs) + [g, k_ctx, vt_ctx, kn_ctx, q_lat, k_lat, vt_lat, kn_lat]
    out_specs = [pl.BlockSpec((None, n_lat, V_DIM), lambda i, j: (i, 0, j))]
    out_shape = [jax.ShapeDtypeStruct((b, n_lat, h * V_DIM), BF16)]
    if with_ctx:
        in_specs.append(tiles_spec(ctx_tiles, q_tile, True))
        operands.append(qkv_ctx[0])
        out_specs.insert(0, pl.BlockSpec((None, n_ctx, V_DIM), lambda i, j: (i, 0, j)))
        out_shape.insert(0, jax.ShapeDtypeStruct((b, n_ctx, h * V_DIM), BF16))
    if bounded:
        out_specs.append(pl.BlockSpec((None, None, SUBLANES, LANES), lambda i, j: (i, j, 0, 0)))
        out_shape.append(jax.ShapeDtypeStruct((b, h, SUBLANES, LANES), F32))
    cast_operands, cast_specs, cast_shapes = _cast_slabs(cast, (b, h))
    outs = pl.pallas_call(
        functools.partial(_attn_kernel, lam_init=lam_init, with_ctx=with_ctx, bounded=bounded, n_cast=len(cast)),
        grid=(b, h),
        in_specs=in_specs + cast_specs,
        out_specs=out_specs + cast_specs,
        out_shape=out_shape + cast_shapes,
        scratch_shapes=[pltpu.VMEM((2 if bounded else 3, 2, n_ctx + n_lat, TM), F32)],
        compiler_params=_params(("arbitrary", "arbitrary"), 56),
        name="attention_bounded" if bounded else "attention_exact",
    )(*operands, *cast_operands)
    n_main = len(out_shape)
    return list(outs[:n_main]), [o.reshape(wt.shape) for o, wt in zip(outs[n_main:], cast)]


def _attention(qkv_ctx, qkv_lat, lam_vecs, g, layer, lam_init, with_ctx, cast=()):
    args = (qkv_ctx, qkv_lat, lam_vecs, g, layer, lam_init, with_ctx)
    (*outs, lmin), casted = _attention_call(*args, bounded=True, cast=cast)
    safe = jnp.min(lmin) >= MIN_SOFTMAX_DENOMINATOR
    outs = lax.cond(safe, lambda: tuple(outs), lambda: tuple(_attention_call(*args, bounded=False)[0]))
    return list(outs), casted


def _fourier_kernel(u_ref, chan_c_ref, chan_s_ref, wf_ref, pos_c_ref, pos_s_ref, mid_ref, flip_ref, y_ref):
    n, width = u_ref.shape
    half = n // 2
    scale = (n * width // FOURIER_GROUPS) ** -0.5
    u = u_ref[...]
    a = _dot(pos_c_ref[...], u).astype(BF16)
    b = _dot(pos_s_ref[...], u).astype(BF16)
    p = _dot(a, chan_c_ref[...])
    q = _dot(b, chan_s_ref[...])
    y_ref[0:half, :] = _dot(((p - q) * scale).astype(BF16), wf_ref[...]).astype(BF16)
    mirrored = _dot(flip_ref[...], ((p + q) * scale).astype(BF16))
    mid = _dot(_dot(mid_ref[...], u).astype(BF16), chan_c_ref[...]) * scale
    rows = mid.shape[0]
    upper = jnp.concatenate([mirrored[0:rows] + mid, mirrored[rows:]], axis=0)
    y_ref[half:, :] = _dot(upper.astype(BF16), wf_ref[...]).astype(BF16)


def _fourier(uf, chan_c, chan_s, wf_bd, layer, pos_tables):
    b, n, fw = uf.shape
    half = n // 2
    idx = jnp.arange(half)
    flip = (idx[:, None] + idx[None, :] == half).astype(BF16)
    mid = np.zeros((SUBLANES, n), np.float32)
    mid[0] = 1.0 - 2.0 * (np.arange(n) % 2)
    mid = jnp.asarray(mid).astype(BF16)
    row_spec = pl.BlockSpec((None, n, fw), lambda i: (i, 0, 0))
    return pl.pallas_call(
        _fourier_kernel,
        grid=(b,),
        in_specs=[row_spec, _resident(chan_c), _resident(chan_s), _resident(wf_bd, layer)]
                 + [_resident(t) for t in pos_tables] + [_resident(mid), _resident(flip)],
        out_specs=row_spec,
        out_shape=jax.ShapeDtypeStruct((b, n, fw), BF16),
        compiler_params=_params(("arbitrary",), 48),
        name="fourier",
    )(uf, chan_c, chan_s, wf_bd, *pos_tables, mid, flip)


def _split_dot(x, m):
    hi = x.astype(BF16)
    lo = (x - hi.astype(F32)).astype(BF16)
    return _dot(hi, m) + _dot(lo, m)


def _conv_kernel(u_ref, w_ref, b_ref, lg_ref, lb_ref, avg_ref, pw_ref, y_ref, zpad_ref, acc_a_ref, acc_b_ref):
    n, cw = y_ref.shape
    halo = jnp.zeros((CONV_HALO, cw), F32)

    def taps(i, acc_ref, slot):
        r0 = pl.multiple_of(i * CONV_CHUNK, CONV_CHUNK)
        for lo in range(0, cw, LANES):
            win = zpad_ref[pl.ds(r0, CONV_CHUNK + 2 * CONV_HALO), lo:lo + LANES]
            part = jnp.zeros((CONV_CHUNK, LANES), F32) + b_ref[:, lo:lo + LANES]
            for shift in range(SUBLANES):
                rolled = win if shift == 0 else pltpu.roll(win, win.shape[0] - shift, 0)
                for aligned in range(0, 2 * CONV_HALO, SUBLANES):
                    tap = aligned + shift - (CONV_HALO - CONV_K // 2)
                    if 0 <= tap < CONV_K:
                        part = part + rolled[aligned:aligned + CONV_CHUNK] * w_ref[tap:tap + 1, lo:lo + LANES]
            acc_ref[slot, :, lo:lo + LANES] = part

    def project(acc_ref, slot):
        acc = acc_ref[slot]
        mu = _split_dot(acc, avg_ref[...])
        dev = acc - mu
        var = _split_dot(dev * dev, avg_ref[...])
        zn = dev * lax.rsqrt(var + EPS) * lg_ref[...] + lb_ref[...]
        act = (zn * _sigmoid(zn)).astype(BF16)
        return _dot(act, pw_ref[...]).astype(BF16)

    def round_(first_tap, tap_ref, ready_ref):
        if first_tap is not None:
            taps(first_tap, tap_ref, 0)
            taps(first_tap + 1, tap_ref, 1)
        return [project(ready_ref, 0), project(ready_ref, 1)]

    def store(first_chunk, ys):
        r0 = pl.multiple_of(first_chunk * CONV_CHUNK, 2 * CONV_CHUNK)
        y_ref[pl.ds(r0, len(ys) * CONV_CHUNK), :] = jnp.concatenate(ys, axis=0)

    zpad_ref[0:CONV_HALO, :] = halo
    zpad_ref[CONV_HALO:CONV_HALO + n, :] = u_ref[:, 0:cw] * _sigmoid(u_ref[:, cw:2 * cw])
    zpad_ref[CONV_HALO + n:2 * CONV_HALO + n, :] = halo

    chunks = n // CONV_CHUNK
    assert chunks == 2 or chunks % 4 == 0

    def step(j, carry):
        c0 = 4 * j
        ys = round_(c0 + 2, acc_b_ref, acc_a_ref) + round_(c0 + 4, acc_a_ref, acc_b_ref)
        store(c0, ys)
        return carry

    taps(0, acc_a_ref, 0)
    taps(1, acc_a_ref, 1)
    if chunks == 2:
        store(0, round_(None, None, acc_a_ref))
    else:
        lax.fori_loop(0, chunks // 4 - 1, step, 0)
        store(chunks - 4, round_(chunks - 2, acc_b_ref, acc_a_ref) + round_(None, None, acc_b_ref))


def _conv(uc, conv_w, conv_b, ln_g, ln_b, avg, w_pw, layer):
    b, n, cw2 = uc.shape
    cw = cw2 // 2
    return pl.pallas_call(
        _conv_kernel,
        grid=(b,),
        in_specs=[
            pl.BlockSpec((None, n, cw2), lambda i: (i, 0, 0)),
            _resident(conv_w, layer), _resident(conv_b, layer), _resident(ln_g, layer), _resident(ln_b, layer),
            _resident(avg), _resident(w_pw, layer),
        ],
        out_specs=pl.BlockSpec((None, n, cw), lambda i: (i, 0, 0)),
        out_shape=jax.ShapeDtypeStruct((b, n, cw), BF16),
        scratch_shapes=[pltpu.VMEM((n + 2 * CONV_HALO, cw), F32), pltpu.VMEM((2, CONV_CHUNK, cw), F32),
                        pltpu.VMEM((2, CONV_CHUNK, cw), F32)],
        compiler_params=_params(("arbitrary",), 40),
        name="conv",
    )(uc, conv_w, conv_b, ln_g, ln_b, avg, w_pw)


def _ffn_kernel(x_ref, o_ref, yf_ref, yc_ref, mod_ref, g_ref, wout_ref, w1_ref, w3_ref, w2_ref, fg_ref, out_ref,
                *, final, n_sub):
    aw = o_ref.shape[-1]
    fw = yf_ref.shape[-1]
    for u in range(n_sub):
        rows = slice(u * TM, (u + 1) * TM)
        y = (_dot(o_ref[rows, :], wout_ref[0:aw, :]) + _dot(yf_ref[rows, :], wout_ref[aw:aw + fw, :])
             + _dot(yc_ref[rows, :], wout_ref[aw + fw:, :]))
        x1 = x_ref[rows, :] + mod_ref[2:3, :] * y
        h = (_rms(x1) * g_ref[...] * (1.0 + mod_ref[4:5, :]) + mod_ref[3:4, :]).astype(BF16)
        a = _dot(h, w1_ref[...])
        gated = (a * _sigmoid(a) * _dot(h, w3_ref[...])).astype(BF16)
        x2 = x1 + mod_ref[5:6, :] * _dot(gated, w2_ref[...])
        if final:
            x2 = _rms(x2) * fg_ref[...]
        out_ref[rows, :] = x2


def _out_ffn(x, o, yf, yc, mod, g2, w_out, w1, w3, w2, final_g, layer, fixed_row, final):
    b, n, d = x.shape
    n_sub = ROW_SUBTILES
    bm = n_sub * TM
    assert n % bm == 0

    def rows(width):
        return pl.BlockSpec((None, bm, width), lambda i, j: (i, j, 0))

    return pl.pallas_call(
        functools.partial(_ffn_kernel, final=final, n_sub=n_sub),
        grid=(b, n // bm),
        in_specs=[
            rows(d), rows(o.shape[-1]), rows(yf.shape[-1]), rows(yc.shape[-1]),
            _mod_spec(mod, layer, fixed_row),
            _resident(g2, layer), _resident(w_out, layer), _resident(w1, layer), _resident(w3, layer),
            _resident(w2, layer), _resident(final_g),
        ],
        out_specs=rows(d),
        out_shape=jax.ShapeDtypeStruct((b, n, d), F32),
        compiler_params=_params(("arbitrary", "arbitrary"), 58),
        name="out_ffn",
    )(x, o, yf, yc, mod, g2, w_out, w1, w3, w2, final_g)


def _rope_tables(n_lat):
    n_freq = QK_DIM // 4
    tok = np.arange(n_lat)
    inv_freq = np.float32(ROPE_BASE) ** (-np.arange(n_freq, dtype=np.float32) / np.float32(n_freq))
    ang_r = ((tok // GRID_W).astype(np.float32)[:, None] * inv_freq).astype(np.float64)
    ang_c = ((tok % GRID_W).astype(np.float32)[:, None] * inv_freq).astype(np.float64)
    cos = np.concatenate([np.cos(ang_r)] * 2 + [np.cos(ang_c)] * 2, axis=-1)
    sin = np.concatenate([-np.sin(ang_r), np.sin(ang_r), -np.sin(ang_c), np.sin(ang_c)], axis=-1)
    reps = LANES // QK_DIM
    return (jnp.asarray(np.tile(cos, (1, reps)), dtype=F32), jnp.asarray(np.tile(sin, (1, reps)), dtype=F32))


def _dft_angles(rows, n):
    return 2.0 * np.pi * ((np.asarray(rows, np.int64)[:, None] * np.arange(n, dtype=np.int64)[None, :]) % n) / n


def _dft_tables_small(n):
    ang = _dft_angles(np.arange(n), n)
    return np.cos(ang).astype(np.float32), np.sin(ang).astype(np.float32)


def _dft_tables(n):
    rows = n // 2
    if n <= 4 * DFT_LO:
        return [jnp.asarray(t[:rows]).astype(BF16) for t in _dft_tables_small(n)]
    hi = rows // DFT_LO
    ang_hi = _dft_angles(np.arange(hi) * DFT_LO, n)
    ang_lo = _dft_angles(np.arange(DFT_LO), n)
    ch, sh = (jnp.asarray(f(ang_hi), dtype=F32)[:, None, :] for f in (np.cos, np.sin))
    cl, sl = (jnp.asarray(f(ang_lo), dtype=F32)[None, :, :] for f in (np.cos, np.sin))
    return [(ch * cl - sh * sl).reshape(rows, n).astype(BF16), (sh * cl + ch * sl).reshape(rows, n).astype(BF16)]


def _block_diag(blocks):
    g, r, c = blocks.shape[-3:]
    eye = jnp.eye(g, dtype=blocks.dtype)
    out = eye[:, None, :, None] * blocks[..., :, :, None, :]
    return out.reshape(blocks.shape[:-3] + (g * r, g * c))


def _permute_inproj(w_in, qk_w):
    lead = w_in.shape[:-1]
    qk = w_in[..., :4 * qk_w].reshape(lead + (2, 2, HEADS, QK_DIM))
    qk = jnp.swapaxes(qk, -3, -2).reshape(lead + (4 * qk_w,))
    return jnp.concatenate([qk, w_in[..., 4 * qk_w:]], axis=-1)


def kernel(x, c, ctx, c_ctx, w_ada, b_ada, norm1_g, norm2_g, w_in, lam_q1, lam_k1, lam_q2, lam_k2, subln_g,
           w_fourier, conv_w, conv_b, conv_ln_g, conv_ln_b, w_conv_out, w_out, w_ffn1, w_ffn3, w_ffn2, final_g):
    b, n_lat, d = x.shape
    n_ctx = ctx.shape[1]
    depth = w_ada.shape[0]
    fw = w_fourier.shape[1] * w_fourier.shape[2]
    cw = conv_w.shape[-1]
    qk_w = HEADS * QK_DIM
    attn_w = HEADS * V_DIM
    assert n_lat % GRID_W == 0 and n_ctx % TM == 0
    assert w_in.shape[-1] == 4 * qk_w + attn_w + fw + 2 * cw

    pad = (-(b + 1)) % SUBLANES
    c_rows = jnp.concatenate([c, c_ctx[None, :], jnp.zeros((pad, d), c.dtype)], axis=0)
    mod = _ada(c_rows, w_ada, b_ada)
    mod = mod.reshape(depth, mod.shape[1], 6, d)
    ctx_row = b

    rope = _rope_tables(n_lat)
    cc, cs = _dft_tables_small(fw // FOURIER_GROUPS)
    eye = np.eye(FOURIER_GROUPS, dtype=np.float32)
    chan_c = jnp.asarray(np.kron(eye, cc)).astype(BF16)
    chan_s = jnp.asarray(np.kron(eye, cs)).astype(BF16)
    lat_tables = _dft_tables(n_lat)
    ctx_tables = _dft_tables(n_ctx)
    group = cw // CONV_GROUPS
    avg = jnp.asarray(np.kron(np.eye(CONV_GROUPS), np.full((group, group), 1.0 / group)), dtype=F32).astype(BF16)

    w_in_b = _permute_inproj(w_in, qk_w).astype(BF16)
    wf_bd = _block_diag(w_fourier).astype(BF16)
    w_pw_b = w_conv_out.astype(BF16)
    w_out_b = w1_b = w3_b = w2_b = None

    def per_layer_rows(a):
        return a.reshape(depth, 1, a.shape[-1])

    norm1, norm2, subln = per_layer_rows(norm1_g), per_layer_rows(norm2_g), per_layer_rows(subln_g)
    lam_vecs = [per_layer_rows(a) for a in (lam_q1, lam_k1, lam_q2, lam_k2)]
    cb, lg, lb = per_layer_rows(conv_b), per_layer_rows(conv_ln_g), per_layer_rows(conv_ln_b)
    fg = final_g[None, :]

    def mixers(u_f, u_c, tables, l):
        yf = _fourier(u_f, chan_c, chan_s, wf_bd, l, tables)
        yc = _conv(u_c, conv_w, cb, lg, lb, avg, w_pw_b, l)
        return yf, yc

    ctx_flat = ctx.reshape(1, b * n_ctx, d)
    for l in range(depth):
        last = l == depth - 1
        lam_init = 0.8 - 0.6 * math.exp(-0.3 * l)
        q, k, vt, kn, uf, uc = _inproj(x, mod, norm1, w_in_b, rope, l, None, fw, 2 * cw)
        yf, yc = mixers(uf, uc, lat_tables, l)
        cast = (w_out, w_ffn1, w_ffn3, w_ffn2) if l == 0 else ()
        if last:
            kv_ctx = _inproj(ctx_flat, mod, norm1, w_in_b, None, l, ctx_row, fw, 2 * cw, kv_only=True)
            (o,), casted = _attention(kv_ctx, (q, k, vt, kn), lam_vecs, subln, l, lam_init, False, cast)
        else:
            qc, kc, vtc, knc, ufc, ucc = _inproj(ctx_flat, mod, norm1, w_in_b, None, l, ctx_row, fw, 2 * cw)
            (oc, o), casted = _attention((qc, kc, vtc, knc), (q, k, vt, kn), lam_vecs, subln, l, lam_init, True, cast)
        if cast:
            w_out_b, w1_b, w3_b, w2_b = casted
        if not last:
            yfc, ycc = mixers(ufc.reshape(b, n_ctx, fw), ucc.reshape(b, n_ctx, 2 * cw), ctx_tables, l)
            ctx_flat = _out_ffn(ctx_flat, oc.reshape(1, b * n_ctx, attn_w), yfc.reshape(1, b * n_ctx, fw),
                                ycc.reshape(1, b * n_ctx, cw), mod, norm2, w_out_b, w1_b, w3_b, w2_b, fg, l,
                                ctx_row, final=False)
        x = _out_ffn(x, o, yf, yc, mod, norm2, w_out_b, w1_b, w3_b, w2_b, fg, l, None, final=last)
    return x
```

```python
import functools
import math

import numpy as np
import jax
import jax.numpy as jnp
from jax import lax
from jax.experimental import pallas as pl
from jax.experimental.pallas import tpu as pltpu

F32 = jnp.float32
BF16 = jnp.bfloat16

GRID_W = 64
HEADS = 4
QK_DIM = 64
V_DIM = 2 * QK_DIM
FOURIER_GROUPS = 4
CONV_GROUPS = 4
CONV_K = 31
ROPE_BASE = 10000.0
EPS = 1e-6
LOG2_E = math.log2(math.e)
BOUND_SLACK = 1.0 + 2.0 ** -6
MIN_SOFTMAX_DENOMINATOR = 2.0 ** -90

LANES = 128
SUBLANES = 8
TM = 256
ROW_SUBTILES = 2
ATT_KC = 256
CONV_HALO = 16
CONV_CHUNK = 128
ADA_TN = 1024
DFT_LO = 64
V7X_VMEM_BYTES = 64 * 1024 * 1024


def _nbytes(shape, dtype):
    return math.prod(shape) * jnp.dtype(dtype).itemsize


def _block_bytes(specs, arrays):
    return sum(_nbytes([s for s in spec.block_shape if s is not None], arr.dtype) for spec, arr in zip(specs, arrays))


def _params(sem, *, resident=0, streamed=0, scratch=0, temporaries=0):
    need = resident + 2 * streamed + scratch + temporaries
    assert need <= V7X_VMEM_BYTES, f"VMEM estimate {need} exceeds the core's {V7X_VMEM_BYTES}"
    return pltpu.CompilerParams(dimension_semantics=sem, vmem_limit_bytes=need)


def _resident(arr, layer=None):
    if layer is None:
        idx = (0,) * arr.ndim
        return pl.BlockSpec(arr.shape, lambda *_: idx, pipeline_mode=pl.Buffered(1))
    idx = (layer,) + (0,) * (arr.ndim - 1)
    return pl.BlockSpec((None,) + arr.shape[1:], lambda *_: idx, pipeline_mode=pl.Buffered(1))


def _mod_spec(mod, layer, fixed_row):
    if fixed_row is None:
        return pl.BlockSpec((None, None) + mod.shape[2:], lambda i, j: (layer, i, 0, 0))
    return pl.BlockSpec((None, None) + mod.shape[2:], lambda i, j: (layer, fixed_row, 0, 0))


def _cast_slabs(weights, grid):
    steps = grid[0] * grid[1]
    operands, specs, shapes = [], [], []
    for wt in weights:
        flat = wt.reshape(-1, wt.shape[-1])
        slab = flat.shape[0] // steps
        assert slab * steps == flat.shape[0] and slab % (2 * SUBLANES) == 0
        operands.append(flat)
        specs.append(pl.BlockSpec((slab, flat.shape[1]), lambda i, j: (i * grid[1] + j, 0)))
        shapes.append(jax.ShapeDtypeStruct(flat.shape, BF16))
    return operands, specs, shapes


def _sigmoid(x):
    return 1.0 / (1.0 + jnp.exp(-x))


def _rms(x):
    return x * lax.rsqrt(jnp.mean(x * x, axis=-1, keepdims=True) + EPS)


def _dot(a, b):
    return jnp.dot(a, b, preferred_element_type=F32)


def _ada_kernel(c_ref, w_ref, b_ref, o_ref):
    c = c_ref[...]
    s = (c * _sigmoid(c)).astype(BF16)
    o_ref[...] = _dot(s, w_ref[...].astype(BF16)) + b_ref[...]


def _ada(c_rows, w_ada, b_ada):
    depth, d, n = w_ada.shape
    rows = c_rows.shape[0]
    return pl.pallas_call(
        _ada_kernel,
        grid=(depth, n // ADA_TN),
        in_specs=[
            pl.BlockSpec((rows, d), lambda l, j: (0, 0)),
            pl.BlockSpec((None, d, ADA_TN), lambda l, j: (l, 0, j)),
            pl.BlockSpec((None, 1, ADA_TN), lambda l, j: (l, 0, j)),
        ],
        out_specs=pl.BlockSpec((None, rows, ADA_TN), lambda l, j: (l, 0, j)),
        out_shape=jax.ShapeDtypeStruct((depth, rows, n), F32),
        compiler_params=_params(
            ("arbitrary", "arbitrary"),
            resident=_nbytes(c_rows.shape, F32),
            streamed=_nbytes((d + 1 + rows, ADA_TN), F32),
            temporaries=_nbytes((d, ADA_TN), BF16) + _nbytes((rows, ADA_TN), F32)),
        name="ada",
    )(c_rows, w_ada, b_ada.reshape(depth, 1, n))


def _inproj_kernel(*refs, use_rope, kv_only, n_sub):
    x_ref, mod_ref, g_ref, w_ref = refs[:4]
    refs = refs[4:]
    if use_rope:
        cos_ref, sin_ref = refs[:2]
        refs = refs[2:]
    hw = HEADS * LANES
    lane = lax.broadcasted_iota(jnp.int32, (TM, LANES), 1)
    first_half = (lane & (QK_DIM // 4)) == 0
    row = lax.broadcasted_iota(jnp.int32, (LANES, TM), 0)
    zero = jnp.zeros((LANES, TM), F32)
    dim = lax.broadcasted_iota(jnp.int32, (LANES, LANES), 0)
    col = lax.broadcasted_iota(jnp.int32, (LANES, LANES), 1)
    map_sum = jnp.where(col == jnp.where(dim < QK_DIM, 0, 1), 1.0, 0.0).astype(BF16)

    for u in range(n_sub):
        rows = slice(u * TM, (u + 1) * TM)
        h = (_rms(x_ref[rows, :]) * g_ref[...] * (1.0 + mod_ref[1:2, :]) + mod_ref[0:1, :]).astype(BF16)

        def rope(t):
            if not use_rope:
                return t
            partner = jnp.where(first_half, pltpu.roll(t, LANES - QK_DIM // 4, 1), pltpu.roll(t, QK_DIM // 4, 1))
            return t * cos_ref[rows, :] + partner * sin_ref[rows, :]

        def put_keys(j, kf):
            k_ref[j, rows, :] = kf.astype(BF16)
            norms = _dot((kf * kf).astype(BF16), map_sum)
            kn_ref[j, u] = jnp.max(norms.reshape(TM // SUBLANES, SUBLANES, LANES), axis=0)

        if kv_only:
            k_ref, vt_ref, kn_ref = refs
            r = _dot(h, w_ref[:, hw:3 * hw])
            for j in range(HEADS):
                put_keys(j, rope(r[:, j * LANES:(j + 1) * LANES]))
                vt_ref[j, :, rows] = r[:, hw + j * LANES:hw + (j + 1) * LANES].T.astype(BF16)
            continue

        q_ref, k_ref, vt_ref, kn_ref, uf_ref, uc_ref = refs
        r = _dot(h, w_ref[...])
        for j in range(HEADS):
            qt = (rope(r[:, j * LANES:(j + 1) * LANES]) * (QK_DIM ** -0.5 * LOG2_E)).T
            for mp in range(2):
                sel = (row < QK_DIM) == (mp == 0)
                qm = qt[mp * QK_DIM:(mp + 1) * QK_DIM, :]
                qn = jnp.sqrt(jnp.sum(qm * qm, axis=0, keepdims=True)) * BOUND_SLACK
                q_ref[j, u, mp, 0:LANES, :] = jnp.where(sel, qt, zero).astype(BF16)
                q_ref[j, u, mp, LANES:2 * LANES, :] = jnp.where(row == 0, qn, zero).astype(BF16)
            put_keys(j, rope(r[:, hw + j * LANES:hw + (j + 1) * LANES]))
            vt_ref[j, :, rows] = r[:, 2 * hw + j * LANES:2 * hw + (j + 1) * LANES].T.astype(BF16)
        fw = uf_ref.shape[-1]
        uf_ref[rows, :] = r[:, 3 * hw:3 * hw + fw].astype(BF16)
        uc_ref[rows, :] = r[:, 3 * hw + fw:]


def _inproj(x, mod, g, w, rope_tables, layer, fixed_row, fw, cw2, kv_only=False):
    b, n, d = x.shape
    n_sub = ROW_SUBTILES
    bm = n_sub * TM
    assert n % bm == 0
    use_rope = rope_tables is not None
    in_specs = [
        pl.BlockSpec((None, bm, d), lambda i, j: (i, j, 0)),
        _mod_spec(mod, layer, fixed_row),
        _resident(g, layer),
        _resident(w, layer),
    ]
    operands = [x, mod, g, w]
    if use_rope:
        in_specs += [pl.BlockSpec((bm, LANES), lambda i, j: (j, 0))] * 2
        operands += list(rope_tables)
    out_specs = [
        pl.BlockSpec((None, HEADS, n_sub, 2, 2 * LANES, TM), lambda i, j: (i, 0, j, 0, 0, 0)),
        pl.BlockSpec((None, HEADS, bm, LANES), lambda i, j: (i, 0, j, 0)),
        pl.BlockSpec((None, HEADS, V_DIM, bm), lambda i, j: (i, 0, 0, j)),
        pl.BlockSpec((None, HEADS, n_sub, SUBLANES, LANES), lambda i, j: (i, 0, j, 0, 0)),
        pl.BlockSpec((None, bm, fw), lambda i, j: (i, j, 0)),
        pl.BlockSpec((None, bm, cw2), lambda i, j: (i, j, 0)),
    ]
    out_shape = [
        jax.ShapeDtypeStruct((b, HEADS, n // TM, 2, 2 * LANES, TM), BF16),
        jax.ShapeDtypeStruct((b, HEADS, n, LANES), BF16),
        jax.ShapeDtypeStruct((b, HEADS, V_DIM, n), BF16),
        jax.ShapeDtypeStruct((b, HEADS, n // TM, SUBLANES, LANES), F32),
        jax.ShapeDtypeStruct((b, n, fw), BF16),
        jax.ShapeDtypeStruct((b, n, cw2), F32),
    ]
    if kv_only:
        out_specs, out_shape = out_specs[1:4], out_shape[1:4]
    return pl.pallas_call(
        functools.partial(_inproj_kernel, use_rope=use_rope, kv_only=kv_only, n_sub=n_sub),
        grid=(b, n // bm),
        in_specs=in_specs,
        out_specs=out_specs,
        out_shape=out_shape,
        compiler_params=_params(
            ("arbitrary", "arbitrary"),
            resident=_block_bytes(in_specs[2:4], operands[2:4]),
            streamed=_block_bytes(in_specs[:2] + in_specs[4:] + out_specs, operands[:2] + operands[4:] + out_shape),
            temporaries=2 * n_sub * _nbytes((TM, w.shape[-1]), F32)),
        name="inproj",
    )(*operands)


def _attn_kernel(*refs, lam_init, with_ctx, bounded, n_cast):
    lq1_ref, lk1_ref, lq2_ref, lk2_ref, g_ref, kc_ref, vtc_ref, knc_ref, ql_ref, kl_ref, vtl_ref, knl_ref = refs[:12]
    refs = list(refs[12:])
    s_ref = refs.pop()
    qc_ref = refs.pop(0) if with_ctx else None
    cast_src = [refs.pop(0) for _ in range(n_cast)]
    oc_ref = refs.pop(0) if with_ctx else None
    ol_ref = refs.pop(0)
    lmin_ref = refs.pop(0) if bounded else None
    for src_ref, dst_ref in zip(cast_src, refs):
        dst_ref[...] = src_ref[...].astype(BF16)
    lam = (jnp.exp(jnp.sum(lq1_ref[...] * lk1_ref[...], axis=-1, keepdims=True))
           - jnp.exp(jnp.sum(lq2_ref[...] * lk2_ref[...], axis=-1, keepdims=True)) + lam_init)

    def fold(x):
        return x.reshape(x.shape[0] // SUBLANES, SUBLANES, x.shape[1])

    def finish(o_ref, t, ot, l1):
        ot = ot * (1.0 / l1)
        ms = jnp.mean(ot * ot, axis=0, keepdims=True)
        on = (ot * lax.rsqrt(ms + EPS)).T * g_ref[...] * (1.0 - lam_init)
        o_ref[t * TM:(t + 1) * TM, :] = on.astype(BF16)

    def key_chunks(key_refs):
        return [(k_ref, vt_ref, c) for k_ref, vt_ref, _ in key_refs for c in range(0, k_ref.shape[0], ATT_KC)]

    def pipeline_bounded(q_ref, o_ref, key_refs):
        chunks = key_chunks(key_refs)
        n_tiles = q_ref.shape[0]
        lane = lax.broadcasted_iota(jnp.int32, (ATT_KC, LANES), 1)
        key_cols = []
        ksq = functools.reduce(jnp.maximum, [jnp.max(kn_ref[...], axis=0) for _, _, kn_ref in key_refs])
        kb = jnp.sqrt(jnp.max(ksq, axis=0, keepdims=True)) * BOUND_SLACK
        for mp in range(2):
            key_cols.append(jnp.where(lane == 0, -kb[:, mp:mp + 1], 0.0).astype(BF16))
        sums = {}
        lmin = None
        for u in range(n_tiles + 1):
            ta, tc = u, u - 1
            do_a, do_c = ta < n_tiles, 0 <= tc < n_tiles
            if do_a:
                qts = [q_ref[ta, mp] for mp in range(2)]
                l8 = [jnp.zeros((SUBLANES, TM), F32) for _ in range(2)]
            if do_c:
                lc = sums.pop(tc)
                rho = lam * lc[0] / lc[1]
                ot = None
            for ci, (k_ref, vt_ref, c) in enumerate(chunks):
                rows = slice(ci * ATT_KC, (ci + 1) * ATT_KC)
                if do_a:
                    for mp in range(2):
                        keys = jnp.concatenate([k_ref[c:c + ATT_KC, :], key_cols[mp]], axis=1)
                        e = jnp.exp2(_dot(keys, qts[mp]))
                        s_ref[ta % 2, mp, rows, :] = e
                        l8[mp] = l8[mp] + jnp.sum(fold(e), axis=0)
                if do_c:
                    p = (s_ref[tc % 2, 0, rows, :] - rho * s_ref[tc % 2, 1, rows, :]).astype(BF16)
                    part = _dot(vt_ref[:, c:c + ATT_KC], p)
                    ot = part if ot is None else ot + part
            if do_a:
                sums[ta] = [jnp.sum(l, axis=0, keepdims=True) for l in l8]
                low = jnp.minimum(sums[ta][0], sums[ta][1])
                lmin = low if lmin is None else jnp.minimum(lmin, low)
            if do_c:
                finish(o_ref, tc, ot, lc[0])
        return jnp.min(lmin, axis=1, keepdims=True)

    def pipeline(q_ref, o_ref, key_refs):
        chunks = key_chunks(key_refs)
        n_tiles = q_ref.shape[0]
        maxima, sums = {}, {}
        for u in range(n_tiles + 2):
            ta, tb, tc = u, u - 1, u - 2
            do_a, do_b, do_c = ta < n_tiles, 0 <= tb < n_tiles, 0 <= tc < n_tiles
            if do_a:
                qts = [q_ref[ta, mp, 0:LANES, :] for mp in range(2)]
                m8 = [None, None]
            if do_b:
                mb = maxima.pop(tb)
                l8 = [jnp.zeros((SUBLANES, TM), F32) for _ in range(2)]
            if do_c:
                lc = sums.pop(tc)
                rho = lam * lc[0] / lc[1]
                ot = None
            for ci, (k_ref, vt_ref, c) in enumerate(chunks):
                rows = slice(ci * ATT_KC, (ci + 1) * ATT_KC)
                if do_a:
                    for mp in range(2):
                        s = _dot(k_ref[c:c + ATT_KC, :], qts[mp])
                        s_ref[ta % 3, mp, rows, :] = s
                        cm = jnp.max(fold(s), axis=0)
                        m8[mp] = cm if m8[mp] is None else jnp.maximum(m8[mp], cm)
                if do_b:
                    for mp in range(2):
                        e = jnp.exp2(s_ref[tb % 3, mp, rows, :] - mb[mp])
                        s_ref[tb % 3, mp, rows, :] = e
                        l8[mp] = l8[mp] + jnp.sum(fold(e), axis=0)
                if do_c:
                    p = (s_ref[tc % 3, 0, rows, :] - rho * s_ref[tc % 3, 1, rows, :]).astype(BF16)
                    part = _dot(vt_ref[:, c:c + ATT_KC], p)
                    ot = part if ot is None else ot + part
            if do_a:
                maxima[ta] = [jnp.max(m, axis=0, keepdims=True) for m in m8]
            if do_b:
                sums[tb] = [jnp.sum(l, axis=0, keepdims=True) for l in l8]
            if do_c:
                finish(o_ref, tc, ot, lc[0])

    run = pipeline_bounded if bounded else pipeline
    ctx_keys, lat_keys = (kc_ref, vtc_ref, knc_ref), (kl_ref, vtl_ref, knl_ref)
    lmin = run(ql_ref, ol_ref, [ctx_keys, lat_keys])
    if with_ctx:
        lmin_ctx = run(qc_ref, oc_ref, [ctx_keys])
    if bounded:
        if with_ctx:
            lmin = jnp.minimum(lmin, lmin_ctx)
        lmin_ref[...] = jnp.broadcast_to(lmin, lmin_ref.shape)


def _attention_call(qkv_ctx, qkv_lat, lam_vecs, g, layer, lam_init, with_ctx, bounded, cast=()):
    q_lat, k_lat, vt_lat, kn_lat = qkv_lat
    k_ctx, vt_ctx, kn_ctx = qkv_ctx[-3:]
    b, h, n_lat, _ = k_lat.shape
    n_ctx = k_ctx.shape[2] // b
    lat_tiles, ctx_tiles = n_lat // TM, n_ctx // TM

    def tiles_spec(tiles, tail, ctx):
        zeros = (0,) * len(tail)
        if ctx:
            return pl.BlockSpec((None, None, tiles) + tail, lambda i, j: (0, j, i) + zeros)
        return pl.BlockSpec((None, None, tiles) + tail, lambda i, j: (i, j, 0) + zeros)

    q_tile, kn_tile = (2, 2 * LANES, TM), (SUBLANES, LANES)
    in_specs = [_resident(v, layer) for v in lam_vecs] + [
        _resident(g, layer),
        pl.BlockSpec((None, None, n_ctx, LANES), lambda i, j: (0, j, i, 0)),
        pl.BlockSpec((None, None, V_DIM, n_ctx), lambda i, j: (0, j, 0, i)),
        tiles_spec(ctx_tiles, kn_tile, True),
        tiles_spec(lat_tiles, q_tile, False),
        pl.BlockSpec((None, None, n_lat, LANES), lambda i, j: (i, j, 0, 0)),
        pl.BlockSpec((None, None, V_DIM, n_lat), lambda i, j: (i, j, 0, 0)),
        tiles_spec(lat_tiles, kn_tile, False),
    ]
    operands = list(lam_vecs) + [g, k_ctx, vt_ctx, kn_ctx, q_lat, k_lat, vt_lat, kn_lat]
    out_specs = [pl.BlockSpec((None, n_lat, V_DIM), lambda i, j: (i, 0, j))]
    out_shape = [jax.ShapeDtypeStruct((b, n_lat, h * V_DIM), BF16)]
    if with_ctx:
        in_specs.append(tiles_spec(ctx_tiles, q_tile, True))
        operands.append(qkv_ctx[0])
        out_specs.insert(0, pl.BlockSpec((None, n_ctx, V_DIM), lambda i, j: (i, 0, j)))
        out_shape.insert(0, jax.ShapeDtypeStruct((b, n_ctx, h * V_DIM), BF16))
    if bounded:
        out_specs.append(pl.BlockSpec((None, None, SUBLANES, LANES), lambda i, j: (i, j, 0, 0)))
        out_shape.append(jax.ShapeDtypeStruct((b, h, SUBLANES, LANES), F32))
    cast_operands, cast_specs, cast_shapes = _cast_slabs(cast, (b, h))
    in_specs, operands = in_specs + cast_specs, operands + cast_operands
    scores_shape = (2 if bounded else 3, 2, n_ctx + n_lat, TM)
    outs = pl.pallas_call(
        functools.partial(_attn_kernel, lam_init=lam_init, with_ctx=with_ctx, bounded=bounded, n_cast=len(cast)),
        grid=(b, h),
        in_specs=in_specs,
        out_specs=out_specs + cast_specs,
        out_shape=out_shape + cast_shapes,
        scratch_shapes=[pltpu.VMEM(scores_shape, F32)],
        compiler_params=_params(
            ("arbitrary", "arbitrary"),
            streamed=_block_bytes(in_specs + out_specs + cast_specs, operands + out_shape + cast_shapes),
            scratch=_nbytes(scores_shape, F32),
            temporaries=8 * _nbytes((ATT_KC, TM), F32)),
        name="attention_bounded" if bounded else "attention_exact",
    )(*operands)
    n_main = len(out_shape)
    return list(outs[:n_main]), [o.reshape(wt.shape) for o, wt in zip(outs[n_main:], cast)]


def _attention(qkv_ctx, qkv_lat, lam_vecs, g, layer, lam_init, with_ctx, cast=()):
    args = (qkv_ctx, qkv_lat, lam_vecs, g, layer, lam_init, with_ctx)
    (*outs, lmin), casted = _attention_call(*args, bounded=True, cast=cast)
    safe = jnp.min(lmin) >= MIN_SOFTMAX_DENOMINATOR
    outs = lax.cond(safe, lambda: tuple(outs), lambda: tuple(_attention_call(*args, bounded=False)[0]))
    return list(outs), casted


def _fourier_kernel(u_ref, chan_c_ref, chan_s_ref, wf_ref, pos_c_ref, pos_s_ref, mid_ref, flip_ref, y_ref):
    n, width = u_ref.shape
    half = n // 2
    scale = (n * width // FOURIER_GROUPS) ** -0.5
    u = u_ref[...]
    a = _dot(pos_c_ref[...], u).astype(BF16)
    b = _dot(pos_s_ref[...], u).astype(BF16)
    p = _dot(a, chan_c_ref[...])
    q = _dot(b, chan_s_ref[...])
    y_ref[0:half, :] = _dot(((p - q) * scale).astype(BF16), wf_ref[...]).astype(BF16)
    mirrored = _dot(flip_ref[...], ((p + q) * scale).astype(BF16))
    mid = _dot(_dot(mid_ref[...], u).astype(BF16), chan_c_ref[...]) * scale
    rows = mid.shape[0]
    upper = jnp.concatenate([mirrored[0:rows] + mid, mirrored[rows:]], axis=0)
    y_ref[half:, :] = _dot(upper.astype(BF16), wf_ref[...]).astype(BF16)


def _fourier(uf, chan_c, chan_s, wf_bd, layer, pos_tables):
    b, n, fw = uf.shape
    half = n // 2
    idx = jnp.arange(half)
    flip = (idx[:, None] + idx[None, :] == half).astype(BF16)
    mid = np.zeros((SUBLANES, n), np.float32)
    mid[0] = 1.0 - 2.0 * (np.arange(n) % 2)
    mid = jnp.asarray(mid).astype(BF16)
    row_spec = pl.BlockSpec((None, n, fw), lambda i: (i, 0, 0))
    operands = [uf, chan_c, chan_s, wf_bd, *pos_tables, mid, flip]
    in_specs = [row_spec, _resident(chan_c), _resident(chan_s), _resident(wf_bd, layer)] + [
        _resident(t) for t in (*pos_tables, mid, flip)]
    return pl.pallas_call(
        _fourier_kernel,
        grid=(b,),
        in_specs=in_specs,
        out_specs=row_spec,
        out_shape=jax.ShapeDtypeStruct((b, n, fw), BF16),
        compiler_params=_params(
            ("arbitrary",),
            resident=_block_bytes(in_specs[1:], operands[1:]),
            streamed=2 * _nbytes((n, fw), BF16),
            temporaries=8 * _nbytes((half, fw), F32)),
        name="fourier",
    )(*operands)


def _split_dot(x, m):
    hi = x.astype(BF16)
    lo = (x - hi.astype(F32)).astype(BF16)
    return _dot(hi, m) + _dot(lo, m)


def _conv_kernel(u_ref, w_ref, b_ref, lg_ref, lb_ref, avg_ref, pw_ref, y_ref, zpad_ref, acc_a_ref, acc_b_ref):
    n, cw = y_ref.shape
    halo = jnp.zeros((CONV_HALO, cw), F32)

    def taps(i, acc_ref, slot):
        r0 = pl.multiple_of(i * CONV_CHUNK, CONV_CHUNK)
        for lo in range(0, cw, LANES):
            win = zpad_ref[pl.ds(r0, CONV_CHUNK + 2 * CONV_HALO), lo:lo + LANES]
            part = jnp.zeros((CONV_CHUNK, LANES), F32) + b_ref[:, lo:lo + LANES]
            for shift in range(SUBLANES):
                rolled = win if shift == 0 else pltpu.roll(win, win.shape[0] - shift, 0)
                for aligned in range(0, 2 * CONV_HALO, SUBLANES):
                    tap = aligned + shift - (CONV_HALO - CONV_K // 2)
                    if 0 <= tap < CONV_K:
                        part = part + rolled[aligned:aligned + CONV_CHUNK] * w_ref[tap:tap + 1, lo:lo + LANES]
            acc_ref[slot, :, lo:lo + LANES] = part

    def project(acc_ref, slot):
        acc = acc_ref[slot]
        mu = _split_dot(acc, avg_ref[...])
        dev = acc - mu
        var = _split_dot(dev * dev, avg_ref[...])
        zn = dev * lax.rsqrt(var + EPS) * lg_ref[...] + lb_ref[...]
        act = (zn * _sigmoid(zn)).astype(BF16)
        return _dot(act, pw_ref[...]).astype(BF16)

    def round_(first_tap, tap_ref, ready_ref):
        if first_tap is not None:
            taps(first_tap, tap_ref, 0)
            taps(first_tap + 1, tap_ref, 1)
        return [project(ready_ref, 0), project(ready_ref, 1)]

    def store(first_chunk, ys):
        r0 = pl.multiple_of(first_chunk * CONV_CHUNK, 2 * CONV_CHUNK)
        y_ref[pl.ds(r0, len(ys) * CONV_CHUNK), :] = jnp.concatenate(ys, axis=0)

    zpad_ref[0:CONV_HALO, :] = halo
    zpad_ref[CONV_HALO:CONV_HALO + n, :] = u_ref[:, 0:cw] * _sigmoid(u_ref[:, cw:2 * cw])
    zpad_ref[CONV_HALO + n:2 * CONV_HALO + n, :] = halo

    chunks = n // CONV_CHUNK
    assert chunks == 2 or chunks % 4 == 0

    def step(j, carry):
        c0 = 4 * j
        ys = round_(c0 + 2, acc_b_ref, acc_a_ref) + round_(c0 + 4, acc_a_ref, acc_b_ref)
        store(c0, ys)
        return carry

    taps(0, acc_a_ref, 0)
    taps(1, acc_a_ref, 1)
    if chunks == 2:
        store(0, round_(None, None, acc_a_ref))
    else:
        lax.fori_loop(0, chunks // 4 - 1, step, 0)
        store(chunks - 4, round_(chunks - 2, acc_b_ref, acc_a_ref) + round_(None, None, acc_b_ref))


def _conv(uc, conv_w, conv_b, ln_g, ln_b, avg, w_pw, layer):
    b, n, cw2 = uc.shape
    cw = cw2 // 2
    operands = [uc, conv_w, conv_b, ln_g, ln_b, avg, w_pw]
    in_specs = [
        pl.BlockSpec((None, n, cw2), lambda i: (i, 0, 0)),
        _resident(conv_w, layer), _resident(conv_b, layer), _resident(ln_g, layer), _resident(ln_b, layer),
        _resident(avg), _resident(w_pw, layer),
    ]
    scratch = [(n + 2 * CONV_HALO, cw), (2, CONV_CHUNK, cw), (2, CONV_CHUNK, cw)]
    return pl.pallas_call(
        _conv_kernel,
        grid=(b,),
        in_specs=in_specs,
        out_specs=pl.BlockSpec((None, n, cw), lambda i: (i, 0, 0)),
        out_shape=jax.ShapeDtypeStruct((b, n, cw), BF16),
        scratch_shapes=[pltpu.VMEM(s, F32) for s in scratch],
        compiler_params=_params(
            ("arbitrary",),
            resident=_block_bytes(in_specs[1:], operands[1:]),
            streamed=_nbytes((n, cw2), F32) + _nbytes((n, cw), BF16),
            scratch=sum(_nbytes(s, F32) for s in scratch),
            temporaries=3 * _nbytes((n, cw), F32) + 16 * _nbytes((CONV_CHUNK, cw), F32)),
        name="conv",
    )(*operands)


def _ffn_kernel(x_ref, o_ref, yf_ref, yc_ref, mod_ref, g_ref, wout_ref, w1_ref, w3_ref, w2_ref, fg_ref, out_ref,
                *, final, n_sub):
    aw = o_ref.shape[-1]
    fw = yf_ref.shape[-1]
    for u in range(n_sub):
        rows = slice(u * TM, (u + 1) * TM)
        y = (_dot(o_ref[rows, :], wout_ref[0:aw, :]) + _dot(yf_ref[rows, :], wout_ref[aw:aw + fw, :])
             + _dot(yc_ref[rows, :], wout_ref[aw + fw:, :]))
        x1 = x_ref[rows, :] + mod_ref[2:3, :] * y
        h = (_rms(x1) * g_ref[...] * (1.0 + mod_ref[4:5, :]) + mod_ref[3:4, :]).astype(BF16)
        a = _dot(h, w1_ref[...])
        gated = (a * _sigmoid(a) * _dot(h, w3_ref[...])).astype(BF16)
        x2 = x1 + mod_ref[5:6, :] * _dot(gated, w2_ref[...])
        if final:
            x2 = _rms(x2) * fg_ref[...]
        out_ref[rows, :] = x2


def _out_ffn(x, o, yf, yc, mod, g2, w_out, w1, w3, w2, final_g, layer, fixed_row, final):
    b, n, d = x.shape
    n_sub = ROW_SUBTILES
    bm = n_sub * TM
    assert n % bm == 0

    def rows(width):
        return pl.BlockSpec((None, bm, width), lambda i, j: (i, j, 0))

    operands = [x, o, yf, yc, mod, g2, w_out, w1, w3, w2, final_g]
    in_specs = [
        rows(d), rows(o.shape[-1]), rows(yf.shape[-1]), rows(yc.shape[-1]),
        _mod_spec(mod, layer, fixed_row),
        _resident(g2, layer), _resident(w_out, layer), _resident(w1, layer), _resident(w3, layer),
        _resident(w2, layer), _resident(final_g),
    ]
    d_ff = w1.shape[-1]
    return pl.pallas_call(
        functools.partial(_ffn_kernel, final=final, n_sub=n_sub),
        grid=(b, n // bm),
        in_specs=in_specs,
        out_specs=rows(d),
        out_shape=jax.ShapeDtypeStruct((b, n, d), F32),
        compiler_params=_params(
            ("arbitrary", "arbitrary"),
            resident=_block_bytes(in_specs[5:], operands[5:]),
            streamed=_block_bytes(in_specs[:5], operands[:5]) + _nbytes((bm, d), F32),
            temporaries=n_sub * (4 * _nbytes((TM, d), F32) + 2 * _nbytes((TM, d_ff), F32) + _nbytes((TM, d_ff), BF16))),
        name="out_ffn",
    )(*operands)


def _rope_tables(n_lat):
    n_freq = QK_DIM // 4
    tok = np.arange(n_lat)
    inv_freq = np.float32(ROPE_BASE) ** (-np.arange(n_freq, dtype=np.float32) / np.float32(n_freq))
    ang_r = ((tok // GRID_W).astype(np.float32)[:, None] * inv_freq).astype(np.float64)
    ang_c = ((tok % GRID_W).astype(np.float32)[:, None] * inv_freq).astype(np.float64)
    cos = np.concatenate([np.cos(ang_r)] * 2 + [np.cos(ang_c)] * 2, axis=-1)
    sin = np.concatenate([-np.sin(ang_r), np.sin(ang_r), -np.sin(ang_c), np.sin(ang_c)], axis=-1)
    reps = LANES // QK_DIM
    return (jnp.asarray(np.tile(cos, (1, reps)), dtype=F32), jnp.asarray(np.tile(sin, (1, reps)), dtype=F32))


def _dft_angles(rows, n):
    return 2.0 * np.pi * ((np.asarray(rows, np.int64)[:, None] * np.arange(n, dtype=np.int64)[None, :]) % n) / n


def _dft_tables_small(n):
    ang = _dft_angles(np.arange(n), n)
    return np.cos(ang).astype(np.float32), np.sin(ang).astype(np.float32)


def _dft_tables(n):
    rows = n // 2
    if n <= 4 * DFT_LO:
        return [jnp.asarray(t[:rows]).astype(BF16) for t in _dft_tables_small(n)]
    hi = rows // DFT_LO
    ang_hi = _dft_angles(np.arange(hi) * DFT_LO, n)
    ang_lo = _dft_angles(np.arange(DFT_LO), n)
    ch, sh = (jnp.asarray(f(ang_hi), dtype=F32)[:, None, :] for f in (np.cos, np.sin))
    cl, sl = (jnp.asarray(f(ang_lo), dtype=F32)[None, :, :] for f in (np.cos, np.sin))
    return [(ch * cl - sh * sl).reshape(rows, n).astype(BF16), (sh * cl + ch * sl).reshape(rows, n).astype(BF16)]


def _block_diag(blocks):
    g, r, c = blocks.shape[-3:]
    eye = jnp.eye(g, dtype=blocks.dtype)
    out = eye[:, None, :, None] * blocks[..., :, :, None, :]
    return out.reshape(blocks.shape[:-3] + (g * r, g * c))


def _permute_inproj(w_in, qk_w):
    lead = w_in.shape[:-1]
    qk = w_in[..., :4 * qk_w].reshape(lead + (2, 2, HEADS, QK_DIM))
    qk = jnp.swapaxes(qk, -3, -2).reshape(lead + (4 * qk_w,))
    return jnp.concatenate([qk, w_in[..., 4 * qk_w:]], axis=-1)


def kernel(x, c, ctx, c_ctx, w_ada, b_ada, norm1_g, norm2_g, w_in, lam_q1, lam_k1, lam_q2, lam_k2, subln_g,
           w_fourier, conv_w, conv_b, conv_ln_g, conv_ln_b, w_conv_out, w_out, w_ffn1, w_ffn3, w_ffn2, final_g):
    b, n_lat, d = x.shape
    n_ctx = ctx.shape[1]
    depth = w_ada.shape[0]
    fw = w_fourier.shape[1] * w_fourier.shape[2]
    cw = conv_w.shape[-1]
    qk_w = HEADS * QK_DIM
    attn_w = HEADS * V_DIM
    assert n_lat % GRID_W == 0 and n_ctx % TM == 0
    assert w_in.shape[-1] == 4 * qk_w + attn_w + fw + 2 * cw

    pad = (-(b + 1)) % SUBLANES
    c_rows = jnp.concatenate([c, c_ctx[None, :], jnp.zeros((pad, d), c.dtype)], axis=0)
    mod = _ada(c_rows, w_ada, b_ada)
    mod = mod.reshape(depth, mod.shape[1], 6, d)
    ctx_row = b

    rope = _rope_tables(n_lat)
    cc, cs = _dft_tables_small(fw // FOURIER_GROUPS)
    eye = np.eye(FOURIER_GROUPS, dtype=np.float32)
    chan_c = jnp.asarray(np.kron(eye, cc)).astype(BF16)
    chan_s = jnp.asarray(np.kron(eye, cs)).astype(BF16)
    lat_tables = _dft_tables(n_lat)
    ctx_tables = _dft_tables(n_ctx)
    group = cw // CONV_GROUPS
    avg = jnp.asarray(np.kron(np.eye(CONV_GROUPS), np.full((group, group), 1.0 / group)), dtype=F32).astype(BF16)

    w_in_b = _permute_inproj(w_in, qk_w).astype(BF16)
    wf_bd = _block_diag(w_fourier).astype(BF16)
    w_pw_b = w_conv_out.astype(BF16)
    w_out_b = w1_b = w3_b = w2_b = None

    def per_layer_rows(a):
        return a.reshape(depth, 1, a.shape[-1])

    norm1, norm2, subln = per_layer_rows(norm1_g), per_layer_rows(norm2_g), per_layer_rows(subln_g)
    lam_vecs = [per_layer_rows(a) for a in (lam_q1, lam_k1, lam_q2, lam_k2)]
    cb, lg, lb = per_layer_rows(conv_b), per_layer_rows(conv_ln_g), per_layer_rows(conv_ln_b)
    fg = final_g[None, :]

    def mixers(u_f, u_c, tables, l):
        yf = _fourier(u_f, chan_c, chan_s, wf_bd, l, tables)
        yc = _conv(u_c, conv_w, cb, lg, lb, avg, w_pw_b, l)
        return yf, yc

    ctx_flat = ctx.reshape(1, b * n_ctx, d)
    for l in range(depth):
        last = l == depth - 1
        lam_init = 0.8 - 0.6 * math.exp(-0.3 * l)
        q, k, vt, kn, uf, uc = _inproj(x, mod, norm1, w_in_b, rope, l, None, fw, 2 * cw)
        yf, yc = mixers(uf, uc, lat_tables, l)
        cast = (w_out, w_ffn1, w_ffn3, w_ffn2) if l == 0 else ()
        if last:
            kv_ctx = _inproj(ctx_flat, mod, norm1, w_in_b, None, l, ctx_row, fw, 2 * cw, kv_only=True)
            (o,), casted = _attention(kv_ctx, (q, k, vt, kn), lam_vecs, subln, l, lam_init, False, cast)
        else:
            qc, kc, vtc, knc, ufc, ucc = _inproj(ctx_flat, mod, norm1, w_in_b, None, l, ctx_row, fw, 2 * cw)
            (oc, o), casted = _attention((qc, kc, vtc, knc), (q, k, vt, kn), lam_vecs, subln, l, lam_init, True, cast)
        if cast:
            w_out_b, w1_b, w3_b, w2_b = casted
        if not last:
            yfc, ycc = mixers(ufc.reshape(b, n_ctx, fw), ucc.reshape(b, n_ctx, 2 * cw), ctx_tables, l)
            ctx_flat = _out_ffn(ctx_flat, oc.reshape(1, b * n_ctx, attn_w), yfc.reshape(1, b * n_ctx, fw),
                                ycc.reshape(1, b * n_ctx, cw), mod, norm2, w_out_b, w1_b, w3_b, w2_b, fg, l,
                                ctx_row, final=False)
        x = _out_ffn(x, o, yf, yc, mod, norm2, w_out_b, w1_b, w3_b, w2_b, fg, l, None, final=last)
    return x
```

```python
import functools
import math

import numpy as np
import jax
import jax.numpy as jnp
from jax import lax
from jax.experimental import pallas as pl
from jax.experimental.pallas import tpu as pltpu

F32 = jnp.float32
BF16 = jnp.bfloat16

GRID_W = 64
HEADS = 4
QK_DIM = 64
V_DIM = 2 * QK_DIM
FOURIER_GROUPS = 4
CONV_GROUPS = 4
CONV_K = 31
ROPE_BASE = 10000.0
EPS = 1e-6
LOG2_E = math.log2(math.e)
BOUND_SLACK = 1.0 + 2.0 ** -6
MIN_SOFTMAX_DENOMINATOR = 2.0 ** -90

LANES = 128
SUBLANES = 8
TM = 256
ROW_SUBTILES = 2
ATT_KC = 256
CONV_HALO = 16
CONV_CHUNK = 128
ADA_TN = 1024
DFT_LO = 64
V7X_VMEM_BYTES = 64 * 1024 * 1024
VMEM_REQUEST_BYTES = V7X_VMEM_BYTES * 15 // 16


def _nbytes(shape, dtype):
    return math.prod(shape) * jnp.dtype(dtype).itemsize


def _block_bytes(specs, arrays):
    return sum(_nbytes([s for s in spec.block_shape if s is not None], arr.dtype) for spec, arr in zip(specs, arrays))


def _params(sem, *, resident=0, streamed=0, scratch=0, temporaries=0):
    need = resident + 2 * streamed + scratch + temporaries
    assert need <= VMEM_REQUEST_BYTES, f"VMEM estimate {need} exceeds the request {VMEM_REQUEST_BYTES}"
    return pltpu.CompilerParams(dimension_semantics=sem, vmem_limit_bytes=VMEM_REQUEST_BYTES)


def _resident(arr, layer=None):
    if layer is None:
        idx = (0,) * arr.ndim
        return pl.BlockSpec(arr.shape, lambda *_: idx, pipeline_mode=pl.Buffered(1))
    idx = (layer,) + (0,) * (arr.ndim - 1)
    return pl.BlockSpec((None,) + arr.shape[1:], lambda *_: idx, pipeline_mode=pl.Buffered(1))


def _mod_spec(mod, layer, fixed_row):
    if fixed_row is None:
        return pl.BlockSpec((None, None) + mod.shape[2:], lambda i, j: (layer, i, 0, 0))
    return pl.BlockSpec((None, None) + mod.shape[2:], lambda i, j: (layer, fixed_row, 0, 0))


def _cast_slabs(weights, grid):
    steps = grid[0] * grid[1]
    operands, specs, shapes = [], [], []
    for wt in weights:
        flat = wt.reshape(-1, wt.shape[-1])
        slab = flat.shape[0] // steps
        assert slab * steps == flat.shape[0] and slab % (2 * SUBLANES) == 0
        operands.append(flat)
        specs.append(pl.BlockSpec((slab, flat.shape[1]), lambda i, j: (i * grid[1] + j, 0)))
        shapes.append(jax.ShapeDtypeStruct(flat.shape, BF16))
    return operands, specs, shapes


def _sigmoid(x):
    return 1.0 / (1.0 + jnp.exp(-x))


def _rms(x):
    return x * lax.rsqrt(jnp.mean(x * x, axis=-1, keepdims=True) + EPS)


def _dot(a, b):
    return jnp.dot(a, b, preferred_element_type=F32)


def _ada_kernel(c_ref, w_ref, b_ref, o_ref):
    c = c_ref[...]
    s = (c * _sigmoid(c)).astype(BF16)
    o_ref[...] = _dot(s, w_ref[...].astype(BF16)) + b_ref[...]


def _ada(c_rows, w_ada, b_ada):
    depth, d, n = w_ada.shape
    rows = c_rows.shape[0]
    return pl.pallas_call(
        _ada_kernel,
        grid=(depth, n // ADA_TN),
        in_specs=[
            pl.BlockSpec((rows, d), lambda l, j: (0, 0)),
            pl.BlockSpec((None, d, ADA_TN), lambda l, j: (l, 0, j)),
            pl.BlockSpec((None, 1, ADA_TN), lambda l, j: (l, 0, j)),
        ],
        out_specs=pl.BlockSpec((None, rows, ADA_TN), lambda l, j: (l, 0, j)),
        out_shape=jax.ShapeDtypeStruct((depth, rows, n), F32),
        compiler_params=_params(
            ("arbitrary", "arbitrary"),
            resident=_nbytes(c_rows.shape, F32),
            streamed=_nbytes((d + 1 + rows, ADA_TN), F32),
            temporaries=_nbytes((d, ADA_TN), BF16) + _nbytes((rows, ADA_TN), F32)),
        name="ada",
    )(c_rows, w_ada, b_ada.reshape(depth, 1, n))


def _inproj_kernel(*refs, use_rope, kv_only, n_sub):
    x_ref, mod_ref, g_ref, w_ref = refs[:4]
    refs = refs[4:]
    if use_rope:
        cos_ref, sin_ref = refs[:2]
        refs = refs[2:]
    hw = HEADS * LANES
    lane = lax.broadcasted_iota(jnp.int32, (TM, LANES), 1)
    first_half = (lane & (QK_DIM // 4)) == 0
    row = lax.broadcasted_iota(jnp.int32, (LANES, TM), 0)
    zero = jnp.zeros((LANES, TM), F32)
    dim = lax.broadcasted_iota(jnp.int32, (LANES, LANES), 0)
    col = lax.broadcasted_iota(jnp.int32, (LANES, LANES), 1)
    map_sum = jnp.where(col == jnp.where(dim < QK_DIM, 0, 1), 1.0, 0.0).astype(BF16)

    for u in range(n_sub):
        rows = slice(u * TM, (u + 1) * TM)
        h = (_rms(x_ref[rows, :]) * g_ref[...] * (1.0 + mod_ref[1:2, :]) + mod_ref[0:1, :]).astype(BF16)

        def rope(t):
            if not use_rope:
                return t
            partner = jnp.where(first_half, pltpu.roll(t, LANES - QK_DIM // 4, 1), pltpu.roll(t, QK_DIM // 4, 1))
            return t * cos_ref[rows, :] + partner * sin_ref[rows, :]

        def put_keys(j, kf):
            k_ref[j, rows, :] = kf.astype(BF16)
            norms = _dot((kf * kf).astype(BF16), map_sum)
            kn_ref[j, u] = jnp.max(norms.reshape(TM // SUBLANES, SUBLANES, LANES), axis=0)

        if kv_only:
            k_ref, vt_ref, kn_ref = refs
            r = _dot(h, w_ref[:, hw:3 * hw])
            for j in range(HEADS):
                put_keys(j, rope(r[:, j * LANES:(j + 1) * LANES]))
                vt_ref[j, :, rows] = r[:, hw + j * LANES:hw + (j + 1) * LANES].T.astype(BF16)
            continue

        q_ref, k_ref, vt_ref, kn_ref, uf_ref, uc_ref = refs
        r = _dot(h, w_ref[...])
        for j in range(HEADS):
            qt = (rope(r[:, j * LANES:(j + 1) * LANES]) * (QK_DIM ** -0.5 * LOG2_E)).T
            for mp in range(2):
                sel = (row < QK_DIM) == (mp == 0)
                qm = qt[mp * QK_DIM:(mp + 1) * QK_DIM, :]
                qn = jnp.sqrt(jnp.sum(qm * qm, axis=0, keepdims=True)) * BOUND_SLACK
                q_ref[j, u, mp, 0:LANES, :] = jnp.where(sel, qt, zero).astype(BF16)
                q_ref[j, u, mp, LANES:2 * LANES, :] = jnp.where(row == 0, qn, zero).astype(BF16)
            put_keys(j, rope(r[:, hw + j * LANES:hw + (j + 1) * LANES]))
            vt_ref[j, :, rows] = r[:, 2 * hw + j * LANES:2 * hw + (j + 1) * LANES].T.astype(BF16)
        fw = uf_ref.shape[-1]
        uf_ref[rows, :] = r[:, 3 * hw:3 * hw + fw].astype(BF16)
        uc_ref[rows, :] = r[:, 3 * hw + fw:]


def _inproj(x, mod, g, w, rope_tables, layer, fixed_row, fw, cw2, kv_only=False):
    b, n, d = x.shape
    n_sub = ROW_SUBTILES
    bm = n_sub * TM
    assert n % bm == 0
    use_rope = rope_tables is not None
    in_specs = [
        pl.BlockSpec((None, bm, d), lambda i, j: (i, j, 0)),
        _mod_spec(mod, layer, fixed_row),
        _resident(g, layer),
        _resident(w, layer),
    ]
    operands = [x, mod, g, w]
    if use_rope:
        in_specs += [pl.BlockSpec((bm, LANES), lambda i, j: (j, 0))] * 2
        operands += list(rope_tables)
    out_specs = [
        pl.BlockSpec((None, HEADS, n_sub, 2, 2 * LANES, TM), lambda i, j: (i, 0, j, 0, 0, 0)),
        pl.BlockSpec((None, HEADS, bm, LANES), lambda i, j: (i, 0, j, 0)),
        pl.BlockSpec((None, HEADS, V_DIM, bm), lambda i, j: (i, 0, 0, j)),
        pl.BlockSpec((None, HEADS, n_sub, SUBLANES, LANES), lambda i, j: (i, 0, j, 0, 0)),
        pl.BlockSpec((None, bm, fw), lambda i, j: (i, j, 0)),
        pl.BlockSpec((None, bm, cw2), lambda i, j: (i, j, 0)),
    ]
    out_shape = [
        jax.ShapeDtypeStruct((b, HEADS, n // TM, 2, 2 * LANES, TM), BF16),
        jax.ShapeDtypeStruct((b, HEADS, n, LANES), BF16),
        jax.ShapeDtypeStruct((b, HEADS, V_DIM, n), BF16),
        jax.ShapeDtypeStruct((b, HEADS, n // TM, SUBLANES, LANES), F32),
        jax.ShapeDtypeStruct((b, n, fw), BF16),
        jax.ShapeDtypeStruct((b, n, cw2), F32),
    ]
    if kv_only:
        out_specs, out_shape = out_specs[1:4], out_shape[1:4]
    return pl.pallas_call(
        functools.partial(_inproj_kernel, use_rope=use_rope, kv_only=kv_only, n_sub=n_sub),
        grid=(b, n // bm),
        in_specs=in_specs,
        out_specs=out_specs,
        out_shape=out_shape,
        compiler_params=_params(
            ("arbitrary", "arbitrary"),
            resident=_block_bytes(in_specs[2:4], operands[2:4]),
            streamed=_block_bytes(in_specs[:2] + in_specs[4:] + out_specs, operands[:2] + operands[4:] + out_shape),
            temporaries=2 * n_sub * _nbytes((TM, w.shape[-1]), F32)),
        name="inproj",
    )(*operands)


def _attn_kernel(*refs, lam_init, with_ctx, bounded, n_cast):
    lq1_ref, lk1_ref, lq2_ref, lk2_ref, g_ref, kc_ref, vtc_ref, knc_ref, ql_ref, kl_ref, vtl_ref, knl_ref = refs[:12]
    refs = list(refs[12:])
    s_ref = refs.pop()
    qc_ref = refs.pop(0) if with_ctx else None
    cast_src = [refs.pop(0) for _ in range(n_cast)]
    oc_ref = refs.pop(0) if with_ctx else None
    ol_ref = refs.pop(0)
    lmin_ref = refs.pop(0) if bounded else None
    for src_ref, dst_ref in zip(cast_src, refs):
        dst_ref[...] = src_ref[...].astype(BF16)
    lam = (jnp.exp(jnp.sum(lq1_ref[...] * lk1_ref[...], axis=-1, keepdims=True))
           - jnp.exp(jnp.sum(lq2_ref[...] * lk2_ref[...], axis=-1, keepdims=True)) + lam_init)

    def fold(x):
        return x.reshape(x.shape[0] // SUBLANES, SUBLANES, x.shape[1])

    def finish(o_ref, t, ot, l1):
        ot = ot * (1.0 / l1)
        ms = jnp.mean(ot * ot, axis=0, keepdims=True)
        on = (ot * lax.rsqrt(ms + EPS)).T * g_ref[...] * (1.0 - lam_init)
        o_ref[t * TM:(t + 1) * TM, :] = on.astype(BF16)

    def key_chunks(key_refs):
        return [(k_ref, vt_ref, c) for k_ref, vt_ref, _ in key_refs for c in range(0, k_ref.shape[0], ATT_KC)]

    def pipeline_bounded(q_ref, o_ref, key_refs):
        chunks = key_chunks(key_refs)
        n_tiles = q_ref.shape[0]
        lane = lax.broadcasted_iota(jnp.int32, (ATT_KC, LANES), 1)
        key_cols = []
        ksq = functools.reduce(jnp.maximum, [jnp.max(kn_ref[...], axis=0) for _, _, kn_ref in key_refs])
        kb = jnp.sqrt(jnp.max(ksq, axis=0, keepdims=True)) * BOUND_SLACK
        for mp in range(2):
            key_cols.append(jnp.where(lane == 0, -kb[:, mp:mp + 1], 0.0).astype(BF16))
        sums = {}
        lmin = None
        for u in range(n_tiles + 1):
            ta, tc = u, u - 1
            do_a, do_c = ta < n_tiles, 0 <= tc < n_tiles
            if do_a:
                qts = [q_ref[ta, mp] for mp in range(2)]
                l8 = [jnp.zeros((SUBLANES, TM), F32) for _ in range(2)]
            if do_c:
                lc = sums.pop(tc)
                rho = lam * lc[0] / lc[1]
                ot = None
            for ci, (k_ref, vt_ref, c) in enumerate(chunks):
                rows = slice(ci * ATT_KC, (ci + 1) * ATT_KC)
                if do_a:
                    for mp in range(2):
                        keys = jnp.concatenate([k_ref[c:c + ATT_KC, :], key_cols[mp]], axis=1)
                        e = jnp.exp2(_dot(keys, qts[mp]))
                        s_ref[ta % 2, mp, rows, :] = e
                        l8[mp] = l8[mp] + jnp.sum(fold(e), axis=0)
                if do_c:
                    p = (s_ref[tc % 2, 0, rows, :] - rho * s_ref[tc % 2, 1, rows, :]).astype(BF16)
                    part = _dot(vt_ref[:, c:c + ATT_KC], p)
                    ot = part if ot is None else ot + part
            if do_a:
                sums[ta] = [jnp.sum(l, axis=0, keepdims=True) for l in l8]
                low = jnp.minimum(sums[ta][0], sums[ta][1])
                lmin = low if lmin is None else jnp.minimum(lmin, low)
            if do_c:
                finish(o_ref, tc, ot, lc[0])
        return jnp.min(lmin, axis=1, keepdims=True)

    def pipeline(q_ref, o_ref, key_refs):
        chunks = key_chunks(key_refs)
        n_tiles = q_ref.shape[0]
        maxima, sums = {}, {}
        for u in range(n_tiles + 2):
            ta, tb, tc = u, u - 1, u - 2
            do_a, do_b, do_c = ta < n_tiles, 0 <= tb < n_tiles, 0 <= tc < n_tiles
            if do_a:
                qts = [q_ref[ta, mp, 0:LANES, :] for mp in range(2)]
                m8 = [None, None]
            if do_b:
                mb = maxima.pop(tb)
                l8 = [jnp.zeros((SUBLANES, TM), F32) for _ in range(2)]
            if do_c:
                lc = sums.pop(tc)
                rho = lam * lc[0] / lc[1]
                ot = None
            for ci, (k_ref, vt_ref, c) in enumerate(chunks):
                rows = slice(ci * ATT_KC, (ci + 1) * ATT_KC)
                if do_a:
                    for mp in range(2):
                        s = _dot(k_ref[c:c + ATT_KC, :], qts[mp])
                        s_ref[ta % 3, mp, rows, :] = s
                        cm = jnp.max(fold(s), axis=0)
                        m8[mp] = cm if m8[mp] is None else jnp.maximum(m8[mp], cm)
                if do_b:
                    for mp in range(2):
                        e = jnp.exp2(s_ref[tb % 3, mp, rows, :] - mb[mp])
                        s_ref[tb % 3, mp, rows, :] = e
                        l8[mp] = l8[mp] + jnp.sum(fold(e), axis=0)
                if do_c:
                    p = (s_ref[tc % 3, 0, rows, :] - rho * s_ref[tc % 3, 1, rows, :]).astype(BF16)
                    part = _dot(vt_ref[:, c:c + ATT_KC], p)
                    ot = part if ot is None else ot + part
            if do_a:
                maxima[ta] = [jnp.max(m, axis=0, keepdims=True) for m in m8]
            if do_b:
                sums[tb] = [jnp.sum(l, axis=0, keepdims=True) for l in l8]
            if do_c:
                finish(o_ref, tc, ot, lc[0])

    run = pipeline_bounded if bounded else pipeline
    ctx_keys, lat_keys = (kc_ref, vtc_ref, knc_ref), (kl_ref, vtl_ref, knl_ref)
    lmin = run(ql_ref, ol_ref, [ctx_keys, lat_keys])
    if with_ctx:
        lmin_ctx = run(qc_ref, oc_ref, [ctx_keys])
    if bounded:
        if with_ctx:
            lmin = jnp.minimum(lmin, lmin_ctx)
        lmin_ref[...] = jnp.broadcast_to(lmin, lmin_ref.shape)


def _attention_call(qkv_ctx, qkv_lat, lam_vecs, g, layer, lam_init, with_ctx, bounded, cast=()):
    q_lat, k_lat, vt_lat, kn_lat = qkv_lat
    k_ctx, vt_ctx, kn_ctx = qkv_ctx[-3:]
    b, h, n_lat, _ = k_lat.shape
    n_ctx = k_ctx.shape[2] // b
    lat_tiles, ctx_tiles = n_lat // TM, n_ctx // TM

    def tiles_spec(tiles, tail, ctx):
        zeros = (0,) * len(tail)
        if ctx:
            return pl.BlockSpec((None, None, tiles) + tail, lambda i, j: (0, j, i) + zeros)
        return pl.BlockSpec((None, None, tiles) + tail, lambda i, j: (i, j, 0) + zeros)

    q_tile, kn_tile = (2, 2 * LANES, TM), (SUBLANES, LANES)
    in_specs = [_resident(v, layer) for v in lam_vecs] + [
        _resident(g, layer),
        pl.BlockSpec((None, None, n_ctx, LANES), lambda i, j: (0, j, i, 0)),
        pl.BlockSpec((None, None, V_DIM, n_ctx), lambda i, j: (0, j, 0, i)),
        tiles_spec(ctx_tiles, kn_tile, True),
        tiles_spec(lat_tiles, q_tile, False),
        pl.BlockSpec((None, None, n_lat, LANES), lambda i, j: (i, j, 0, 0)),
        pl.BlockSpec((None, None, V_DIM, n_lat), lambda i, j: (i, j, 0, 0)),
        tiles_spec(lat_tiles, kn_tile, False),
    ]
    operands = list(lam_vecs) + [g, k_ctx, vt_ctx, kn_ctx, q_lat, k_lat, vt_lat, kn_lat]
    out_specs = [pl.BlockSpec((None, n_lat, V_DIM), lambda i, j: (i, 0, j))]
    out_shape = [jax.ShapeDtypeStruct((b, n_lat, h * V_DIM), BF16)]
    if with_ctx:
        in_specs.append(tiles_spec(ctx_tiles, q_tile, True))
        operands.append(qkv_ctx[0])
        out_specs.insert(0, pl.BlockSpec((None, n_ctx, V_DIM), lambda i, j: (i, 0, j)))
        out_shape.insert(0, jax.ShapeDtypeStruct((b, n_ctx, h * V_DIM), BF16))
    if bounded:
        out_specs.append(pl.BlockSpec((None, None, SUBLANES, LANES), lambda i, j: (i, j, 0, 0)))
        out_shape.append(jax.ShapeDtypeStruct((b, h, SUBLANES, LANES), F32))
    cast_operands, cast_specs, cast_shapes = _cast_slabs(cast, (b, h))
    in_specs, operands = in_specs + cast_specs, operands + cast_operands
    scores_shape = (2 if bounded else 3, 2, n_ctx + n_lat, TM)
    outs = pl.pallas_call(
        functools.partial(_attn_kernel, lam_init=lam_init, with_ctx=with_ctx, bounded=bounded, n_cast=len(cast)),
        grid=(b, h),
        in_specs=in_specs,
        out_specs=out_specs + cast_specs,
        out_shape=out_shape + cast_shapes,
        scratch_shapes=[pltpu.VMEM(scores_shape, F32)],
        compiler_params=_params(
            ("arbitrary", "arbitrary"),
            streamed=_block_bytes(in_specs + out_specs + cast_specs, operands + out_shape + cast_shapes),
            scratch=_nbytes(scores_shape, F32),
            temporaries=8 * _nbytes((ATT_KC, TM), F32)),
        name="attention_bounded" if bounded else "attention_exact",
    )(*operands)
    n_main = len(out_shape)
    return list(outs[:n_main]), [o.reshape(wt.shape) for o, wt in zip(outs[n_main:], cast)]


def _attention(qkv_ctx, qkv_lat, lam_vecs, g, layer, lam_init, with_ctx, cast=()):
    args = (qkv_ctx, qkv_lat, lam_vecs, g, layer, lam_init, with_ctx)
    (*outs, lmin), casted = _attention_call(*args, bounded=True, cast=cast)
    safe = jnp.min(lmin) >= MIN_SOFTMAX_DENOMINATOR
    outs = lax.cond(safe, lambda: tuple(outs), lambda: tuple(_attention_call(*args, bounded=False)[0]))
    return list(outs), casted


def _fourier_kernel(u_ref, chan_c_ref, chan_s_ref, wf_ref, pos_c_ref, pos_s_ref, mid_ref, flip_ref, y_ref):
    n, width = u_ref.shape
    half = n // 2
    scale = (n * width // FOURIER_GROUPS) ** -0.5
    u = u_ref[...]
    a = _dot(pos_c_ref[...], u).astype(BF16)
    b = _dot(pos_s_ref[...], u).astype(BF16)
    p = _dot(a, chan_c_ref[...])
    q = _dot(b, chan_s_ref[...])
    y_ref[0:half, :] = _dot(((p - q) * scale).astype(BF16), wf_ref[...]).astype(BF16)
    mirrored = _dot(flip_ref[...], ((p + q) * scale).astype(BF16))
    mid = _dot(_dot(mid_ref[...], u).astype(BF16), chan_c_ref[...]) * scale
    rows = mid.shape[0]
    upper = jnp.concatenate([mirrored[0:rows] + mid, mirrored[rows:]], axis=0)
    y_ref[half:, :] = _dot(upper.astype(BF16), wf_ref[...]).astype(BF16)


def _fourier(uf, chan_c, chan_s, wf_bd, layer, pos_tables):
    b, n, fw = uf.shape
    half = n // 2
    idx = jnp.arange(half)
    flip = (idx[:, None] + idx[None, :] == half).astype(BF16)
    mid = np.zeros((SUBLANES, n), np.float32)
    mid[0] = 1.0 - 2.0 * (np.arange(n) % 2)
    mid = jnp.asarray(mid).astype(BF16)
    row_spec = pl.BlockSpec((None, n, fw), lambda i: (i, 0, 0))
    operands = [uf, chan_c, chan_s, wf_bd, *pos_tables, mid, flip]
    in_specs = [row_spec, _resident(chan_c), _resident(chan_s), _resident(wf_bd, layer)] + [
        _resident(t) for t in (*pos_tables, mid, flip)]
    return pl.pallas_call(
        _fourier_kernel,
        grid=(b,),
        in_specs=in_specs,
        out_specs=row_spec,
        out_shape=jax.ShapeDtypeStruct((b, n, fw), BF16),
        compiler_params=_params(
            ("arbitrary",),
            resident=_block_bytes(in_specs[1:], operands[1:]),
            streamed=2 * _nbytes((n, fw), BF16),
            temporaries=8 * _nbytes((half, fw), F32)),
        name="fourier",
    )(*operands)


def _split_dot(x, m):
    hi = x.astype(BF16)
    lo = (x - hi.astype(F32)).astype(BF16)
    return _dot(hi, m) + _dot(lo, m)


def _conv_kernel(u_ref, w_ref, b_ref, lg_ref, lb_ref, avg_ref, pw_ref, y_ref, zpad_ref, acc_a_ref, acc_b_ref):
    n, cw = y_ref.shape
    halo = jnp.zeros((CONV_HALO, cw), F32)

    def taps(i, acc_ref, slot):
        r0 = pl.multiple_of(i * CONV_CHUNK, CONV_CHUNK)
        for lo in range(0, cw, LANES):
            win = zpad_ref[pl.ds(r0, CONV_CHUNK + 2 * CONV_HALO), lo:lo + LANES]
            part = jnp.zeros((CONV_CHUNK, LANES), F32) + b_ref[:, lo:lo + LANES]
            for shift in range(SUBLANES):
                rolled = win if shift == 0 else pltpu.roll(win, win.shape[0] - shift, 0)
                for aligned in range(0, 2 * CONV_HALO, SUBLANES):
                    tap = aligned + shift - (CONV_HALO - CONV_K // 2)
                    if 0 <= tap < CONV_K:
                        part = part + rolled[aligned:aligned + CONV_CHUNK] * w_ref[tap:tap + 1, lo:lo + LANES]
            acc_ref[slot, :, lo:lo + LANES] = part

    def project(acc_ref, slot):
        acc = acc_ref[slot]
        mu = _split_dot(acc, avg_ref[...])
        dev = acc - mu
        var = _split_dot(dev * dev, avg_ref[...])
        zn = dev * lax.rsqrt(var + EPS) * lg_ref[...] + lb_ref[...]
        act = (zn * _sigmoid(zn)).astype(BF16)
        return _dot(act, pw_ref[...]).astype(BF16)

    def round_(first_tap, tap_ref, ready_ref):
        if first_tap is not None:
            taps(first_tap, tap_ref, 0)
            taps(first_tap + 1, tap_ref, 1)
        return [project(ready_ref, 0), project(ready_ref, 1)]

    def store(first_chunk, ys):
        r0 = pl.multiple_of(first_chunk * CONV_CHUNK, 2 * CONV_CHUNK)
        y_ref[pl.ds(r0, len(ys) * CONV_CHUNK), :] = jnp.concatenate(ys, axis=0)

    zpad_ref[0:CONV_HALO, :] = halo
    zpad_ref[CONV_HALO:CONV_HALO + n, :] = u_ref[:, 0:cw] * _sigmoid(u_ref[:, cw:2 * cw])
    zpad_ref[CONV_HALO + n:2 * CONV_HALO + n, :] = halo

    chunks = n // CONV_CHUNK
    assert chunks == 2 or chunks % 4 == 0

    def step(j, carry):
        c0 = 4 * j
        ys = round_(c0 + 2, acc_b_ref, acc_a_ref) + round_(c0 + 4, acc_a_ref, acc_b_ref)
        store(c0, ys)
        return carry

    taps(0, acc_a_ref, 0)
    taps(1, acc_a_ref, 1)
    if chunks == 2:
        store(0, round_(None, None, acc_a_ref))
    else:
        lax.fori_loop(0, chunks // 4 - 1, step, 0)
        store(chunks - 4, round_(chunks - 2, acc_b_ref, acc_a_ref) + round_(None, None, acc_b_ref))


def _conv(uc, conv_w, conv_b, ln_g, ln_b, avg, w_pw, layer):
    b, n, cw2 = uc.shape
    cw = cw2 // 2
    operands = [uc, conv_w, conv_b, ln_g, ln_b, avg, w_pw]
    in_specs = [
        pl.BlockSpec((None, n, cw2), lambda i: (i, 0, 0)),
        _resident(conv_w, layer), _resident(conv_b, layer), _resident(ln_g, layer), _resident(ln_b, layer),
        _resident(avg), _resident(w_pw, layer),
    ]
    scratch = [(n + 2 * CONV_HALO, cw), (2, CONV_CHUNK, cw), (2, CONV_CHUNK, cw)]
    return pl.pallas_call(
        _conv_kernel,
        grid=(b,),
        in_specs=in_specs,
        out_specs=pl.BlockSpec((None, n, cw), lambda i: (i, 0, 0)),
        out_shape=jax.ShapeDtypeStruct((b, n, cw), BF16),
        scratch_shapes=[pltpu.VMEM(s, F32) for s in scratch],
        compiler_params=_params(
            ("arbitrary",),
            resident=_block_bytes(in_specs[1:], operands[1:]),
            streamed=_nbytes((n, cw2), F32) + _nbytes((n, cw), BF16),
            scratch=sum(_nbytes(s, F32) for s in scratch),
            temporaries=3 * _nbytes((n, cw), F32) + 16 * _nbytes((CONV_CHUNK, cw), F32)),
        name="conv",
    )(*operands)


def _ffn_kernel(x_ref, o_ref, yf_ref, yc_ref, mod_ref, g_ref, wout_ref, w1_ref, w3_ref, w2_ref, fg_ref, out_ref,
                *, final, n_sub):
    aw = o_ref.shape[-1]
    fw = yf_ref.shape[-1]
    for u in range(n_sub):
        rows = slice(u * TM, (u + 1) * TM)
        y = (_dot(o_ref[rows, :], wout_ref[0:aw, :]) + _dot(yf_ref[rows, :], wout_ref[aw:aw + fw, :])
             + _dot(yc_ref[rows, :], wout_ref[aw + fw:, :]))
        x1 = x_ref[rows, :] + mod_ref[2:3, :] * y
        h = (_rms(x1) * g_ref[...] * (1.0 + mod_ref[4:5, :]) + mod_ref[3:4, :]).astype(BF16)
        a = _dot(h, w1_ref[...])
        gated = (a * _sigmoid(a) * _dot(h, w3_ref[...])).astype(BF16)
        x2 = x1 + mod_ref[5:6, :] * _dot(gated, w2_ref[...])
        if final:
            x2 = _rms(x2) * fg_ref[...]
        out_ref[rows, :] = x2


def _out_ffn(x, o, yf, yc, mod, g2, w_out, w1, w3, w2, final_g, layer, fixed_row, final):
    b, n, d = x.shape
    n_sub = ROW_SUBTILES
    bm = n_sub * TM
    assert n % bm == 0

    def rows(width):
        return pl.BlockSpec((None, bm, width), lambda i, j: (i, j, 0))

    operands = [x, o, yf, yc, mod, g2, w_out, w1, w3, w2, final_g]
    in_specs = [
        rows(d), rows(o.shape[-1]), rows(yf.shape[-1]), rows(yc.shape[-1]),
        _mod_spec(mod, layer, fixed_row),
        _resident(g2, layer), _resident(w_out, layer), _resident(w1, layer), _resident(w3, layer),
        _resident(w2, layer), _resident(final_g),
    ]
    d_ff = w1.shape[-1]
    return pl.pallas_call(
        functools.partial(_ffn_kernel, final=final, n_sub=n_sub),
        grid=(b, n // bm),
        in_specs=in_specs,
        out_specs=rows(d),
        out_shape=jax.ShapeDtypeStruct((b, n, d), F32),
        compiler_params=_params(
            ("arbitrary", "arbitrary"),
            resident=_block_bytes(in_specs[5:], operands[5:]),
            streamed=_block_bytes(in_specs[:5], operands[:5]) + _nbytes((bm, d), F32),
            temporaries=n_sub * (4 * _nbytes((TM, d), F32) + 2 * _nbytes((TM, d_ff), F32) + _nbytes((TM, d_ff), BF16))),
        name="out_ffn",
    )(*operands)


def _rope_tables(n_lat):
    n_freq = QK_DIM // 4
    tok = np.arange(n_lat)
    inv_freq = np.float32(ROPE_BASE) ** (-np.arange(n_freq, dtype=np.float32) / np.float32(n_freq))
    ang_r = ((tok // GRID_W).astype(np.float32)[:, None] * inv_freq).astype(np.float64)
    ang_c = ((tok % GRID_W).astype(np.float32)[:, None] * inv_freq).astype(np.float64)
    cos = np.concatenate([np.cos(ang_r)] * 2 + [np.cos(ang_c)] * 2, axis=-1)
    sin = np.concatenate([-np.sin(ang_r), np.sin(ang_r), -np.sin(ang_c), np.sin(ang_c)], axis=-1)
    reps = LANES // QK_DIM
    return (jnp.asarray(np.tile(cos, (1, reps)), dtype=F32), jnp.asarray(np.tile(sin, (1, reps)), dtype=F32))


def _dft_angles(rows, n):
    return 2.0 * np.pi * ((np.asarray(rows, np.int64)[:, None] * np.arange(n, dtype=np.int64)[None, :]) % n) / n


def _dft_tables_small(n):
    ang = _dft_angles(np.arange(n), n)
    return np.cos(ang).astype(np.float32), np.sin(ang).astype(np.float32)


def _dft_tables(n):
    rows = n // 2
    if n <= 4 * DFT_LO:
        return [jnp.asarray(t[:rows]).astype(BF16) for t in _dft_tables_small(n)]
    hi = rows // DFT_LO
    ang_hi = _dft_angles(np.arange(hi) * DFT_LO, n)
    ang_lo = _dft_angles(np.arange(DFT_LO), n)
    ch, sh = (jnp.asarray(f(ang_hi), dtype=F32)[:, None, :] for f in (np.cos, np.sin))
    cl, sl = (jnp.asarray(f(ang_lo), dtype=F32)[None, :, :] for f in (np.cos, np.sin))
    return [(ch * cl - sh * sl).reshape(rows, n).astype(BF16), (sh * cl + ch * sl).reshape(rows, n).astype(BF16)]


def _block_diag(blocks):
    g, r, c = blocks.shape[-3:]
    eye = jnp.eye(g, dtype=blocks.dtype)
    out = eye[:, None, :, None] * blocks[..., :, :, None, :]
    return out.reshape(blocks.shape[:-3] + (g * r, g * c))


def _permute_inproj(w_in, qk_w):
    lead = w_in.shape[:-1]
    qk = w_in[..., :4 * qk_w].reshape(lead + (2, 2, HEADS, QK_DIM))
    qk = jnp.swapaxes(qk, -3, -2).reshape(lead + (4 * qk_w,))
    return jnp.concatenate([qk, w_in[..., 4 * qk_w:]], axis=-1)


def kernel(x, c, ctx, c_ctx, w_ada, b_ada, norm1_g, norm2_g, w_in, lam_q1, lam_k1, lam_q2, lam_k2, subln_g,
           w_fourier, conv_w, conv_b, conv_ln_g, conv_ln_b, w_conv_out, w_out, w_ffn1, w_ffn3, w_ffn2, final_g):
    b, n_lat, d = x.shape
    n_ctx = ctx.shape[1]
    depth = w_ada.shape[0]
    fw = w_fourier.shape[1] * w_fourier.shape[2]
    cw = conv_w.shape[-1]
    qk_w = HEADS * QK_DIM
    attn_w = HEADS * V_DIM
    assert n_lat % GRID_W == 0 and n_ctx % TM == 0
    assert w_in.shape[-1] == 4 * qk_w + attn_w + fw + 2 * cw

    pad = (-(b + 1)) % SUBLANES
    c_rows = jnp.concatenate([c, c_ctx[None, :], jnp.zeros((pad, d), c.dtype)], axis=0)
    mod = _ada(c_rows, w_ada, b_ada)
    mod = mod.reshape(depth, mod.shape[1], 6, d)
    ctx_row = b

    rope = _rope_tables(n_lat)
    cc, cs = _dft_tables_small(fw // FOURIER_GROUPS)
    eye = np.eye(FOURIER_GROUPS, dtype=np.float32)
    chan_c = jnp.asarray(np.kron(eye, cc)).astype(BF16)
    chan_s = jnp.asarray(np.kron(eye, cs)).astype(BF16)
    lat_tables = _dft_tables(n_lat)
    ctx_tables = _dft_tables(n_ctx)
    group = cw // CONV_GROUPS
    avg = jnp.asarray(np.kron(np.eye(CONV_GROUPS), np.full((group, group), 1.0 / group)), dtype=F32).astype(BF16)

    w_in_b = _permute_inproj(w_in, qk_w).astype(BF16)
    wf_bd = _block_diag(w_fourier).astype(BF16)
    w_pw_b = w_conv_out.astype(BF16)
    w_out_b = w1_b = w3_b = w2_b = None

    def per_layer_rows(a):
        return a.reshape(depth, 1, a.shape[-1])

    norm1, norm2, subln = per_layer_rows(norm1_g), per_layer_rows(norm2_g), per_layer_rows(subln_g)
    lam_vecs = [per_layer_rows(a) for a in (lam_q1, lam_k1, lam_q2, lam_k2)]
    cb, lg, lb = per_layer_rows(conv_b), per_layer_rows(conv_ln_g), per_layer_rows(conv_ln_b)
    fg = final_g[None, :]

    def mixers(u_f, u_c, tables, l):
        yf = _fourier(u_f, chan_c, chan_s, wf_bd, l, tables)
        yc = _conv(u_c, conv_w, cb, lg, lb, avg, w_pw_b, l)
        return yf, yc

    ctx_flat = ctx.reshape(1, b * n_ctx, d)
    for l in range(depth):
        last = l == depth - 1
        lam_init = 0.8 - 0.6 * math.exp(-0.3 * l)
        q, k, vt, kn, uf, uc = _inproj(x, mod, norm1, w_in_b, rope, l, None, fw, 2 * cw)
        yf, yc = mixers(uf, uc, lat_tables, l)
        cast = (w_out, w_ffn1, w_ffn3, w_ffn2) if l == 0 else ()
        if last:
            kv_ctx = _inproj(ctx_flat, mod, norm1, w_in_b, None, l, ctx_row, fw, 2 * cw, kv_only=True)
            (o,), casted = _attention(kv_ctx, (q, k, vt, kn), lam_vecs, subln, l, lam_init, False, cast)
        else:
            qc, kc, vtc, knc, ufc, ucc = _inproj(ctx_flat, mod, norm1, w_in_b, None, l, ctx_row, fw, 2 * cw)
            (oc, o), casted = _attention((qc, kc, vtc, knc), (q, k, vt, kn), lam_vecs, subln, l, lam_init, True, cast)
        if cast:
            w_out_b, w1_b, w3_b, w2_b = casted
        if not last:
            yfc, ycc = mixers(ufc.reshape(b, n_ctx, fw), ucc.reshape(b, n_ctx, 2 * cw), ctx_tables, l)
            ctx_flat = _out_ffn(ctx_flat, oc.reshape(1, b * n_ctx, attn_w), yfc.reshape(1, b * n_ctx, fw),
                                ycc.reshape(1, b * n_ctx, cw), mod, norm2, w_out_b, w1_b, w3_b, w2_b, fg, l,
                                ctx_row, final=False)
        x = _out_ffn(x, o, yf, yc, mod, norm2, w_out_b, w1_b, w3_b, w2_b, fg, l, None, final=last)
    return x
```

```python
import functools
import math

import numpy as np
import jax
import jax.numpy as jnp
from jax import lax
from jax.experimental import pallas as pl
from jax.experimental.pallas import tpu as pltpu

F32 = jnp.float32
BF16 = jnp.bfloat16

GRID_W = 64
HEADS = 4
QK_DIM = 64
V_DIM = 2 * QK_DIM
FOURIER_GROUPS = 4
CONV_GROUPS = 4
CONV_K = 31
ROPE_BASE = 10000.0
EPS = 1e-6
LOG2_E = math.log2(math.e)
BOUND_SLACK = 1.0 + 2.0 ** -6
MIN_SOFTMAX_DENOMINATOR = 2.0 ** -90

LANES = 128
SUBLANES = 8
TM = 256
ROW_SUBTILES = 2
ATT_KC = 256
CONV_HALO = 16
CONV_CHUNK = 128
ADA_TN = 1024
DFT_LO = 64
V7X_VMEM_BYTES = 64 * 1024 * 1024
VMEM_REQUEST_BYTES = V7X_VMEM_BYTES * 15 // 16


def _nbytes(shape, dtype):
    return math.prod(shape) * jnp.dtype(dtype).itemsize


def _block_bytes(specs, arrays):
    return sum(_nbytes([s for s in spec.block_shape if s is not None], arr.dtype) for spec, arr in zip(specs, arrays))


def _params(sem, *, resident=0, streamed=0, scratch=0, temporaries=0):
    need = resident + 2 * streamed + scratch + temporaries
    assert need <= VMEM_REQUEST_BYTES, f"VMEM estimate {need} exceeds the request {VMEM_REQUEST_BYTES}"
    return pltpu.CompilerParams(dimension_semantics=sem, vmem_limit_bytes=VMEM_REQUEST_BYTES)


def _resident(arr, layer=None):
    if layer is None:
        idx = (0,) * arr.ndim
        return pl.BlockSpec(arr.shape, lambda *_: idx, pipeline_mode=pl.Buffered(1))
    idx = (layer,) + (0,) * (arr.ndim - 1)
    return pl.BlockSpec((None,) + arr.shape[1:], lambda *_: idx, pipeline_mode=pl.Buffered(1))


def _mod_spec(mod, layer, fixed_row):
    if fixed_row is None:
        return pl.BlockSpec((None, None) + mod.shape[2:], lambda i, j: (layer, i, 0, 0))
    return pl.BlockSpec((None, None) + mod.shape[2:], lambda i, j: (layer, fixed_row, 0, 0))


def _cast_slabs(weights, grid):
    steps = grid[0] * grid[1]
    operands, specs, shapes = [], [], []
    for wt in weights:
        flat = wt.reshape(-1, wt.shape[-1])
        slab = flat.shape[0] // steps
        assert slab * steps == flat.shape[0] and slab % (2 * SUBLANES) == 0
        operands.append(flat)
        specs.append(pl.BlockSpec((slab, flat.shape[1]), lambda i, j: (i * grid[1] + j, 0)))
        shapes.append(jax.ShapeDtypeStruct(flat.shape, BF16))
    return operands, specs, shapes


def _sigmoid(x):
    return 1.0 / (1.0 + jnp.exp(-x))


def _rms(x):
    return x * lax.rsqrt(jnp.mean(x * x, axis=-1, keepdims=True) + EPS)


def _dot(a, b):
    return jnp.dot(a, b, preferred_element_type=F32)


def _ada_kernel(c_ref, w_ref, b_ref, o_ref):
    c = c_ref[...]
    s = (c * _sigmoid(c)).astype(BF16)
    o_ref[...] = _dot(s, w_ref[...].astype(BF16)) + b_ref[...]


def _ada(c_rows, w_ada, b_ada):
    depth, d, n = w_ada.shape
    rows = c_rows.shape[0]
    return pl.pallas_call(
        _ada_kernel,
        grid=(depth, n // ADA_TN),
        in_specs=[
            pl.BlockSpec((rows, d), lambda l, j: (0, 0)),
            pl.BlockSpec((None, d, ADA_TN), lambda l, j: (l, 0, j)),
            pl.BlockSpec((None, 1, ADA_TN), lambda l, j: (l, 0, j)),
        ],
        out_specs=pl.BlockSpec((None, rows, ADA_TN), lambda l, j: (l, 0, j)),
        out_shape=jax.ShapeDtypeStruct((depth, rows, n), F32),
        compiler_params=_params(
            ("arbitrary", "arbitrary"),
            resident=_nbytes(c_rows.shape, F32),
            streamed=_nbytes((d + 1 + rows, ADA_TN), F32),
            temporaries=_nbytes((d, ADA_TN), BF16) + _nbytes((rows, ADA_TN), F32)),
        name="ada",
    )(c_rows, w_ada, b_ada.reshape(depth, 1, n))


def _inproj_kernel(*refs, use_rope, kv_only, n_sub):
    x_ref, mod_ref, g_ref, w_ref = refs[:4]
    refs = refs[4:]
    if use_rope:
        cos_ref, sin_ref = refs[:2]
        refs = refs[2:]
    hw = HEADS * LANES
    lane = lax.broadcasted_iota(jnp.int32, (TM, LANES), 1)
    first_half = (lane & (QK_DIM // 4)) == 0
    row = lax.broadcasted_iota(jnp.int32, (LANES, TM), 0)
    zero = jnp.zeros((LANES, TM), F32)
    dim = lax.broadcasted_iota(jnp.int32, (LANES, LANES), 0)
    col = lax.broadcasted_iota(jnp.int32, (LANES, LANES), 1)
    map_sum = jnp.where(col == jnp.where(dim < QK_DIM, 0, 1), 1.0, 0.0).astype(BF16)

    for u in range(n_sub):
        rows = slice(u * TM, (u + 1) * TM)
        h = (_rms(x_ref[rows, :]) * g_ref[...] * (1.0 + mod_ref[1:2, :]) + mod_ref[0:1, :]).astype(BF16)

        def rope(t):
            if not use_rope:
                return t
            partner = jnp.where(first_half, pltpu.roll(t, LANES - QK_DIM // 4, 1), pltpu.roll(t, QK_DIM // 4, 1))
            return t * cos_ref[rows, :] + partner * sin_ref[rows, :]

        def put_keys(j, kf):
            k_ref[j, rows, :] = kf.astype(BF16)
            norms = _dot((kf * kf).astype(BF16), map_sum)
            kn_ref[j, u] = jnp.max(norms.reshape(TM // SUBLANES, SUBLANES, LANES), axis=0)

        if kv_only:
            k_ref, vt_ref, kn_ref = refs
            r = _dot(h, w_ref[:, hw:3 * hw])
            for j in range(HEADS):
                put_keys(j, rope(r[:, j * LANES:(j + 1) * LANES]))
                vt_ref[j, :, rows] = r[:, hw + j * LANES:hw + (j + 1) * LANES].T.astype(BF16)
            continue

        q_ref, k_ref, vt_ref, kn_ref, uf_ref, uc_ref = refs
        r = _dot(h, w_ref[...])
        for j in range(HEADS):
            qt = (rope(r[:, j * LANES:(j + 1) * LANES]) * (QK_DIM ** -0.5 * LOG2_E)).T
            for mp in range(2):
                sel = (row < QK_DIM) == (mp == 0)
                qm = qt[mp * QK_DIM:(mp + 1) * QK_DIM, :]
                qn = jnp.sqrt(jnp.sum(qm * qm, axis=0, keepdims=True)) * BOUND_SLACK
                q_ref[j, u, mp, 0:LANES, :] = jnp.where(sel, qt, zero).astype(BF16)
                q_ref[j, u, mp, LANES:2 * LANES, :] = jnp.where(row == 0, qn, zero).astype(BF16)
            put_keys(j, rope(r[:, hw + j * LANES:hw + (j + 1) * LANES]))
            vt_ref[j, :, rows] = r[:, 2 * hw + j * LANES:2 * hw + (j + 1) * LANES].T.astype(BF16)
        fw = uf_ref.shape[-1]
        uf_ref[rows, :] = r[:, 3 * hw:3 * hw + fw].astype(BF16)
        uc_ref[rows, :] = r[:, 3 * hw + fw:]


def _inproj(x, mod, g, w, rope_tables, layer, fixed_row, fw, cw2, kv_only=False):
    b, n, d = x.shape
    n_sub = ROW_SUBTILES
    bm = n_sub * TM
    assert n % bm == 0
    use_rope = rope_tables is not None
    in_specs = [
        pl.BlockSpec((None, bm, d), lambda i, j: (i, j, 0)),
        _mod_spec(mod, layer, fixed_row),
        _resident(g, layer),
        _resident(w, layer),
    ]
    operands = [x, mod, g, w]
    if use_rope:
        in_specs += [pl.BlockSpec((bm, LANES), lambda i, j: (j, 0))] * 2
        operands += list(rope_tables)
    out_specs = [
        pl.BlockSpec((None, HEADS, n_sub, 2, 2 * LANES, TM), lambda i, j: (i, 0, j, 0, 0, 0)),
        pl.BlockSpec((None, HEADS, bm, LANES), lambda i, j: (i, 0, j, 0)),
        pl.BlockSpec((None, HEADS, V_DIM, bm), lambda i, j: (i, 0, 0, j)),
        pl.BlockSpec((None, HEADS, n_sub, SUBLANES, LANES), lambda i, j: (i, 0, j, 0, 0)),
        pl.BlockSpec((None, bm, fw), lambda i, j: (i, j, 0)),
        pl.BlockSpec((None, bm, cw2), lambda i, j: (i, j, 0)),
    ]
    out_shape = [
        jax.ShapeDtypeStruct((b, HEADS, n // TM, 2, 2 * LANES, TM), BF16),
        jax.ShapeDtypeStruct((b, HEADS, n, LANES), BF16),
        jax.ShapeDtypeStruct((b, HEADS, V_DIM, n), BF16),
        jax.ShapeDtypeStruct((b, HEADS, n // TM, SUBLANES, LANES), F32),
        jax.ShapeDtypeStruct((b, n, fw), BF16),
        jax.ShapeDtypeStruct((b, n, cw2), F32),
    ]
    if kv_only:
        out_specs, out_shape = out_specs[1:4], out_shape[1:4]
    return pl.pallas_call(
        functools.partial(_inproj_kernel, use_rope=use_rope, kv_only=kv_only, n_sub=n_sub),
        grid=(b, n // bm),
        in_specs=in_specs,
        out_specs=out_specs,
        out_shape=out_shape,
        compiler_params=_params(
            ("arbitrary", "arbitrary"),
            resident=_block_bytes(in_specs[2:4], operands[2:4]),
            streamed=_block_bytes(in_specs[:2] + in_specs[4:] + out_specs, operands[:2] + operands[4:] + out_shape),
            temporaries=2 * n_sub * _nbytes((TM, w.shape[-1]), F32)),
        name="inproj",
    )(*operands)


def _attn_kernel(*refs, lam_init, with_ctx, bounded, n_cast):
    lq1_ref, lk1_ref, lq2_ref, lk2_ref, g_ref, kc_ref, vtc_ref, knc_ref, ql_ref, kl_ref, vtl_ref, knl_ref = refs[:12]
    refs = list(refs[12:])
    s_ref = refs.pop()
    qc_ref = refs.pop(0) if with_ctx else None
    cast_src = [refs.pop(0) for _ in range(n_cast)]
    oc_ref = refs.pop(0) if with_ctx else None
    ol_ref = refs.pop(0)
    lmin_ref = refs.pop(0) if bounded else None
    for src_ref, dst_ref in zip(cast_src, refs):
        dst_ref[...] = src_ref[...].astype(BF16)
    lam = (jnp.exp(jnp.sum(lq1_ref[...] * lk1_ref[...], axis=-1, keepdims=True))
           - jnp.exp(jnp.sum(lq2_ref[...] * lk2_ref[...], axis=-1, keepdims=True)) + lam_init)

    def fold(x):
        return x.reshape(x.shape[0] // SUBLANES, SUBLANES, x.shape[1])

    def finish(o_ref, t, ot, l1):
        ot = ot * (1.0 / l1)
        ms = jnp.mean(ot * ot, axis=0, keepdims=True)
        on = (ot * lax.rsqrt(ms + EPS)).T * g_ref[...] * (1.0 - lam_init)
        o_ref[t * TM:(t + 1) * TM, :] = on.astype(BF16)

    def key_chunks(key_refs):
        return [(k_ref, vt_ref, c) for k_ref, vt_ref, _ in key_refs for c in range(0, k_ref.shape[0], ATT_KC)]

    def pipeline_bounded(q_ref, o_ref, key_refs):
        chunks = key_chunks(key_refs)
        n_tiles = q_ref.shape[0]
        lane = lax.broadcasted_iota(jnp.int32, (ATT_KC, LANES), 1)
        key_cols = []
        ksq = functools.reduce(jnp.maximum, [jnp.max(kn_ref[...], axis=0) for _, _, kn_ref in key_refs])
        kb = jnp.sqrt(jnp.max(ksq, axis=0, keepdims=True)) * BOUND_SLACK
        for mp in range(2):
            key_cols.append(jnp.where(lane == 0, -kb[:, mp:mp + 1], 0.0).astype(BF16))
        sums = {}
        lmin = None
        for u in range(n_tiles + 1):
            ta, tc = u, u - 1
            do_a, do_c = ta < n_tiles, 0 <= tc < n_tiles
            if do_a:
                qts = [q_ref[ta, mp] for mp in range(2)]
                l8 = [jnp.zeros((SUBLANES, TM), F32) for _ in range(2)]
            if do_c:
                lc = sums.pop(tc)
                rho = lam * lc[0] / lc[1]
                ot = None
            for ci, (k_ref, vt_ref, c) in enumerate(chunks):
                rows = slice(ci * ATT_KC, (ci + 1) * ATT_KC)
                if do_a:
                    for mp in range(2):
                        keys = jnp.concatenate([k_ref[c:c + ATT_KC, :], key_cols[mp]], axis=1)
                        e = jnp.exp2(_dot(keys, qts[mp]))
                        s_ref[ta % 2, mp, rows, :] = e
                        l8[mp] = l8[mp] + jnp.sum(fold(e), axis=0)
                if do_c:
                    p = (s_ref[tc % 2, 0, rows, :] - rho * s_ref[tc % 2, 1, rows, :]).astype(BF16)
                    part = _dot(vt_ref[:, c:c + ATT_KC], p)
                    ot = part if ot is None else ot + part
            if do_a:
                sums[ta] = [jnp.sum(l, axis=0, keepdims=True) for l in l8]
                low = jnp.minimum(sums[ta][0], sums[ta][1])
                lmin = low if lmin is None else jnp.minimum(lmin, low)
            if do_c:
                finish(o_ref, tc, ot, lc[0])
        return jnp.min(lmin, axis=1, keepdims=True)

    def pipeline(q_ref, o_ref, key_refs):
        chunks = key_chunks(key_refs)
        n_tiles = q_ref.shape[0]
        maxima, sums = {}, {}
        for u in range(n_tiles + 2):
            ta, tb, tc = u, u - 1, u - 2
            do_a, do_b, do_c = ta < n_tiles, 0 <= tb < n_tiles, 0 <= tc < n_tiles
            if do_a:
                qts = [q_ref[ta, mp, 0:LANES, :] for mp in range(2)]
                m8 = [None, None]
            if do_b:
                mb = maxima.pop(tb)
                l8 = [jnp.zeros((SUBLANES, TM), F32) for _ in range(2)]
            if do_c:
                lc = sums.pop(tc)
                rho = lam * lc[0] / lc[1]
                ot = None
            for ci, (k_ref, vt_ref, c) in enumerate(chunks):
                rows = slice(ci * ATT_KC, (ci + 1) * ATT_KC)
                if do_a:
                    for mp in range(2):
                        s = _dot(k_ref[c:c + ATT_KC, :], qts[mp])
                        s_ref[ta % 3, mp, rows, :] = s
                        cm = jnp.max(fold(s), axis=0)
                        m8[mp] = cm if m8[mp] is None else jnp.maximum(m8[mp], cm)
                if do_b:
                    for mp in range(2):
                        e = jnp.exp2(s_ref[tb % 3, mp, rows, :] - mb[mp])
                        s_ref[tb % 3, mp, rows, :] = e
                        l8[mp] = l8[mp] + jnp.sum(fold(e), axis=0)
                if do_c:
                    p = (s_ref[tc % 3, 0, rows, :] - rho * s_ref[tc % 3, 1, rows, :]).astype(BF16)
                    part = _dot(vt_ref[:, c:c + ATT_KC], p)
                    ot = part if ot is None else ot + part
            if do_a:
                maxima[ta] = [jnp.max(m, axis=0, keepdims=True) for m in m8]
            if do_b:
                sums[tb] = [jnp.sum(l, axis=0, keepdims=True) for l in l8]
            if do_c:
                finish(o_ref, tc, ot, lc[0])

    run = pipeline_bounded if bounded else pipeline
    ctx_keys, lat_keys = (kc_ref, vtc_ref, knc_ref), (kl_ref, vtl_ref, knl_ref)
    lmin = run(ql_ref, ol_ref, [ctx_keys, lat_keys])
    if with_ctx:
        lmin_ctx = run(qc_ref, oc_ref, [ctx_keys])
    if bounded:
        if with_ctx:
            lmin = jnp.minimum(lmin, lmin_ctx)
        lmin_ref[...] = jnp.broadcast_to(lmin, lmin_ref.shape)


def _attention_call(qkv_ctx, qkv_lat, lam_vecs, g, layer, lam_init, with_ctx, bounded, cast=()):
    q_lat, k_lat, vt_lat, kn_lat = qkv_lat
    k_ctx, vt_ctx, kn_ctx = qkv_ctx[-3:]
    b, h, n_lat, _ = k_lat.shape
    n_ctx = k_ctx.shape[2] // b
    lat_tiles, ctx_tiles = n_lat // TM, n_ctx // TM

    def tiles_spec(tiles, tail, ctx):
        zeros = (0,) * len(tail)
        if ctx:
            return pl.BlockSpec((None, None, tiles) + tail, lambda i, j: (0, j, i) + zeros)
        return pl.BlockSpec((None, None, tiles) + tail, lambda i, j: (i, j, 0) + zeros)

    q_tile, kn_tile = (2, 2 * LANES, TM), (SUBLANES, LANES)
    in_specs = [_resident(v, layer) for v in lam_vecs] + [
        _resident(g, layer),
        pl.BlockSpec((None, None, n_ctx, LANES), lambda i, j: (0, j, i, 0)),
        pl.BlockSpec((None, None, V_DIM, n_ctx), lambda i, j: (0, j, 0, i)),
        tiles_spec(ctx_tiles, kn_tile, True),
        tiles_spec(lat_tiles, q_tile, False),
        pl.BlockSpec((None, None, n_lat, LANES), lambda i, j: (i, j, 0, 0)),
        pl.BlockSpec((None, None, V_DIM, n_lat), lambda i, j: (i, j, 0, 0)),
        tiles_spec(lat_tiles, kn_tile, False),
    ]
    operands = list(lam_vecs) + [g, k_ctx, vt_ctx, kn_ctx, q_lat, k_lat, vt_lat, kn_lat]
    out_specs = [pl.BlockSpec((None, n_lat, V_DIM), lambda i, j: (i, 0, j))]
    out_shape = [jax.ShapeDtypeStruct((b, n_lat, h * V_DIM), BF16)]
    if with_ctx:
        in_specs.append(tiles_spec(ctx_tiles, q_tile, True))
        operands.append(qkv_ctx[0])
        out_specs.insert(0, pl.BlockSpec((None, n_ctx, V_DIM), lambda i, j: (i, 0, j)))
        out_shape.insert(0, jax.ShapeDtypeStruct((b, n_ctx, h * V_DIM), BF16))
    if bounded:
        out_specs.append(pl.BlockSpec((None, None, SUBLANES, LANES), lambda i, j: (i, j, 0, 0)))
        out_shape.append(jax.ShapeDtypeStruct((b, h, SUBLANES, LANES), F32))
    cast_operands, cast_specs, cast_shapes = _cast_slabs(cast, (b, h))
    in_specs, operands = in_specs + cast_specs, operands + cast_operands
    scores_shape = (2 if bounded else 3, 2, n_ctx + n_lat, TM)
    outs = pl.pallas_call(
        functools.partial(_attn_kernel, lam_init=lam_init, with_ctx=with_ctx, bounded=bounded, n_cast=len(cast)),
        grid=(b, h),
        in_specs=in_specs,
        out_specs=out_specs + cast_specs,
        out_shape=out_shape + cast_shapes,
        scratch_shapes=[pltpu.VMEM(scores_shape, F32)],
        compiler_params=_params(
            ("arbitrary", "arbitrary"),
            streamed=_block_bytes(in_specs + out_specs + cast_specs, operands + out_shape + cast_shapes),
            scratch=_nbytes(scores_shape, F32),
            temporaries=8 * _nbytes((ATT_KC, TM), F32)),
        name="attention_bounded" if bounded else "attention_exact",
    )(*operands)
    n_main = len(out_shape)
    return list(outs[:n_main]), [o.reshape(wt.shape) for o, wt in zip(outs[n_main:], cast)]


def _attention(qkv_ctx, qkv_lat, lam_vecs, g, layer, lam_init, with_ctx, cast=()):
    args = (qkv_ctx, qkv_lat, lam_vecs, g, layer, lam_init, with_ctx)
    (*outs, lmin), casted = _attention_call(*args, bounded=True, cast=cast)
    safe = jnp.min(lmin) >= MIN_SOFTMAX_DENOMINATOR
    outs = lax.cond(safe, lambda: tuple(outs), lambda: tuple(_attention_call(*args, bounded=False)[0]))
    return list(outs), casted


def _fourier_kernel(u_ref, chan_c_ref, chan_s_ref, wf_ref, pos_c_ref, pos_s_ref, mid_ref, flip_ref, y_ref):
    n, width = u_ref.shape
    half = n // 2
    scale = (n * width // FOURIER_GROUPS) ** -0.5
    u = u_ref[...]
    a = _dot(pos_c_ref[...], u).astype(BF16)
    b = _dot(pos_s_ref[...], u).astype(BF16)
    p = _dot(a, chan_c_ref[...])
    q = _dot(b, chan_s_ref[...])
    y_ref[0:half, :] = _dot(((p - q) * scale).astype(BF16), wf_ref[...]).astype(BF16)
    mirrored = _dot(flip_ref[...], ((p + q) * scale).astype(BF16))
    mid = _dot(_dot(mid_ref[...], u).astype(BF16), chan_c_ref[...]) * scale
    rows = mid.shape[0]
    upper = jnp.concatenate([mirrored[0:rows] + mid, mirrored[rows:]], axis=0)
    y_ref[half:, :] = _dot(upper.astype(BF16), wf_ref[...]).astype(BF16)


def _fourier(uf, chan_c, chan_s, wf_bd, layer, pos_tables):
    b, n, fw = uf.shape
    half = n // 2
    idx = jnp.arange(half)
    flip = (idx[:, None] + idx[None, :] == half).astype(BF16)
    mid = np.zeros((SUBLANES, n), np.float32)
    mid[0] = 1.0 - 2.0 * (np.arange(n) % 2)
    mid = jnp.asarray(mid).astype(BF16)
    row_spec = pl.BlockSpec((None, n, fw), lambda i: (i, 0, 0))
    operands = [uf, chan_c, chan_s, wf_bd, *pos_tables, mid, flip]
    in_specs = [row_spec, _resident(chan_c), _resident(chan_s), _resident(wf_bd, layer)] + [
        _resident(t) for t in (*pos_tables, mid, flip)]
    return pl.pallas_call(
        _fourier_kernel,
        grid=(b,),
        in_specs=in_specs,
        out_specs=row_spec,
        out_shape=jax.ShapeDtypeStruct((b, n, fw), BF16),
        compiler_params=_params(
            ("arbitrary",),
            resident=_block_bytes(in_specs[1:], operands[1:]),
            streamed=2 * _nbytes((n, fw), BF16),
            temporaries=8 * _nbytes((half, fw), F32)),
        name="fourier",
    )(*operands)


def _split_dot(x, m):
    hi = x.astype(BF16)
    lo = (x - hi.astype(F32)).astype(BF16)
    return _dot(hi, m) + _dot(lo, m)


def _conv_kernel(u_ref, w_ref, b_ref, lg_ref, lb_ref, avg_ref, pw_ref, y_ref, zpad_ref, acc_a_ref, acc_b_ref):
    n, cw = y_ref.shape
    halo = jnp.zeros((CONV_HALO, cw), F32)

    def taps(i, acc_ref, slot):
        r0 = pl.multiple_of(i * CONV_CHUNK, CONV_CHUNK)
        for lo in range(0, cw, LANES):
            win = zpad_ref[pl.ds(r0, CONV_CHUNK + 2 * CONV_HALO), lo:lo + LANES]
            part = jnp.zeros((CONV_CHUNK, LANES), F32) + b_ref[:, lo:lo + LANES]
            for shift in range(SUBLANES):
                rolled = win if shift == 0 else pltpu.roll(win, win.shape[0] - shift, 0)
                for aligned in range(0, 2 * CONV_HALO, SUBLANES):
                    tap = aligned + shift - (CONV_HALO - CONV_K // 2)
                    if 0 <= tap < CONV_K:
                        part = part + rolled[aligned:aligned + CONV_CHUNK] * w_ref[tap:tap + 1, lo:lo + LANES]
            acc_ref[slot, :, lo:lo + LANES] = part

    def project(acc_ref, slot):
        acc = acc_ref[slot]
        mu = _split_dot(acc, avg_ref[...])
        dev = acc - mu
        var = _split_dot(dev * dev, avg_ref[...])
        zn = dev * lax.rsqrt(var + EPS) * lg_ref[...] + lb_ref[...]
        act = (zn * _sigmoid(zn)).astype(BF16)
        return _dot(act, pw_ref[...]).astype(BF16)

    def round_(first_tap, tap_ref, ready_ref):
        if first_tap is not None:
            taps(first_tap, tap_ref, 0)
            taps(first_tap + 1, tap_ref, 1)
        return [project(ready_ref, 0), project(ready_ref, 1)]

    def store(first_chunk, ys):
        r0 = pl.multiple_of(first_chunk * CONV_CHUNK, 2 * CONV_CHUNK)
        y_ref[pl.ds(r0, len(ys) * CONV_CHUNK), :] = jnp.concatenate(ys, axis=0)

    zpad_ref[0:CONV_HALO, :] = halo
    zpad_ref[CONV_HALO:CONV_HALO + n, :] = u_ref[:, 0:cw] * _sigmoid(u_ref[:, cw:2 * cw])
    zpad_ref[CONV_HALO + n:2 * CONV_HALO + n, :] = halo

    chunks = n // CONV_CHUNK
    assert chunks == 2 or chunks % 4 == 0

    def step(j, carry):
        c0 = 4 * j
        ys = round_(c0 + 2, acc_b_ref, acc_a_ref) + round_(c0 + 4, acc_a_ref, acc_b_ref)
        store(c0, ys)
        return carry

    taps(0, acc_a_ref, 0)
    taps(1, acc_a_ref, 1)
    if chunks == 2:
        store(0, round_(None, None, acc_a_ref))
    else:
        lax.fori_loop(0, chunks // 4 - 1, step, 0)
        store(chunks - 4, round_(chunks - 2, acc_b_ref, acc_a_ref) + round_(None, None, acc_b_ref))


def _conv(uc, conv_w, conv_b, ln_g, ln_b, avg, w_pw, layer):
    b, n, cw2 = uc.shape
    cw = cw2 // 2
    operands = [uc, conv_w, conv_b, ln_g, ln_b, avg, w_pw]
    in_specs = [
        pl.BlockSpec((None, n, cw2), lambda i: (i, 0, 0)),
        _resident(conv_w, layer), _resident(conv_b, layer), _resident(ln_g, layer), _resident(ln_b, layer),
        _resident(avg), _resident(w_pw, layer),
    ]
    scratch = [(n + 2 * CONV_HALO, cw), (2, CONV_CHUNK, cw), (2, CONV_CHUNK, cw)]
    return pl.pallas_call(
        _conv_kernel,
        grid=(b,),
        in_specs=in_specs,
        out_specs=pl.BlockSpec((None, n, cw), lambda i: (i, 0, 0)),
        out_shape=jax.ShapeDtypeStruct((b, n, cw), BF16),
        scratch_shapes=[pltpu.VMEM(s, F32) for s in scratch],
        compiler_params=_params(
            ("arbitrary",),
            resident=_block_bytes(in_specs[1:], operands[1:]),
            streamed=_nbytes((n, cw2), F32) + _nbytes((n, cw), BF16),
            scratch=sum(_nbytes(s, F32) for s in scratch),
            temporaries=3 * _nbytes((n, cw), F32) + 16 * _nbytes((CONV_CHUNK, cw), F32)),
        name="conv",
    )(*operands)


def _ffn_kernel(x_ref, o_ref, yf_ref, yc_ref, mod_ref, g_ref, wout_ref, w1_ref, w3_ref, w2_ref, fg_ref, out_ref,
                *, final, n_sub):
    aw = o_ref.shape[-1]
    fw = yf_ref.shape[-1]
    for u in range(n_sub):
        rows = slice(u * TM, (u + 1) * TM)
        y = (_dot(o_ref[rows, :], wout_ref[0:aw, :]) + _dot(yf_ref[rows, :], wout_ref[aw:aw + fw, :])
             + _dot(yc_ref[rows, :], wout_ref[aw + fw:, :]))
        x1 = x_ref[rows, :] + mod_ref[2:3, :] * y
        h = (_rms(x1) * g_ref[...] * (1.0 + mod_ref[4:5, :]) + mod_ref[3:4, :]).astype(BF16)
        a = _dot(h, w1_ref[...])
        gated = (a * _sigmoid(a) * _dot(h, w3_ref[...])).astype(BF16)
        x2 = x1 + mod_ref[5:6, :] * _dot(gated, w2_ref[...])
        if final:
            x2 = _rms(x2) * fg_ref[...]
        out_ref[rows, :] = x2


def _out_ffn(x, o, yf, yc, mod, g2, w_out, w1, w3, w2, final_g, layer, fixed_row, final):
    b, n, d = x.shape
    n_sub = ROW_SUBTILES
    bm = n_sub * TM
    assert n % bm == 0

    def rows(width):
        return pl.BlockSpec((None, bm, width), lambda i, j: (i, j, 0))

    operands = [x, o, yf, yc, mod, g2, w_out, w1, w3, w2, final_g]
    in_specs = [
        rows(d), rows(o.shape[-1]), rows(yf.shape[-1]), rows(yc.shape[-1]),
        _mod_spec(mod, layer, fixed_row),
        _resident(g2, layer), _resident(w_out, layer), _resident(w1, layer), _resident(w3, layer),
        _resident(w2, layer), _resident(final_g),
    ]
    d_ff = w1.shape[-1]
    return pl.pallas_call(
        functools.partial(_ffn_kernel, final=final, n_sub=n_sub),
        grid=(b, n // bm),
        in_specs=in_specs,
        out_specs=rows(d),
        out_shape=jax.ShapeDtypeStruct((b, n, d), F32),
        compiler_params=_params(
            ("arbitrary", "arbitrary"),
            resident=_block_bytes(in_specs[5:], operands[5:]),
            streamed=_block_bytes(in_specs[:5], operands[:5]) + _nbytes((bm, d), F32),
            temporaries=n_sub * (4 * _nbytes((TM, d), F32) + 2 * _nbytes((TM, d_ff), F32) + _nbytes((TM, d_ff), BF16))),
        name="out_ffn",
    )(*operands)


def _rope_tables(n_lat):
    n_freq = QK_DIM // 4
    tok = np.arange(n_lat)
    inv_freq = np.float32(ROPE_BASE) ** (-np.arange(n_freq, dtype=np.float32) / np.float32(n_freq))
    ang_r = ((tok // GRID_W).astype(np.float32)[:, None] * inv_freq).astype(np.float64)
    ang_c = ((tok % GRID_W).astype(np.float32)[:, None] * inv_freq).astype(np.float64)
    cos = np.concatenate([np.cos(ang_r)] * 2 + [np.cos(ang_c)] * 2, axis=-1)
    sin = np.concatenate([-np.sin(ang_r), np.sin(ang_r), -np.sin(ang_c), np.sin(ang_c)], axis=-1)
    reps = LANES // QK_DIM
    return (jnp.asarray(np.tile(cos, (1, reps)), dtype=F32), jnp.asarray(np.tile(sin, (1, reps)), dtype=F32))


def _dft_angles(rows, n):
    return 2.0 * np.pi * ((np.asarray(rows, np.int64)[:, None] * np.arange(n, dtype=np.int64)[None, :]) % n) / n


def _dft_tables_small(n):
    ang = _dft_angles(np.arange(n), n)
    return np.cos(ang).astype(np.float32), np.sin(ang).astype(np.float32)


def _dft_tables(n):
    rows = n // 2
    if n <= 4 * DFT_LO:
        return [jnp.asarray(t[:rows]).astype(BF16) for t in _dft_tables_small(n)]
    hi = rows // DFT_LO
    ang_hi = _dft_angles(np.arange(hi) * DFT_LO, n)
    ang_lo = _dft_angles(np.arange(DFT_LO), n)
    ch, sh = (jnp.asarray(f(ang_hi), dtype=F32)[:, None, :] for f in (np.cos, np.sin))
    cl, sl = (jnp.asarray(f(ang_lo), dtype=F32)[None, :, :] for f in (np.cos, np.sin))
    return [(ch * cl - sh * sl).reshape(rows, n).astype(BF16), (sh * cl + ch * sl).reshape(rows, n).astype(BF16)]


def _block_diag(blocks):
    g, r, c = blocks.shape[-3:]
    eye = jnp.eye(g, dtype=blocks.dtype)
    out = eye[:, None, :, None] * blocks[..., :, :, None, :]
    return out.reshape(blocks.shape[:-3] + (g * r, g * c))


def _prep_inproj_kernel(w_ref, perm_ref, o_ref):
    wb = w_ref[...].astype(BF16)
    nqk = perm_ref.shape[0]
    o_ref[:, 0:nqk] = _dot(wb[:, 0:nqk], perm_ref[...]).astype(BF16)
    o_ref[:, nqk:] = wb[:, nqk:]


def _prep_inproj(w_in, qk_w):
    depth, d, n = w_in.shape
    nqk = 4 * qk_w
    dst = np.arange(nqk)
    pair, head, mp, dim = dst // (2 * qk_w), dst % (2 * qk_w) // LANES, dst % LANES // QK_DIM, dst % QK_DIM
    src = ((2 * pair + mp) * HEADS + head) * QK_DIM + dim
    perm = (jnp.arange(nqk)[:, None] == jnp.asarray(src)[None, :]).astype(BF16)
    w_spec = pl.BlockSpec((None, TM, n), lambda l, j: (l, j, 0))
    return pl.pallas_call(
        _prep_inproj_kernel,
        grid=(depth, d // TM),
        in_specs=[w_spec, _resident(perm)],
        out_specs=w_spec,
        out_shape=jax.ShapeDtypeStruct(w_in.shape, BF16),
        compiler_params=_params(
            ("arbitrary", "arbitrary"),
            resident=_nbytes(perm.shape, BF16),
            streamed=_nbytes((TM, n), F32) + _nbytes((TM, n), BF16),
            temporaries=_nbytes((TM, n), BF16) + _nbytes((TM, nqk), F32)),
        name="prep_inproj",
    )(w_in, perm)


def kernel(x, c, ctx, c_ctx, w_ada, b_ada, norm1_g, norm2_g, w_in, lam_q1, lam_k1, lam_q2, lam_k2, subln_g,
           w_fourier, conv_w, conv_b, conv_ln_g, conv_ln_b, w_conv_out, w_out, w_ffn1, w_ffn3, w_ffn2, final_g):
    b, n_lat, d = x.shape
    n_ctx = ctx.shape[1]
    depth = w_ada.shape[0]
    fw = w_fourier.shape[1] * w_fourier.shape[2]
    cw = conv_w.shape[-1]
    qk_w = HEADS * QK_DIM
    attn_w = HEADS * V_DIM
    assert n_lat % GRID_W == 0 and n_ctx % TM == 0
    assert w_in.shape[-1] == 4 * qk_w + attn_w + fw + 2 * cw

    pad = (-(b + 1)) % SUBLANES
    c_rows = jnp.concatenate([c, c_ctx[None, :], jnp.zeros((pad, d), c.dtype)], axis=0)
    mod = _ada(c_rows, w_ada, b_ada)
    mod = mod.reshape(depth, mod.shape[1], 6, d)
    ctx_row = b

    rope = _rope_tables(n_lat)
    cc, cs = _dft_tables_small(fw // FOURIER_GROUPS)
    eye = np.eye(FOURIER_GROUPS, dtype=np.float32)
    chan_c = jnp.asarray(np.kron(eye, cc)).astype(BF16)
    chan_s = jnp.asarray(np.kron(eye, cs)).astype(BF16)
    lat_tables = _dft_tables(n_lat)
    ctx_tables = _dft_tables(n_ctx)
    group = cw // CONV_GROUPS
    avg = jnp.asarray(np.kron(np.eye(CONV_GROUPS), np.full((group, group), 1.0 / group)), dtype=F32).astype(BF16)

    w_in_b = _prep_inproj(w_in, qk_w)
    wf_bd = _block_diag(w_fourier).astype(BF16)
    w_pw_b = w_conv_out.astype(BF16)
    w_out_b = w1_b = w3_b = w2_b = None

    def per_layer_rows(a):
        return a.reshape(depth, 1, a.shape[-1])

    norm1, norm2, subln = per_layer_rows(norm1_g), per_layer_rows(norm2_g), per_layer_rows(subln_g)
    lam_vecs = [per_layer_rows(a) for a in (lam_q1, lam_k1, lam_q2, lam_k2)]
    cb, lg, lb = per_layer_rows(conv_b), per_layer_rows(conv_ln_g), per_layer_rows(conv_ln_b)
    fg = final_g[None, :]

    def mixers(u_f, u_c, tables, l):
        yf = _fourier(u_f, chan_c, chan_s, wf_bd, l, tables)
        yc = _conv(u_c, conv_w, cb, lg, lb, avg, w_pw_b, l)
        return yf, yc

    ctx_flat = ctx.reshape(1, b * n_ctx, d)
    for l in range(depth):
        last = l == depth - 1
        lam_init = 0.8 - 0.6 * math.exp(-0.3 * l)
        q, k, vt, kn, uf, uc = _inproj(x, mod, norm1, w_in_b, rope, l, None, fw, 2 * cw)
        yf, yc = mixers(uf, uc, lat_tables, l)
        cast = (w_out, w_ffn1, w_ffn3, w_ffn2) if l == 0 else ()
        if last:
            kv_ctx = _inproj(ctx_flat, mod, norm1, w_in_b, None, l, ctx_row, fw, 2 * cw, kv_only=True)
            (o,), casted = _attention(kv_ctx, (q, k, vt, kn), lam_vecs, subln, l, lam_init, False, cast)
        else:
            qc, kc, vtc, knc, ufc, ucc = _inproj(ctx_flat, mod, norm1, w_in_b, None, l, ctx_row, fw, 2 * cw)
            (oc, o), casted = _attention((qc, kc, vtc, knc), (q, k, vt, kn), lam_vecs, subln, l, lam_init, True, cast)
        if cast:
            w_out_b, w1_b, w3_b, w2_b = casted
        if not last:
            yfc, ycc = mixers(ufc.reshape(b, n_ctx, fw), ucc.reshape(b, n_ctx, 2 * cw), ctx_tables, l)
            ctx_flat = _out_ffn(ctx_flat, oc.reshape(1, b * n_ctx, attn_w), yfc.reshape(1, b * n_ctx, fw),
                                ycc.reshape(1, b * n_ctx, cw), mod, norm2, w_out_b, w1_b, w3_b, w2_b, fg, l,
                                ctx_row, final=False)
        x = _out_ffn(x, o, yf, yc, mod, norm2, w_out_b, w1_b, w3_b, w2_b, fg, l, None, final=last)
    return x
```

```python
import functools
import math

import numpy as np
import jax
import jax.numpy as jnp
from jax import lax
from jax.experimental import pallas as pl
from jax.experimental.pallas import tpu as pltpu

F32 = jnp.float32
BF16 = jnp.bfloat16

GRID_W = 64
HEADS = 4
QK_DIM = 64
V_DIM = 2 * QK_DIM
FOURIER_GROUPS = 4
CONV_GROUPS = 4
CONV_K = 31
ROPE_BASE = 10000.0
EPS = 1e-6
LOG2_E = math.log2(math.e)
BOUND_SLACK = 1.0 + 2.0 ** -6
MIN_SOFTMAX_DENOMINATOR = 2.0 ** -90

LANES = 128
SUBLANES = 8
TM = 256
ROW_SUBTILES = 2
ATT_KC = 256
CONV_HALO = 16
CONV_CHUNK = 128
ADA_TN = 1024
DFT_LO = 64
V7X_VMEM_BYTES = 64 * 1024 * 1024
VMEM_REQUEST_BYTES = V7X_VMEM_BYTES * 15 // 16


def _nbytes(shape, dtype):
    return math.prod(shape) * jnp.dtype(dtype).itemsize


def _block_bytes(specs, arrays):
    return sum(_nbytes([s for s in spec.block_shape if s is not None], arr.dtype) for spec, arr in zip(specs, arrays))


def _params(sem, *, resident=0, streamed=0, scratch=0, temporaries=0):
    need = resident + 2 * streamed + scratch + temporaries
    assert need <= VMEM_REQUEST_BYTES, f"VMEM estimate {need} exceeds the request {VMEM_REQUEST_BYTES}"
    return pltpu.CompilerParams(dimension_semantics=sem, vmem_limit_bytes=VMEM_REQUEST_BYTES)


def _resident(arr, layer=None, depth=None):
    if layer is None:
        idx = (0,) * arr.ndim
        return pl.BlockSpec(arr.shape, lambda *_: idx, pipeline_mode=pl.Buffered(1))
    idx = (layer,) + (0,) * (arr.ndim - 1)
    if depth is not None:
        return pl.BlockSpec((arr.shape[0] // depth, arr.shape[1]), lambda *_: idx, pipeline_mode=pl.Buffered(1))
    return pl.BlockSpec((None,) + arr.shape[1:], lambda *_: idx, pipeline_mode=pl.Buffered(1))


def _segment_spec(arr, n):
    if arr.shape[0] == 1:
        return pl.BlockSpec((None, n, arr.shape[-1]), lambda i: (0, i, 0))
    return pl.BlockSpec((None, n, arr.shape[-1]), lambda i: (i, 0, 0))


def _mod_spec(mod, layer, fixed_row):
    if fixed_row is None:
        return pl.BlockSpec((None, None) + mod.shape[2:], lambda i, j: (layer, i, 0, 0))
    return pl.BlockSpec((None, None) + mod.shape[2:], lambda i, j: (layer, fixed_row, 0, 0))


def _cast_slabs(weights, grid):
    steps = grid[0] * grid[1]
    operands, specs, shapes = [], [], []
    for wt in weights:
        flat = wt.reshape(-1, wt.shape[-1])
        slab = flat.shape[0] // steps
        assert slab * steps == flat.shape[0] and slab % (2 * SUBLANES) == 0
        operands.append(flat)
        specs.append(pl.BlockSpec((slab, flat.shape[1]), lambda i, j: (i * grid[1] + j, 0)))
        shapes.append(jax.ShapeDtypeStruct(flat.shape, BF16))
    return operands, specs, shapes


def _sigmoid(x):
    return 1.0 / (1.0 + jnp.exp(-x))


def _rms(x):
    return x * lax.rsqrt(jnp.mean(x * x, axis=-1, keepdims=True) + EPS)


def _dot(a, b):
    return jnp.dot(a, b, preferred_element_type=F32)


def _ada_kernel(c_ref, w_ref, b_ref, o_ref):
    c = c_ref[...]
    s = (c * _sigmoid(c)).astype(BF16)
    o_ref[...] = _dot(s, w_ref[...].astype(BF16)) + b_ref[...]


def _ada(c_rows, w_ada, b_ada):
    depth, d, n = w_ada.shape
    rows = c_rows.shape[0]
    return pl.pallas_call(
        _ada_kernel,
        grid=(depth, n // ADA_TN),
        in_specs=[
            pl.BlockSpec((rows, d), lambda l, j: (0, 0)),
            pl.BlockSpec((None, d, ADA_TN), lambda l, j: (l, 0, j)),
            pl.BlockSpec((None, 1, ADA_TN), lambda l, j: (l, 0, j)),
        ],
        out_specs=pl.BlockSpec((None, rows, ADA_TN), lambda l, j: (l, 0, j)),
        out_shape=jax.ShapeDtypeStruct((depth, rows, n), F32),
        compiler_params=_params(
            ("arbitrary", "arbitrary"),
            resident=_nbytes(c_rows.shape, F32),
            streamed=_nbytes((d + 1 + rows, ADA_TN), F32),
            temporaries=_nbytes((d, ADA_TN), BF16) + _nbytes((rows, ADA_TN), F32)),
        name="ada",
    )(c_rows, w_ada, b_ada.reshape(depth, 1, n))


def _inproj_kernel(*refs, use_rope, kv_only, n_sub):
    x_ref, mod_ref, g_ref, w_ref = refs[:4]
    refs = refs[4:]
    if use_rope:
        cos_ref, sin_ref = refs[:2]
        refs = refs[2:]
    hw = HEADS * LANES
    lane = lax.broadcasted_iota(jnp.int32, (TM, LANES), 1)
    first_half = (lane & (QK_DIM // 4)) == 0
    row = lax.broadcasted_iota(jnp.int32, (LANES, TM), 0)
    zero = jnp.zeros((LANES, TM), F32)
    dim = lax.broadcasted_iota(jnp.int32, (LANES, LANES), 0)
    col = lax.broadcasted_iota(jnp.int32, (LANES, LANES), 1)
    map_sum = jnp.where(col == jnp.where(dim < QK_DIM, 0, 1), 1.0, 0.0).astype(BF16)

    for u in range(n_sub):
        rows = slice(u * TM, (u + 1) * TM)
        h = (_rms(x_ref[rows, :]) * g_ref[...] * (1.0 + mod_ref[1:2, :]) + mod_ref[0:1, :]).astype(BF16)

        def rope(t):
            if not use_rope:
                return t
            partner = jnp.where(first_half, pltpu.roll(t, LANES - QK_DIM // 4, 1), pltpu.roll(t, QK_DIM // 4, 1))
            return t * cos_ref[rows, :] + partner * sin_ref[rows, :]

        def put_keys(j, kf):
            k_ref[j, rows, :] = kf.astype(BF16)
            norms = _dot((kf * kf).astype(BF16), map_sum)
            kn_ref[j, u] = jnp.max(norms.reshape(TM // SUBLANES, SUBLANES, LANES), axis=0)

        if kv_only:
            k_ref, vt_ref, kn_ref = refs
            r = _dot(h, w_ref[:, hw:3 * hw])
            for j in range(HEADS):
                put_keys(j, rope(r[:, j * LANES:(j + 1) * LANES]))
                vt_ref[j, :, rows] = r[:, hw + j * LANES:hw + (j + 1) * LANES].T.astype(BF16)
            continue

        q_ref, k_ref, vt_ref, kn_ref, uf_ref, uc_ref = refs
        r = _dot(h, w_ref[...])
        for j in range(HEADS):
            qt = (rope(r[:, j * LANES:(j + 1) * LANES]) * (QK_DIM ** -0.5 * LOG2_E)).T
            for mp in range(2):
                sel = (row < QK_DIM) == (mp == 0)
                qm = qt[mp * QK_DIM:(mp + 1) * QK_DIM, :]
                qn = jnp.sqrt(jnp.sum(qm * qm, axis=0, keepdims=True)) * BOUND_SLACK
                q_ref[j, u, mp, 0:LANES, :] = jnp.where(sel, qt, zero).astype(BF16)
                q_ref[j, u, mp, LANES:2 * LANES, :] = jnp.where(row == 0, qn, zero).astype(BF16)
            put_keys(j, rope(r[:, hw + j * LANES:hw + (j + 1) * LANES]))
            vt_ref[j, :, rows] = r[:, 2 * hw + j * LANES:2 * hw + (j + 1) * LANES].T.astype(BF16)
        fw = uf_ref.shape[-1]
        uf_ref[rows, :] = r[:, 3 * hw:3 * hw + fw].astype(BF16)
        uc_ref[rows, :] = r[:, 3 * hw + fw:]


def _inproj(x, mod, g, w, rope_tables, layer, fixed_row, fw, cw2, kv_only=False):
    b, n, d = x.shape
    n_sub = ROW_SUBTILES
    bm = n_sub * TM
    assert n % bm == 0
    use_rope = rope_tables is not None
    in_specs = [
        pl.BlockSpec((None, bm, d), lambda i, j: (i, j, 0)),
        _mod_spec(mod, layer, fixed_row),
        _resident(g, layer),
        _resident(w, layer),
    ]
    operands = [x, mod, g, w]
    if use_rope:
        in_specs += [pl.BlockSpec((bm, LANES), lambda i, j: (j, 0))] * 2
        operands += list(rope_tables)
    out_specs = [
        pl.BlockSpec((None, HEADS, n_sub, 2, 2 * LANES, TM), lambda i, j: (i, 0, j, 0, 0, 0)),
        pl.BlockSpec((None, HEADS, bm, LANES), lambda i, j: (i, 0, j, 0)),
        pl.BlockSpec((None, HEADS, V_DIM, bm), lambda i, j: (i, 0, 0, j)),
        pl.BlockSpec((None, HEADS, n_sub, SUBLANES, LANES), lambda i, j: (i, 0, j, 0, 0)),
        pl.BlockSpec((None, bm, fw), lambda i, j: (i, j, 0)),
        pl.BlockSpec((None, bm, cw2), lambda i, j: (i, j, 0)),
    ]
    out_shape = [
        jax.ShapeDtypeStruct((b, HEADS, n // TM, 2, 2 * LANES, TM), BF16),
        jax.ShapeDtypeStruct((b, HEADS, n, LANES), BF16),
        jax.ShapeDtypeStruct((b, HEADS, V_DIM, n), BF16),
        jax.ShapeDtypeStruct((b, HEADS, n // TM, SUBLANES, LANES), F32),
        jax.ShapeDtypeStruct((b, n, fw), BF16),
        jax.ShapeDtypeStruct((b, n, cw2), F32),
    ]
    if kv_only:
        out_specs, out_shape = out_specs[1:4], out_shape[1:4]
    return pl.pallas_call(
        functools.partial(_inproj_kernel, use_rope=use_rope, kv_only=kv_only, n_sub=n_sub),
        grid=(b, n // bm),
        in_specs=in_specs,
        out_specs=out_specs,
        out_shape=out_shape,
        compiler_params=_params(
            ("arbitrary", "arbitrary"),
            resident=_block_bytes(in_specs[2:4], operands[2:4]),
            streamed=_block_bytes(in_specs[:2] + in_specs[4:] + out_specs, operands[:2] + operands[4:] + out_shape),
            temporaries=2 * n_sub * _nbytes((TM, w.shape[-1]), F32)),
        name="inproj",
    )(*operands)


def _attn_kernel(*refs, lam_init, with_ctx, bounded, n_cast):
    lq1_ref, lk1_ref, lq2_ref, lk2_ref, g_ref, kc_ref, vtc_ref, knc_ref, ql_ref, kl_ref, vtl_ref, knl_ref = refs[:12]
    refs = list(refs[12:])
    s_ref = refs.pop()
    qc_ref = refs.pop(0) if with_ctx else None
    cast_src = [refs.pop(0) for _ in range(n_cast)]
    oc_ref = refs.pop(0) if with_ctx else None
    ol_ref = refs.pop(0)
    lmin_ref = refs.pop(0) if bounded else None
    for src_ref, dst_ref in zip(cast_src, refs):
        dst_ref[...] = src_ref[...].astype(BF16)
    lam = (jnp.exp(jnp.sum(lq1_ref[...] * lk1_ref[...], axis=-1, keepdims=True))
           - jnp.exp(jnp.sum(lq2_ref[...] * lk2_ref[...], axis=-1, keepdims=True)) + lam_init)

    def fold(x):
        return x.reshape(x.shape[0] // SUBLANES, SUBLANES, x.shape[1])

    def finish(o_ref, t, ot, l1):
        ot = ot * (1.0 / l1)
        ms = jnp.mean(ot * ot, axis=0, keepdims=True)
        on = (ot * lax.rsqrt(ms + EPS)).T * g_ref[...] * (1.0 - lam_init)
        o_ref[t * TM:(t + 1) * TM, :] = on.astype(BF16)

    def key_chunks(key_refs):
        return [(k_ref, vt_ref, c) for k_ref, vt_ref, _ in key_refs for c in range(0, k_ref.shape[0], ATT_KC)]

    def pipeline_bounded(q_ref, o_ref, key_refs):
        chunks = key_chunks(key_refs)
        n_tiles = q_ref.shape[0]
        lane = lax.broadcasted_iota(jnp.int32, (ATT_KC, LANES), 1)
        key_cols = []
        ksq = functools.reduce(jnp.maximum, [jnp.max(kn_ref[...], axis=0) for _, _, kn_ref in key_refs])
        kb = jnp.sqrt(jnp.max(ksq, axis=0, keepdims=True)) * BOUND_SLACK
        for mp in range(2):
            key_cols.append(jnp.where(lane == 0, -kb[:, mp:mp + 1], 0.0).astype(BF16))
        sums = {}
        lmin = None
        for u in range(n_tiles + 1):
            ta, tc = u, u - 1
            do_a, do_c = ta < n_tiles, 0 <= tc < n_tiles
            if do_a:
                qts = [q_ref[ta, mp] for mp in range(2)]
                l8 = [jnp.zeros((SUBLANES, TM), F32) for _ in range(2)]
            if do_c:
                lc = sums.pop(tc)
                rho = lam * lc[0] / lc[1]
                ot = None
            for ci, (k_ref, vt_ref, c) in enumerate(chunks):
                rows = slice(ci * ATT_KC, (ci + 1) * ATT_KC)
                if do_a:
                    for mp in range(2):
                        keys = jnp.concatenate([k_ref[c:c + ATT_KC, :], key_cols[mp]], axis=1)
                        e = jnp.exp2(_dot(keys, qts[mp]))
                        s_ref[ta % 2, mp, rows, :] = e
                        l8[mp] = l8[mp] + jnp.sum(fold(e), axis=0)
                if do_c:
                    p = (s_ref[tc % 2, 0, rows, :] - rho * s_ref[tc % 2, 1, rows, :]).astype(BF16)
                    part = _dot(vt_ref[:, c:c + ATT_KC], p)
                    ot = part if ot is None else ot + part
            if do_a:
                sums[ta] = [jnp.sum(l, axis=0, keepdims=True) for l in l8]
                low = jnp.minimum(sums[ta][0], sums[ta][1])
                lmin = low if lmin is None else jnp.minimum(lmin, low)
            if do_c:
                finish(o_ref, tc, ot, lc[0])
        return jnp.min(lmin, axis=1, keepdims=True)

    def pipeline(q_ref, o_ref, key_refs):
        chunks = key_chunks(key_refs)
        n_tiles = q_ref.shape[0]
        maxima, sums = {}, {}
        for u in range(n_tiles + 2):
            ta, tb, tc = u, u - 1, u - 2
            do_a, do_b, do_c = ta < n_tiles, 0 <= tb < n_tiles, 0 <= tc < n_tiles
            if do_a:
                qts = [q_ref[ta, mp, 0:LANES, :] for mp in range(2)]
                m8 = [None, None]
            if do_b:
                mb = maxima.pop(tb)
                l8 = [jnp.zeros((SUBLANES, TM), F32) for _ in range(2)]
            if do_c:
                lc = sums.pop(tc)
                rho = lam * lc[0] / lc[1]
                ot = None
            for ci, (k_ref, vt_ref, c) in enumerate(chunks):
                rows = slice(ci * ATT_KC, (ci + 1) * ATT_KC)
                if do_a:
                    for mp in range(2):
                        s = _dot(k_ref[c:c + ATT_KC, :], qts[mp])
                        s_ref[ta % 3, mp, rows, :] = s
                        cm = jnp.max(fold(s), axis=0)
                        m8[mp] = cm if m8[mp] is None else jnp.maximum(m8[mp], cm)
                if do_b:
                    for mp in range(2):
                        e = jnp.exp2(s_ref[tb % 3, mp, rows, :] - mb[mp])
                        s_ref[tb % 3, mp, rows, :] = e
                        l8[mp] = l8[mp] + jnp.sum(fold(e), axis=0)
                if do_c:
                    p = (s_ref[tc % 3, 0, rows, :] - rho * s_ref[tc % 3, 1, rows, :]).astype(BF16)
                    part = _dot(vt_ref[:, c:c + ATT_KC], p)
                    ot = part if ot is None else ot + part
            if do_a:
                maxima[ta] = [jnp.max(m, axis=0, keepdims=True) for m in m8]
            if do_b:
                sums[tb] = [jnp.sum(l, axis=0, keepdims=True) for l in l8]
            if do_c:
                finish(o_ref, tc, ot, lc[0])

    run = pipeline_bounded if bounded else pipeline
    ctx_keys, lat_keys = (kc_ref, vtc_ref, knc_ref), (kl_ref, vtl_ref, knl_ref)
    lmin = run(ql_ref, ol_ref, [ctx_keys, lat_keys])
    if with_ctx:
        lmin_ctx = run(qc_ref, oc_ref, [ctx_keys])
    if bounded:
        if with_ctx:
            lmin = jnp.minimum(lmin, lmin_ctx)
        lmin_ref[...] = jnp.broadcast_to(lmin, lmin_ref.shape)


def _attention_call(qkv_ctx, qkv_lat, lam_vecs, g, layer, lam_init, with_ctx, bounded, cast=()):
    q_lat, k_lat, vt_lat, kn_lat = qkv_lat
    k_ctx, vt_ctx, kn_ctx = qkv_ctx[-3:]
    b, h, n_lat, _ = k_lat.shape
    n_ctx = k_ctx.shape[2] // b
    lat_tiles, ctx_tiles = n_lat // TM, n_ctx // TM

    def tiles_spec(tiles, tail, ctx):
        zeros = (0,) * len(tail)
        if ctx:
            return pl.BlockSpec((None, None, tiles) + tail, lambda i, j: (0, j, i) + zeros)
        return pl.BlockSpec((None, None, tiles) + tail, lambda i, j: (i, j, 0) + zeros)

    q_tile, kn_tile = (2, 2 * LANES, TM), (SUBLANES, LANES)
    in_specs = [_resident(v, layer) for v in lam_vecs] + [
        _resident(g, layer),
        pl.BlockSpec((None, None, n_ctx, LANES), lambda i, j: (0, j, i, 0)),
        pl.BlockSpec((None, None, V_DIM, n_ctx), lambda i, j: (0, j, 0, i)),
        tiles_spec(ctx_tiles, kn_tile, True),
        tiles_spec(lat_tiles, q_tile, False),
        pl.BlockSpec((None, None, n_lat, LANES), lambda i, j: (i, j, 0, 0)),
        pl.BlockSpec((None, None, V_DIM, n_lat), lambda i, j: (i, j, 0, 0)),
        tiles_spec(lat_tiles, kn_tile, False),
    ]
    operands = list(lam_vecs) + [g, k_ctx, vt_ctx, kn_ctx, q_lat, k_lat, vt_lat, kn_lat]
    out_specs = [pl.BlockSpec((None, n_lat, V_DIM), lambda i, j: (i, 0, j))]
    out_shape = [jax.ShapeDtypeStruct((b, n_lat, h * V_DIM), BF16)]
    if with_ctx:
        in_specs.append(tiles_spec(ctx_tiles, q_tile, True))
        operands.append(qkv_ctx[0])
        out_specs.insert(0, pl.BlockSpec((None, n_ctx, V_DIM), lambda i, j: (0, i, j)))
        out_shape.insert(0, jax.ShapeDtypeStruct((1, b * n_ctx, h * V_DIM), BF16))
    if bounded:
        out_specs.append(pl.BlockSpec((None, None, SUBLANES, LANES), lambda i, j: (i, j, 0, 0)))
        out_shape.append(jax.ShapeDtypeStruct((b, h, SUBLANES, LANES), F32))
    cast_operands, cast_specs, cast_shapes = _cast_slabs(cast, (b, h))
    in_specs, operands = in_specs + cast_specs, operands + cast_operands
    scores_shape = (2 if bounded else 3, 2, n_ctx + n_lat, TM)
    outs = pl.pallas_call(
        functools.partial(_attn_kernel, lam_init=lam_init, with_ctx=with_ctx, bounded=bounded, n_cast=len(cast)),
        grid=(b, h),
        in_specs=in_specs,
        out_specs=out_specs + cast_specs,
        out_shape=out_shape + cast_shapes,
        scratch_shapes=[pltpu.VMEM(scores_shape, F32)],
        compiler_params=_params(
            ("arbitrary", "arbitrary"),
            streamed=_block_bytes(in_specs + out_specs + cast_specs, operands + out_shape + cast_shapes),
            scratch=_nbytes(scores_shape, F32),
            temporaries=8 * _nbytes((ATT_KC, TM), F32)),
        name="attention_bounded" if bounded else "attention_exact",
    )(*operands)
    n_main = len(out_shape)
    return list(outs[:n_main]), list(outs[n_main:])


def _attention(qkv_ctx, qkv_lat, lam_vecs, g, layer, lam_init, with_ctx, cast=()):
    args = (qkv_ctx, qkv_lat, lam_vecs, g, layer, lam_init, with_ctx)
    (*outs, lmin), casted = _attention_call(*args, bounded=True, cast=cast)
    safe = jnp.min(lmin) >= MIN_SOFTMAX_DENOMINATOR
    outs = lax.cond(safe, lambda: tuple(outs), lambda: tuple(_attention_call(*args, bounded=False)[0]))
    return list(outs), casted


def _fourier_kernel(u_ref, chan_c_ref, chan_s_ref, wf_ref, pos_c_ref, pos_s_ref, mid_ref, flip_ref, y_ref):
    n, width = u_ref.shape
    half = n // 2
    scale = (n * width // FOURIER_GROUPS) ** -0.5
    u = u_ref[...]
    a = _dot(pos_c_ref[...], u).astype(BF16)
    b = _dot(pos_s_ref[...], u).astype(BF16)
    p = _dot(a, chan_c_ref[...])
    q = _dot(b, chan_s_ref[...])
    y_ref[0:half, :] = _dot(((p - q) * scale).astype(BF16), wf_ref[...]).astype(BF16)
    mirrored = _dot(flip_ref[...], ((p + q) * scale).astype(BF16))
    mid = _dot(_dot(mid_ref[...], u).astype(BF16), chan_c_ref[...]) * scale
    rows = mid.shape[0]
    upper = jnp.concatenate([mirrored[0:rows] + mid, mirrored[rows:]], axis=0)
    y_ref[half:, :] = _dot(upper.astype(BF16), wf_ref[...]).astype(BF16)


def _fourier(uf, n, chan_c, chan_s, wf_bd, layer, pos_tables):
    fw = uf.shape[-1]
    half = n // 2
    idx = jnp.arange(half)
    flip = (idx[:, None] + idx[None, :] == half).astype(BF16)
    mid = np.zeros((SUBLANES, n), np.float32)
    mid[0] = 1.0 - 2.0 * (np.arange(n) % 2)
    mid = jnp.asarray(mid).astype(BF16)
    row_spec = _segment_spec(uf, n)
    operands = [uf, chan_c, chan_s, wf_bd, *pos_tables, mid, flip]
    in_specs = [row_spec, _resident(chan_c), _resident(chan_s), _resident(wf_bd, layer)] + [
        _resident(t) for t in (*pos_tables, mid, flip)]
    return pl.pallas_call(
        _fourier_kernel,
        grid=(uf.shape[0] * uf.shape[1] // n,),
        in_specs=in_specs,
        out_specs=row_spec,
        out_shape=jax.ShapeDtypeStruct(uf.shape, BF16),
        compiler_params=_params(
            ("arbitrary",),
            resident=_block_bytes(in_specs[1:], operands[1:]),
            streamed=2 * _nbytes((n, fw), BF16),
            temporaries=8 * _nbytes((half, fw), F32)),
        name="fourier",
    )(*operands)


def _split_dot(x, m):
    hi = x.astype(BF16)
    lo = (x - hi.astype(F32)).astype(BF16)
    return _dot(hi, m) + _dot(lo, m)


def _conv_kernel(u_ref, w_ref, b_ref, lg_ref, lb_ref, avg_ref, pw_ref, y_ref, zpad_ref, acc_a_ref, acc_b_ref):
    n, cw = y_ref.shape
    halo = jnp.zeros((CONV_HALO, cw), F32)

    def taps(i, acc_ref, slot):
        r0 = pl.multiple_of(i * CONV_CHUNK, CONV_CHUNK)
        for lo in range(0, cw, LANES):
            win = zpad_ref[pl.ds(r0, CONV_CHUNK + 2 * CONV_HALO), lo:lo + LANES]
            part = jnp.zeros((CONV_CHUNK, LANES), F32) + b_ref[:, lo:lo + LANES]
            for shift in range(SUBLANES):
                rolled = win if shift == 0 else pltpu.roll(win, win.shape[0] - shift, 0)
                for aligned in range(0, 2 * CONV_HALO, SUBLANES):
                    tap = aligned + shift - (CONV_HALO - CONV_K // 2)
                    if 0 <= tap < CONV_K:
                        part = part + rolled[aligned:aligned + CONV_CHUNK] * w_ref[tap:tap + 1, lo:lo + LANES]
            acc_ref[slot, :, lo:lo + LANES] = part

    def project(acc_ref, slot):
        acc = acc_ref[slot]
        mu = _split_dot(acc, avg_ref[...])
        dev = acc - mu
        var = _split_dot(dev * dev, avg_ref[...])
        zn = dev * lax.rsqrt(var + EPS) * lg_ref[...] + lb_ref[...]
        act = (zn * _sigmoid(zn)).astype(BF16)
        return _dot(act, pw_ref[...]).astype(BF16)

    def round_(first_tap, tap_ref, ready_ref):
        if first_tap is not None:
            taps(first_tap, tap_ref, 0)
            taps(first_tap + 1, tap_ref, 1)
        return [project(ready_ref, 0), project(ready_ref, 1)]

    def store(first_chunk, ys):
        r0 = pl.multiple_of(first_chunk * CONV_CHUNK, 2 * CONV_CHUNK)
        y_ref[pl.ds(r0, len(ys) * CONV_CHUNK), :] = jnp.concatenate(ys, axis=0)

    zpad_ref[0:CONV_HALO, :] = halo
    zpad_ref[CONV_HALO:CONV_HALO + n, :] = u_ref[:, 0:cw] * _sigmoid(u_ref[:, cw:2 * cw])
    zpad_ref[CONV_HALO + n:2 * CONV_HALO + n, :] = halo

    chunks = n // CONV_CHUNK
    assert chunks == 2 or chunks % 4 == 0

    def step(j, carry):
        c0 = 4 * j
        ys = round_(c0 + 2, acc_b_ref, acc_a_ref) + round_(c0 + 4, acc_a_ref, acc_b_ref)
        store(c0, ys)
        return carry

    taps(0, acc_a_ref, 0)
    taps(1, acc_a_ref, 1)
    if chunks == 2:
        store(0, round_(None, None, acc_a_ref))
    else:
        lax.fori_loop(0, chunks // 4 - 1, step, 0)
        store(chunks - 4, round_(chunks - 2, acc_b_ref, acc_a_ref) + round_(None, None, acc_b_ref))


def _conv(uc, n, conv_w, conv_b, ln_g, ln_b, avg, w_pw, layer):
    cw2 = uc.shape[-1]
    cw = cw2 // 2
    y_shape = jax.ShapeDtypeStruct(uc.shape[:2] + (cw,), BF16)
    operands = [uc, conv_w, conv_b, ln_g, ln_b, avg, w_pw]
    in_specs = [
        _segment_spec(uc, n),
        _resident(conv_w, layer), _resident(conv_b, layer), _resident(ln_g, layer), _resident(ln_b, layer),
        _resident(avg), _resident(w_pw, layer),
    ]
    scratch = [(n + 2 * CONV_HALO, cw), (2, CONV_CHUNK, cw), (2, CONV_CHUNK, cw)]
    return pl.pallas_call(
        _conv_kernel,
        grid=(uc.shape[0] * uc.shape[1] // n,),
        in_specs=in_specs,
        out_specs=_segment_spec(y_shape, n),
        out_shape=y_shape,
        scratch_shapes=[pltpu.VMEM(s, F32) for s in scratch],
        compiler_params=_params(
            ("arbitrary",),
            resident=_block_bytes(in_specs[1:], operands[1:]),
            streamed=_nbytes((n, cw2), F32) + _nbytes((n, cw), BF16),
            scratch=sum(_nbytes(s, F32) for s in scratch),
            temporaries=3 * _nbytes((n, cw), F32) + 16 * _nbytes((CONV_CHUNK, cw), F32)),
        name="conv",
    )(*operands)


def _ffn_kernel(x_ref, o_ref, yf_ref, yc_ref, mod_ref, g_ref, wout_ref, w1_ref, w3_ref, w2_ref, fg_ref, out_ref,
                *, final, n_sub):
    aw = o_ref.shape[-1]
    fw = yf_ref.shape[-1]
    for u in range(n_sub):
        rows = slice(u * TM, (u + 1) * TM)
        y = (_dot(o_ref[rows, :], wout_ref[0:aw, :]) + _dot(yf_ref[rows, :], wout_ref[aw:aw + fw, :])
             + _dot(yc_ref[rows, :], wout_ref[aw + fw:, :]))
        x1 = x_ref[rows, :] + mod_ref[2:3, :] * y
        h = (_rms(x1) * g_ref[...] * (1.0 + mod_ref[4:5, :]) + mod_ref[3:4, :]).astype(BF16)
        a = _dot(h, w1_ref[...])
        gated = (a * _sigmoid(a) * _dot(h, w3_ref[...])).astype(BF16)
        x2 = x1 + mod_ref[5:6, :] * _dot(gated, w2_ref[...])
        if final:
            x2 = _rms(x2) * fg_ref[...]
        out_ref[rows, :] = x2


def _out_ffn(x, o, yf, yc, mod, g2, w_out, w1, w3, w2, final_g, layer, fixed_row, final):
    b, n, d = x.shape
    depth = g2.shape[0]
    n_sub = ROW_SUBTILES
    bm = n_sub * TM
    assert n % bm == 0

    def rows(width):
        return pl.BlockSpec((None, bm, width), lambda i, j: (i, j, 0))

    operands = [x, o, yf, yc, mod, g2, w_out, w1, w3, w2, final_g]
    in_specs = [
        rows(d), rows(o.shape[-1]), rows(yf.shape[-1]), rows(yc.shape[-1]),
        _mod_spec(mod, layer, fixed_row),
        _resident(g2, layer), _resident(w_out, layer, depth), _resident(w1, layer, depth),
        _resident(w3, layer, depth), _resident(w2, layer, depth), _resident(final_g),
    ]
    d_ff = w1.shape[-1]
    return pl.pallas_call(
        functools.partial(_ffn_kernel, final=final, n_sub=n_sub),
        grid=(b, n // bm),
        in_specs=in_specs,
        out_specs=rows(d),
        out_shape=jax.ShapeDtypeStruct((b, n, d), F32),
        compiler_params=_params(
            ("arbitrary", "arbitrary"),
            resident=_block_bytes(in_specs[5:], operands[5:]),
            streamed=_block_bytes(in_specs[:5], operands[:5]) + _nbytes((bm, d), F32),
            temporaries=n_sub * (4 * _nbytes((TM, d), F32) + 2 * _nbytes((TM, d_ff), F32) + _nbytes((TM, d_ff), BF16))),
        name="out_ffn",
    )(*operands)


def _rope_tables(n_lat):
    n_freq = QK_DIM // 4
    tok = np.arange(n_lat)
    inv_freq = np.float32(ROPE_BASE) ** (-np.arange(n_freq, dtype=np.float32) / np.float32(n_freq))
    ang_r = ((tok // GRID_W).astype(np.float32)[:, None] * inv_freq).astype(np.float64)
    ang_c = ((tok % GRID_W).astype(np.float32)[:, None] * inv_freq).astype(np.float64)
    cos = np.concatenate([np.cos(ang_r)] * 2 + [np.cos(ang_c)] * 2, axis=-1)
    sin = np.concatenate([-np.sin(ang_r), np.sin(ang_r), -np.sin(ang_c), np.sin(ang_c)], axis=-1)
    reps = LANES // QK_DIM
    return (jnp.asarray(np.tile(cos, (1, reps)), dtype=F32), jnp.asarray(np.tile(sin, (1, reps)), dtype=F32))


def _dft_angles(rows, n):
    return 2.0 * np.pi * ((np.asarray(rows, np.int64)[:, None] * np.arange(n, dtype=np.int64)[None, :]) % n) / n


def _dft_tables_small(n):
    ang = _dft_angles(np.arange(n), n)
    return np.cos(ang).astype(np.float32), np.sin(ang).astype(np.float32)


def _dft_tables(n):
    rows = n // 2
    if n <= 4 * DFT_LO:
        return [jnp.asarray(t[:rows]).astype(BF16) for t in _dft_tables_small(n)]
    hi = rows // DFT_LO
    ang_hi = _dft_angles(np.arange(hi) * DFT_LO, n)
    ang_lo = _dft_angles(np.arange(DFT_LO), n)
    ch, sh = (jnp.asarray(f(ang_hi), dtype=F32)[:, None, :] for f in (np.cos, np.sin))
    cl, sl = (jnp.asarray(f(ang_lo), dtype=F32)[None, :, :] for f in (np.cos, np.sin))
    return [(ch * cl - sh * sl).reshape(rows, n).astype(BF16), (sh * cl + ch * sl).reshape(rows, n).astype(BF16)]


def _block_diag(blocks):
    g, r, c = blocks.shape[-3:]
    eye = jnp.eye(g, dtype=blocks.dtype)
    out = eye[:, None, :, None] * blocks[..., :, :, None, :]
    return out.reshape(blocks.shape[:-3] + (g * r, g * c))


def _prep_inproj_kernel(w_ref, perm_ref, o_ref):
    wb = w_ref[...].astype(BF16)
    nqk = perm_ref.shape[0]
    o_ref[:, 0:nqk] = _dot(wb[:, 0:nqk], perm_ref[...]).astype(BF16)
    o_ref[:, nqk:] = wb[:, nqk:]


def _prep_inproj(w_in, qk_w):
    depth, d, n = w_in.shape
    nqk = 4 * qk_w
    dst = np.arange(nqk)
    pair, head, mp, dim = dst // (2 * qk_w), dst % (2 * qk_w) // LANES, dst % LANES // QK_DIM, dst % QK_DIM
    src = ((2 * pair + mp) * HEADS + head) * QK_DIM + dim
    perm = (jnp.arange(nqk)[:, None] == jnp.asarray(src)[None, :]).astype(BF16)
    w_spec = pl.BlockSpec((None, TM, n), lambda l, j: (l, j, 0))
    return pl.pallas_call(
        _prep_inproj_kernel,
        grid=(depth, d // TM),
        in_specs=[w_spec, _resident(perm)],
        out_specs=w_spec,
        out_shape=jax.ShapeDtypeStruct(w_in.shape, BF16),
        compiler_params=_params(
            ("arbitrary", "arbitrary"),
            resident=_nbytes(perm.shape, BF16),
            streamed=_nbytes((TM, n), F32) + _nbytes((TM, n), BF16),
            temporaries=_nbytes((TM, n), BF16) + _nbytes((TM, nqk), F32)),
        name="prep_inproj",
    )(w_in, perm)


def kernel(x, c, ctx, c_ctx, w_ada, b_ada, norm1_g, norm2_g, w_in, lam_q1, lam_k1, lam_q2, lam_k2, subln_g,
           w_fourier, conv_w, conv_b, conv_ln_g, conv_ln_b, w_conv_out, w_out, w_ffn1, w_ffn3, w_ffn2, final_g):
    b, n_lat, d = x.shape
    n_ctx = ctx.shape[1]
    depth = w_ada.shape[0]
    fw = w_fourier.shape[1] * w_fourier.shape[2]
    cw = conv_w.shape[-1]
    qk_w = HEADS * QK_DIM
    attn_w = HEADS * V_DIM
    assert n_lat % GRID_W == 0 and n_ctx % TM == 0
    assert w_in.shape[-1] == 4 * qk_w + attn_w + fw + 2 * cw

    pad = (-(b + 1)) % SUBLANES
    c_rows = jnp.concatenate([c, c_ctx[None, :], jnp.zeros((pad, d), c.dtype)], axis=0)
    mod = _ada(c_rows, w_ada, b_ada)
    mod = mod.reshape(depth, mod.shape[1], 6, d)
    ctx_row = b

    rope = _rope_tables(n_lat)
    cc, cs = _dft_tables_small(fw // FOURIER_GROUPS)
    eye = np.eye(FOURIER_GROUPS, dtype=np.float32)
    chan_c = jnp.asarray(np.kron(eye, cc)).astype(BF16)
    chan_s = jnp.asarray(np.kron(eye, cs)).astype(BF16)
    lat_tables = _dft_tables(n_lat)
    ctx_tables = _dft_tables(n_ctx)
    group = cw // CONV_GROUPS
    avg = jnp.asarray(np.kron(np.eye(CONV_GROUPS), np.full((group, group), 1.0 / group)), dtype=F32).astype(BF16)

    w_in_b = _prep_inproj(w_in, qk_w)
    wf_bd = _block_diag(w_fourier).astype(BF16)
    w_pw_b = w_conv_out.astype(BF16)
    w_out_b = w1_b = w3_b = w2_b = None

    def per_layer_rows(a):
        return a.reshape(depth, 1, a.shape[-1])

    norm1, norm2, subln = per_layer_rows(norm1_g), per_layer_rows(norm2_g), per_layer_rows(subln_g)
    lam_vecs = [per_layer_rows(a) for a in (lam_q1, lam_k1, lam_q2, lam_k2)]
    cb, lg, lb = per_layer_rows(conv_b), per_layer_rows(conv_ln_g), per_layer_rows(conv_ln_b)
    fg = final_g[None, :]

    def mixers(u_f, u_c, n, tables, l):
        yf = _fourier(u_f, n, chan_c, chan_s, wf_bd, l, tables)
        yc = _conv(u_c, n, conv_w, cb, lg, lb, avg, w_pw_b, l)
        return yf, yc

    ctx_flat = ctx.reshape(1, b * n_ctx, d)
    for l in range(depth):
        last = l == depth - 1
        lam_init = 0.8 - 0.6 * math.exp(-0.3 * l)
        q, k, vt, kn, uf, uc = _inproj(x, mod, norm1, w_in_b, rope, l, None, fw, 2 * cw)
        yf, yc = mixers(uf, uc, n_lat, lat_tables, l)
        cast = (w_out, w_ffn1, w_ffn3, w_ffn2) if l == 0 else ()
        if last:
            kv_ctx = _inproj(ctx_flat, mod, norm1, w_in_b, None, l, ctx_row, fw, 2 * cw, kv_only=True)
            (o,), casted = _attention(kv_ctx, (q, k, vt, kn), lam_vecs, subln, l, lam_init, False, cast)
        else:
            qc, kc, vtc, knc, ufc, ucc = _inproj(ctx_flat, mod, norm1, w_in_b, None, l, ctx_row, fw, 2 * cw)
            (oc, o), casted = _attention((qc, kc, vtc, knc), (q, k, vt, kn), lam_vecs, subln, l, lam_init, True, cast)
        if cast:
            w_out_b, w1_b, w3_b, w2_b = casted
        if not last:
            yfc, ycc = mixers(ufc, ucc, n_ctx, ctx_tables, l)
            ctx_flat = _out_ffn(ctx_flat, oc, yfc, ycc, mod, norm2, w_out_b, w1_b, w3_b, w2_b, fg, l, ctx_row,
                                final=False)
        x = _out_ffn(x, o, yf, yc, mod, norm2, w_out_b, w1_b, w3_b, w2_b, fg, l, None, final=last)
    return x
```

```python
import functools
import math

import numpy as np
import jax
import jax.numpy as jnp
from jax import lax
from jax.experimental import pallas as pl
from jax.experimental.pallas import tpu as pltpu

F32 = jnp.float32
BF16 = jnp.bfloat16

GRID_W = 64
HEADS = 4
QK_DIM = 64
V_DIM = 2 * QK_DIM
FOURIER_GROUPS = 4
CONV_GROUPS = 4
CONV_K = 31
ROPE_BASE = 10000.0
EPS = 1e-6
LOG2_E = math.log2(math.e)
BOUND_SLACK = 1.0 + 2.0 ** -6
MIN_SOFTMAX_DENOMINATOR = 2.0 ** -90

LANES = 128
SUBLANES = 8
TM = 256
ROW_SUBTILES = 2
INPROJ_SUBTILES = 4
ATT_KC = 256
CONV_HALO = 16
CONV_CHUNK = 128
ADA_TN = 1024
DFT_LO = 64
V7X_VMEM_BYTES = 64 * 1024 * 1024
VMEM_REQUEST_BYTES = V7X_VMEM_BYTES * 15 // 16


def _nbytes(shape, dtype):
    return math.prod(shape) * jnp.dtype(dtype).itemsize


def _block_bytes(specs, arrays):
    return sum(_nbytes([s for s in spec.block_shape if s is not None], arr.dtype) for spec, arr in zip(specs, arrays))


def _params(sem, *, resident=0, streamed=0, scratch=0, temporaries=0):
    need = resident + 2 * streamed + scratch + temporaries
    assert need <= VMEM_REQUEST_BYTES, f"VMEM estimate {need} exceeds the request {VMEM_REQUEST_BYTES}"
    return pltpu.CompilerParams(dimension_semantics=sem, vmem_limit_bytes=VMEM_REQUEST_BYTES)


def _resident(arr, layer=None, depth=None):
    if layer is None:
        idx = (0,) * arr.ndim
        return pl.BlockSpec(arr.shape, lambda *_: idx, pipeline_mode=pl.Buffered(1))
    idx = (layer,) + (0,) * (arr.ndim - 1)
    if depth is not None:
        return pl.BlockSpec((arr.shape[0] // depth, arr.shape[1]), lambda *_: idx, pipeline_mode=pl.Buffered(1))
    return pl.BlockSpec((None,) + arr.shape[1:], lambda *_: idx, pipeline_mode=pl.Buffered(1))


def _segment_spec(arr, n):
    if arr.shape[0] == 1:
        return pl.BlockSpec((None, n, arr.shape[-1]), lambda i: (0, i, 0))
    return pl.BlockSpec((None, n, arr.shape[-1]), lambda i: (i, 0, 0))


def _mod_spec(mod, layer, fixed_row):
    if fixed_row is None:
        return pl.BlockSpec((None, None) + mod.shape[2:], lambda i, j: (layer, i, 0, 0))
    return pl.BlockSpec((None, None) + mod.shape[2:], lambda i, j: (layer, fixed_row, 0, 0))


def _cast_slabs(weights, grid):
    steps = grid[0] * grid[1]
    operands, specs, shapes = [], [], []
    for wt in weights:
        flat = wt.reshape(-1, wt.shape[-1])
        slab = flat.shape[0] // steps
        assert slab * steps == flat.shape[0] and slab % (2 * SUBLANES) == 0
        operands.append(flat)
        specs.append(pl.BlockSpec((slab, flat.shape[1]), lambda i, j: (i * grid[1] + j, 0)))
        shapes.append(jax.ShapeDtypeStruct(flat.shape, BF16))
    return operands, specs, shapes


def _sigmoid(x):
    return 1.0 / (1.0 + jnp.exp(-x))


def _rms(x):
    return x * lax.rsqrt(jnp.mean(x * x, axis=-1, keepdims=True) + EPS)


def _dot(a, b):
    return jnp.dot(a, b, preferred_element_type=F32)


def _ada_kernel(c_ref, w_ref, b_ref, o_ref):
    c = c_ref[...]
    s = (c * _sigmoid(c)).astype(BF16)
    o_ref[...] = _dot(s, w_ref[...].astype(BF16)) + b_ref[...]


def _ada(c_rows, w_ada, b_ada):
    depth, d, n = w_ada.shape
    rows = c_rows.shape[0]
    return pl.pallas_call(
        _ada_kernel,
        grid=(depth, n // ADA_TN),
        in_specs=[
            pl.BlockSpec((rows, d), lambda l, j: (0, 0)),
            pl.BlockSpec((None, d, ADA_TN), lambda l, j: (l, 0, j)),
            pl.BlockSpec((None, 1, ADA_TN), lambda l, j: (l, 0, j)),
        ],
        out_specs=pl.BlockSpec((None, rows, ADA_TN), lambda l, j: (l, 0, j)),
        out_shape=jax.ShapeDtypeStruct((depth, rows, n), F32),
        compiler_params=_params(
            ("arbitrary", "arbitrary"),
            resident=_nbytes(c_rows.shape, F32),
            streamed=_nbytes((d + 1 + rows, ADA_TN), F32),
            temporaries=_nbytes((d, ADA_TN), BF16) + _nbytes((rows, ADA_TN), F32)),
        name="ada",
    )(c_rows, w_ada, b_ada.reshape(depth, 1, n))


def _inproj_kernel(*refs, use_rope, kv_only, n_sub):
    x_ref, mod_ref, g_ref, w_ref = refs[:4]
    refs = refs[4:]
    if use_rope:
        cos_ref, sin_ref = refs[:2]
        refs = refs[2:]
    hw = HEADS * LANES
    lane = lax.broadcasted_iota(jnp.int32, (TM, LANES), 1)
    first_half = (lane & (QK_DIM // 4)) == 0
    row = lax.broadcasted_iota(jnp.int32, (LANES, TM), 0)
    zero = jnp.zeros((LANES, TM), F32)
    dim = lax.broadcasted_iota(jnp.int32, (LANES, LANES), 0)
    col = lax.broadcasted_iota(jnp.int32, (LANES, LANES), 1)
    map_sum = jnp.where(col == jnp.where(dim < QK_DIM, 0, 1), 1.0, 0.0).astype(BF16)

    for u in range(n_sub):
        rows = slice(u * TM, (u + 1) * TM)
        h = (_rms(x_ref[rows, :]) * g_ref[...] * (1.0 + mod_ref[1:2, :]) + mod_ref[0:1, :]).astype(BF16)

        def rope(t):
            if not use_rope:
                return t
            partner = jnp.where(first_half, pltpu.roll(t, LANES - QK_DIM // 4, 1), pltpu.roll(t, QK_DIM // 4, 1))
            return t * cos_ref[rows, :] + partner * sin_ref[rows, :]

        def put_keys(j, kf):
            k_ref[j, rows, :] = kf.astype(BF16)
            norms = _dot((kf * kf).astype(BF16), map_sum)
            kn_ref[j, u] = jnp.max(norms.reshape(TM // SUBLANES, SUBLANES, LANES), axis=0)

        if kv_only:
            k_ref, vt_ref, kn_ref = refs
            r = _dot(h, w_ref[:, hw:3 * hw])
            for j in range(HEADS):
                put_keys(j, rope(r[:, j * LANES:(j + 1) * LANES]))
                vt_ref[j, :, rows] = r[:, hw + j * LANES:hw + (j + 1) * LANES].T.astype(BF16)
            continue

        q_ref, k_ref, vt_ref, kn_ref, uf_ref, uc_ref = refs
        r = _dot(h, w_ref[...])
        for j in range(HEADS):
            qt = (rope(r[:, j * LANES:(j + 1) * LANES]) * (QK_DIM ** -0.5 * LOG2_E)).T
            for mp in range(2):
                sel = (row < QK_DIM) == (mp == 0)
                qm = qt[mp * QK_DIM:(mp + 1) * QK_DIM, :]
                qn = jnp.sqrt(jnp.sum(qm * qm, axis=0, keepdims=True)) * BOUND_SLACK
                q_ref[j, u, mp, 0:LANES, :] = jnp.where(sel, qt, zero).astype(BF16)
                q_ref[j, u, mp, LANES:2 * LANES, :] = jnp.where(row == 0, qn, zero).astype(BF16)
            put_keys(j, rope(r[:, hw + j * LANES:hw + (j + 1) * LANES]))
            vt_ref[j, :, rows] = r[:, 2 * hw + j * LANES:2 * hw + (j + 1) * LANES].T.astype(BF16)
        fw = uf_ref.shape[-1]
        uf_ref[rows, :] = r[:, 3 * hw:3 * hw + fw].astype(BF16)
        uc_ref[rows, :] = r[:, 3 * hw + fw:]


def _inproj(x, mod, g, w, rope_tables, layer, fixed_row, fw, cw2, kv_only=False):
    b, n, d = x.shape
    n_sub = INPROJ_SUBTILES if n % (INPROJ_SUBTILES * TM) == 0 else ROW_SUBTILES
    bm = n_sub * TM
    assert n % bm == 0
    use_rope = rope_tables is not None
    in_specs = [
        pl.BlockSpec((None, bm, d), lambda i, j: (i, j, 0)),
        _mod_spec(mod, layer, fixed_row),
        _resident(g, layer),
        _resident(w, layer),
    ]
    operands = [x, mod, g, w]
    if use_rope:
        in_specs += [pl.BlockSpec((bm, LANES), lambda i, j: (j, 0))] * 2
        operands += list(rope_tables)
    out_specs = [
        pl.BlockSpec((None, HEADS, n_sub, 2, 2 * LANES, TM), lambda i, j: (i, 0, j, 0, 0, 0)),
        pl.BlockSpec((None, HEADS, bm, LANES), lambda i, j: (i, 0, j, 0)),
        pl.BlockSpec((None, HEADS, V_DIM, bm), lambda i, j: (i, 0, 0, j)),
        pl.BlockSpec((None, HEADS, n_sub, SUBLANES, LANES), lambda i, j: (i, 0, j, 0, 0)),
        pl.BlockSpec((None, bm, fw), lambda i, j: (i, j, 0)),
        pl.BlockSpec((None, bm, cw2), lambda i, j: (i, j, 0)),
    ]
    out_shape = [
        jax.ShapeDtypeStruct((b, HEADS, n // TM, 2, 2 * LANES, TM), BF16),
        jax.ShapeDtypeStruct((b, HEADS, n, LANES), BF16),
        jax.ShapeDtypeStruct((b, HEADS, V_DIM, n), BF16),
        jax.ShapeDtypeStruct((b, HEADS, n // TM, SUBLANES, LANES), F32),
        jax.ShapeDtypeStruct((b, n, fw), BF16),
        jax.ShapeDtypeStruct((b, n, cw2), F32),
    ]
    if kv_only:
        out_specs, out_shape = out_specs[1:4], out_shape[1:4]
    return pl.pallas_call(
        functools.partial(_inproj_kernel, use_rope=use_rope, kv_only=kv_only, n_sub=n_sub),
        grid=(b, n // bm),
        in_specs=in_specs,
        out_specs=out_specs,
        out_shape=out_shape,
        compiler_params=_params(
            ("arbitrary", "arbitrary"),
            resident=_block_bytes(in_specs[2:4], operands[2:4]),
            streamed=_block_bytes(in_specs[:2] + in_specs[4:] + out_specs, operands[:2] + operands[4:] + out_shape),
            temporaries=2 * n_sub * _nbytes((TM, w.shape[-1]), F32)),
        name="inproj",
    )(*operands)


def _attn_kernel(*refs, lam_init, with_ctx, bounded, n_cast):
    lq1_ref, lk1_ref, lq2_ref, lk2_ref, g_ref, kc_ref, vtc_ref, knc_ref, ql_ref, kl_ref, vtl_ref, knl_ref = refs[:12]
    refs = list(refs[12:])
    s_ref = refs.pop()
    qc_ref = refs.pop(0) if with_ctx else None
    cast_src = [refs.pop(0) for _ in range(n_cast)]
    oc_ref = refs.pop(0) if with_ctx else None
    ol_ref = refs.pop(0)
    lmin_ref = refs.pop(0) if bounded else None
    for src_ref, dst_ref in zip(cast_src, refs):
        dst_ref[...] = src_ref[...].astype(BF16)
    lam = (jnp.exp(jnp.sum(lq1_ref[...] * lk1_ref[...], axis=-1, keepdims=True))
           - jnp.exp(jnp.sum(lq2_ref[...] * lk2_ref[...], axis=-1, keepdims=True)) + lam_init)

    def fold(x):
        return x.reshape(x.shape[0] // SUBLANES, SUBLANES, x.shape[1])

    def finish(o_ref, t, ot, l1):
        ot = ot * (1.0 / l1)
        ms = jnp.mean(ot * ot, axis=0, keepdims=True)
        on = (ot * lax.rsqrt(ms + EPS)).T * g_ref[...] * (1.0 - lam_init)
        o_ref[t * TM:(t + 1) * TM, :] = on.astype(BF16)

    def key_chunks(key_refs):
        return [(k_ref, vt_ref, c) for k_ref, vt_ref, _ in key_refs for c in range(0, k_ref.shape[0], ATT_KC)]

    def pipeline_bounded(q_ref, o_ref, key_refs):
        chunks = key_chunks(key_refs)
        n_tiles = q_ref.shape[0]
        lane = lax.broadcasted_iota(jnp.int32, (ATT_KC, LANES), 1)
        key_cols = []
        ksq = functools.reduce(jnp.maximum, [jnp.max(kn_ref[...], axis=0) for _, _, kn_ref in key_refs])
        kb = jnp.sqrt(jnp.max(ksq, axis=0, keepdims=True)) * BOUND_SLACK
        for mp in range(2):
            key_cols.append(jnp.where(lane == 0, -kb[:, mp:mp + 1], 0.0).astype(BF16))
        sums = {}
        lmin = None
        for u in range(n_tiles + 1):
            ta, tc = u, u - 1
            do_a, do_c = ta < n_tiles, 0 <= tc < n_tiles
            if do_a:
                qts = [q_ref[ta, mp] for mp in range(2)]
                l8 = [jnp.zeros((SUBLANES, TM), F32) for _ in range(2)]
            if do_c:
                lc = sums.pop(tc)
                rho = lam * lc[0] / lc[1]
                ot = None
            for ci, (k_ref, vt_ref, c) in enumerate(chunks):
                rows = slice(ci * ATT_KC, (ci + 1) * ATT_KC)
                if do_a:
                    for mp in range(2):
                        keys = jnp.concatenate([k_ref[c:c + ATT_KC, :], key_cols[mp]], axis=1)
                        e = jnp.exp2(_dot(keys, qts[mp]))
                        s_ref[ta % 2, mp, rows, :] = e
                        l8[mp] = l8[mp] + jnp.sum(fold(e), axis=0)
                if do_c:
                    p = (s_ref[tc % 2, 0, rows, :] - rho * s_ref[tc % 2, 1, rows, :]).astype(BF16)
                    part = _dot(vt_ref[:, c:c + ATT_KC], p)
                    ot = part if ot is None else ot + part
            if do_a:
                sums[ta] = [jnp.sum(l, axis=0, keepdims=True) for l in l8]
                low = jnp.minimum(sums[ta][0], sums[ta][1])
                lmin = low if lmin is None else jnp.minimum(lmin, low)
            if do_c:
                finish(o_ref, tc, ot, lc[0])
        return jnp.min(lmin, axis=1, keepdims=True)

    def pipeline(q_ref, o_ref, key_refs):
        chunks = key_chunks(key_refs)
        n_tiles = q_ref.shape[0]
        maxima, sums = {}, {}
        for u in range(n_tiles + 2):
            ta, tb, tc = u, u - 1, u - 2
            do_a, do_b, do_c = ta < n_tiles, 0 <= tb < n_tiles, 0 <= tc < n_tiles
            if do_a:
                qts = [q_ref[ta, mp, 0:LANES, :] for mp in range(2)]
                m8 = [None, None]
            if do_b:
                mb = maxima.pop(tb)
                l8 = [jnp.zeros((SUBLANES, TM), F32) for _ in range(2)]
            if do_c:
                lc = sums.pop(tc)
                rho = lam * lc[0] / lc[1]
                ot = None
            for ci, (k_ref, vt_ref, c) in enumerate(chunks):
                rows = slice(ci * ATT_KC, (ci + 1) * ATT_KC)
                if do_a:
                    for mp in range(2):
                        s = _dot(k_ref[c:c + ATT_KC, :], qts[mp])
                        s_ref[ta % 3, mp, rows, :] = s
                        cm = jnp.max(fold(s), axis=0)
                        m8[mp] = cm if m8[mp] is None else jnp.maximum(m8[mp], cm)
                if do_b:
                    for mp in range(2):
                        e = jnp.exp2(s_ref[tb % 3, mp, rows, :] - mb[mp])
                        s_ref[tb % 3, mp, rows, :] = e
                        l8[mp] = l8[mp] + jnp.sum(fold(e), axis=0)
                if do_c:
                    p = (s_ref[tc % 3, 0, rows, :] - rho * s_ref[tc % 3, 1, rows, :]).astype(BF16)
                    part = _dot(vt_ref[:, c:c + ATT_KC], p)
                    ot = part if ot is None else ot + part
            if do_a:
                maxima[ta] = [jnp.max(m, axis=0, keepdims=True) for m in m8]
            if do_b:
                sums[tb] = [jnp.sum(l, axis=0, keepdims=True) for l in l8]
            if do_c:
                finish(o_ref, tc, ot, lc[0])

    run = pipeline_bounded if bounded else pipeline
    ctx_keys, lat_keys = (kc_ref, vtc_ref, knc_ref), (kl_ref, vtl_ref, knl_ref)
    lmin = run(ql_ref, ol_ref, [ctx_keys, lat_keys])
    if with_ctx:
        lmin_ctx = run(qc_ref, oc_ref, [ctx_keys])
    if bounded:
        if with_ctx:
            lmin = jnp.minimum(lmin, lmin_ctx)
        lmin_ref[...] = jnp.broadcast_to(lmin, lmin_ref.shape)


def _attention_call(qkv_ctx, qkv_lat, lam_vecs, g, layer, lam_init, with_ctx, bounded, cast=()):
    q_lat, k_lat, vt_lat, kn_lat = qkv_lat
    k_ctx, vt_ctx, kn_ctx = qkv_ctx[-3:]
    b, h, n_lat, _ = k_lat.shape
    n_ctx = k_ctx.shape[2] // b
    lat_tiles, ctx_tiles = n_lat // TM, n_ctx // TM

    def tiles_spec(tiles, tail, ctx):
        zeros = (0,) * len(tail)
        if ctx:
            return pl.BlockSpec((None, None, tiles) + tail, lambda i, j: (0, j, i) + zeros)
        return pl.BlockSpec((None, None, tiles) + tail, lambda i, j: (i, j, 0) + zeros)

    q_tile, kn_tile = (2, 2 * LANES, TM), (SUBLANES, LANES)
    in_specs = [_resident(v, layer) for v in lam_vecs] + [
        _resident(g, layer),
        pl.BlockSpec((None, None, n_ctx, LANES), lambda i, j: (0, j, i, 0)),
        pl.BlockSpec((None, None, V_DIM, n_ctx), lambda i, j: (0, j, 0, i)),
        tiles_spec(ctx_tiles, kn_tile, True),
        tiles_spec(lat_tiles, q_tile, False),
        pl.BlockSpec((None, None, n_lat, LANES), lambda i, j: (i, j, 0, 0)),
        pl.BlockSpec((None, None, V_DIM, n_lat), lambda i, j: (i, j, 0, 0)),
        tiles_spec(lat_tiles, kn_tile, False),
    ]
    operands = list(lam_vecs) + [g, k_ctx, vt_ctx, kn_ctx, q_lat, k_lat, vt_lat, kn_lat]
    out_specs = [pl.BlockSpec((None, n_lat, V_DIM), lambda i, j: (i, 0, j))]
    out_shape = [jax.ShapeDtypeStruct((b, n_lat, h * V_DIM), BF16)]
    if with_ctx:
        in_specs.append(tiles_spec(ctx_tiles, q_tile, True))
        operands.append(qkv_ctx[0])
        out_specs.insert(0, pl.BlockSpec((None, n_ctx, V_DIM), lambda i, j: (0, i, j)))
        out_shape.insert(0, jax.ShapeDtypeStruct((1, b * n_ctx, h * V_DIM), BF16))
    if bounded:
        out_specs.append(pl.BlockSpec((None, None, SUBLANES, LANES), lambda i, j: (i, j, 0, 0)))
        out_shape.append(jax.ShapeDtypeStruct((b, h, SUBLANES, LANES), F32))
    cast_operands, cast_specs, cast_shapes = _cast_slabs(cast, (b, h))
    in_specs, operands = in_specs + cast_specs, operands + cast_operands
    scores_shape = (2 if bounded else 3, 2, n_ctx + n_lat, TM)
    outs = pl.pallas_call(
        functools.partial(_attn_kernel, lam_init=lam_init, with_ctx=with_ctx, bounded=bounded, n_cast=len(cast)),
        grid=(b, h),
        in_specs=in_specs,
        out_specs=out_specs + cast_specs,
        out_shape=out_shape + cast_shapes,
        scratch_shapes=[pltpu.VMEM(scores_shape, F32)],
        compiler_params=_params(
            ("arbitrary", "arbitrary"),
            streamed=_block_bytes(in_specs + out_specs + cast_specs, operands + out_shape + cast_shapes),
            scratch=_nbytes(scores_shape, F32),
            temporaries=8 * _nbytes((ATT_KC, TM), F32)),
        name="attention_bounded" if bounded else "attention_exact",
    )(*operands)
    n_main = len(out_shape)
    return list(outs[:n_main]), list(outs[n_main:])


def _attention(qkv_ctx, qkv_lat, lam_vecs, g, layer, lam_init, with_ctx, cast=()):
    args = (qkv_ctx, qkv_lat, lam_vecs, g, layer, lam_init, with_ctx)
    (*outs, lmin), casted = _attention_call(*args, bounded=True, cast=cast)
    safe = jnp.min(lmin) >= MIN_SOFTMAX_DENOMINATOR
    outs = lax.cond(safe, lambda: tuple(outs), lambda: tuple(_attention_call(*args, bounded=False)[0]))
    return list(outs), casted


def _fourier_kernel(u_ref, chan_c_ref, chan_s_ref, wf_ref, pos_c_ref, pos_s_ref, mid_ref, flip_ref, y_ref):
    n, width = u_ref.shape
    half = n // 2
    scale = (n * width // FOURIER_GROUPS) ** -0.5
    u = u_ref[...]
    a = _dot(pos_c_ref[...], u).astype(BF16)
    b = _dot(pos_s_ref[...], u).astype(BF16)
    p = _dot(a, chan_c_ref[...])
    q = _dot(b, chan_s_ref[...])
    y_ref[0:half, :] = _dot(((p - q) * scale).astype(BF16), wf_ref[...]).astype(BF16)
    mirrored = _dot(flip_ref[...], ((p + q) * scale).astype(BF16))
    mid = _dot(_dot(mid_ref[...], u).astype(BF16), chan_c_ref[...]) * scale
    rows = mid.shape[0]
    upper = jnp.concatenate([mirrored[0:rows] + mid, mirrored[rows:]], axis=0)
    y_ref[half:, :] = _dot(upper.astype(BF16), wf_ref[...]).astype(BF16)


def _fourier(uf, n, chan_c, chan_s, wf_bd, layer, pos_tables):
    fw = uf.shape[-1]
    half = n // 2
    idx = jnp.arange(half)
    flip = (idx[:, None] + idx[None, :] == half).astype(BF16)
    mid = np.zeros((SUBLANES, n), np.float32)
    mid[0] = 1.0 - 2.0 * (np.arange(n) % 2)
    mid = jnp.asarray(mid).astype(BF16)
    row_spec = _segment_spec(uf, n)
    operands = [uf, chan_c, chan_s, wf_bd, *pos_tables, mid, flip]
    in_specs = [row_spec, _resident(chan_c), _resident(chan_s), _resident(wf_bd, layer)] + [
        _resident(t) for t in (*pos_tables, mid, flip)]
    return pl.pallas_call(
        _fourier_kernel,
        grid=(uf.shape[0] * uf.shape[1] // n,),
        in_specs=in_specs,
        out_specs=row_spec,
        out_shape=jax.ShapeDtypeStruct(uf.shape, BF16),
        compiler_params=_params(
            ("arbitrary",),
            resident=_block_bytes(in_specs[1:], operands[1:]),
            streamed=2 * _nbytes((n, fw), BF16),
            temporaries=8 * _nbytes((half, fw), F32)),
        name="fourier",
    )(*operands)


def _split_dot(x, m):
    hi = x.astype(BF16)
    lo = (x - hi.astype(F32)).astype(BF16)
    return _dot(hi, m) + _dot(lo, m)


def _conv_kernel(u_ref, w_ref, b_ref, lg_ref, lb_ref, avg_ref, pw_ref, y_ref, zpad_ref, acc_a_ref, acc_b_ref):
    n, cw = y_ref.shape
    halo = jnp.zeros((CONV_HALO, cw), F32)

    def taps(i, acc_ref, slot):
        r0 = pl.multiple_of(i * CONV_CHUNK, CONV_CHUNK)
        for lo in range(0, cw, LANES):
            win = zpad_ref[pl.ds(r0, CONV_CHUNK + 2 * CONV_HALO), lo:lo + LANES]
            part = jnp.zeros((CONV_CHUNK, LANES), F32) + b_ref[:, lo:lo + LANES]
            for shift in range(SUBLANES):
                rolled = win if shift == 0 else pltpu.roll(win, win.shape[0] - shift, 0)
                for aligned in range(0, 2 * CONV_HALO, SUBLANES):
                    tap = aligned + shift - (CONV_HALO - CONV_K // 2)
                    if 0 <= tap < CONV_K:
                        part = part + rolled[aligned:aligned + CONV_CHUNK] * w_ref[tap:tap + 1, lo:lo + LANES]
            acc_ref[slot, :, lo:lo + LANES] = part

    def project(acc_ref, slot):
        acc = acc_ref[slot]
        mu = _split_dot(acc, avg_ref[...])
        dev = acc - mu
        var = _split_dot(dev * dev, avg_ref[...])
        zn = dev * lax.rsqrt(var + EPS) * lg_ref[...] + lb_ref[...]
        act = (zn * _sigmoid(zn)).astype(BF16)
        return _dot(act, pw_ref[...]).astype(BF16)

    def round_(first_tap, tap_ref, ready_ref):
        if first_tap is not None:
            taps(first_tap, tap_ref, 0)
            taps(first_tap + 1, tap_ref, 1)
        return [project(ready_ref, 0), project(ready_ref, 1)]

    def store(first_chunk, ys):
        r0 = pl.multiple_of(first_chunk * CONV_CHUNK, 2 * CONV_CHUNK)
        y_ref[pl.ds(r0, len(ys) * CONV_CHUNK), :] = jnp.concatenate(ys, axis=0)

    zpad_ref[0:CONV_HALO, :] = halo
    zpad_ref[CONV_HALO:CONV_HALO + n, :] = u_ref[:, 0:cw] * _sigmoid(u_ref[:, cw:2 * cw])
    zpad_ref[CONV_HALO + n:2 * CONV_HALO + n, :] = halo

    chunks = n // CONV_CHUNK
    assert chunks == 2 or chunks % 4 == 0

    def step(j, carry):
        c0 = 4 * j
        ys = round_(c0 + 2, acc_b_ref, acc_a_ref) + round_(c0 + 4, acc_a_ref, acc_b_ref)
        store(c0, ys)
        return carry

    taps(0, acc_a_ref, 0)
    taps(1, acc_a_ref, 1)
    if chunks == 2:
        store(0, round_(None, None, acc_a_ref))
    else:
        lax.fori_loop(0, chunks // 4 - 1, step, 0)
        store(chunks - 4, round_(chunks - 2, acc_b_ref, acc_a_ref) + round_(None, None, acc_b_ref))


def _conv(uc, n, conv_w, conv_b, ln_g, ln_b, avg, w_pw, layer):
    cw2 = uc.shape[-1]
    cw = cw2 // 2
    y_shape = jax.ShapeDtypeStruct(uc.shape[:2] + (cw,), BF16)
    operands = [uc, conv_w, conv_b, ln_g, ln_b, avg, w_pw]
    in_specs = [
        _segment_spec(uc, n),
        _resident(conv_w, layer), _resident(conv_b, layer), _resident(ln_g, layer), _resident(ln_b, layer),
        _resident(avg), _resident(w_pw, layer),
    ]
    scratch = [(n + 2 * CONV_HALO, cw), (2, CONV_CHUNK, cw), (2, CONV_CHUNK, cw)]
    return pl.pallas_call(
        _conv_kernel,
        grid=(uc.shape[0] * uc.shape[1] // n,),
        in_specs=in_specs,
        out_specs=_segment_spec(y_shape, n),
        out_shape=y_shape,
        scratch_shapes=[pltpu.VMEM(s, F32) for s in scratch],
        compiler_params=_params(
            ("arbitrary",),
            resident=_block_bytes(in_specs[1:], operands[1:]),
            streamed=_nbytes((n, cw2), F32) + _nbytes((n, cw), BF16),
            scratch=sum(_nbytes(s, F32) for s in scratch),
            temporaries=3 * _nbytes((n, cw), F32) + 16 * _nbytes((CONV_CHUNK, cw), F32)),
        name="conv",
    )(*operands)


def _ffn_kernel(x_ref, o_ref, yf_ref, yc_ref, mod_ref, g_ref, wout_ref, w1_ref, w3_ref, w2_ref, fg_ref, out_ref,
                *, final, n_sub):
    aw = o_ref.shape[-1]
    fw = yf_ref.shape[-1]
    for u in range(n_sub):
        rows = slice(u * TM, (u + 1) * TM)
        y = (_dot(o_ref[rows, :], wout_ref[0:aw, :]) + _dot(yf_ref[rows, :], wout_ref[aw:aw + fw, :])
             + _dot(yc_ref[rows, :], wout_ref[aw + fw:, :]))
        x1 = x_ref[rows, :] + mod_ref[2:3, :] * y
        h = (_rms(x1) * g_ref[...] * (1.0 + mod_ref[4:5, :]) + mod_ref[3:4, :]).astype(BF16)
        a = _dot(h, w1_ref[...])
        gated = (a * _sigmoid(a) * _dot(h, w3_ref[...])).astype(BF16)
        x2 = x1 + mod_ref[5:6, :] * _dot(gated, w2_ref[...])
        if final:
            x2 = _rms(x2) * fg_ref[...]
        out_ref[rows, :] = x2


def _out_ffn(x, o, yf, yc, mod, g2, w_out, w1, w3, w2, final_g, layer, fixed_row, final):
    b, n, d = x.shape
    depth = g2.shape[0]
    n_sub = ROW_SUBTILES
    bm = n_sub * TM
    assert n % bm == 0

    def rows(width):
        return pl.BlockSpec((None, bm, width), lambda i, j: (i, j, 0))

    operands = [x, o, yf, yc, mod, g2, w_out, w1, w3, w2, final_g]
    in_specs = [
        rows(d), rows(o.shape[-1]), rows(yf.shape[-1]), rows(yc.shape[-1]),
        _mod_spec(mod, layer, fixed_row),
        _resident(g2, layer), _resident(w_out, layer, depth), _resident(w1, layer, depth),
        _resident(w3, layer, depth), _resident(w2, layer, depth), _resident(final_g),
    ]
    d_ff = w1.shape[-1]
    return pl.pallas_call(
        functools.partial(_ffn_kernel, final=final, n_sub=n_sub),
        grid=(b, n // bm),
        in_specs=in_specs,
        out_specs=rows(d),
        out_shape=jax.ShapeDtypeStruct((b, n, d), F32),
        compiler_params=_params(
            ("arbitrary", "arbitrary"),
            resident=_block_bytes(in_specs[5:], operands[5:]),
            streamed=_block_bytes(in_specs[:5], operands[:5]) + _nbytes((bm, d), F32),
            temporaries=n_sub * (4 * _nbytes((TM, d), F32) + 2 * _nbytes((TM, d_ff), F32) + _nbytes((TM, d_ff), BF16))),
        name="out_ffn",
    )(*operands)


def _rope_tables(n_lat):
    n_freq = QK_DIM // 4
    tok = np.arange(n_lat)
    inv_freq = np.float32(ROPE_BASE) ** (-np.arange(n_freq, dtype=np.float32) / np.float32(n_freq))
    ang_r = ((tok // GRID_W).astype(np.float32)[:, None] * inv_freq).astype(np.float64)
    ang_c = ((tok % GRID_W).astype(np.float32)[:, None] * inv_freq).astype(np.float64)
    cos = np.concatenate([np.cos(ang_r)] * 2 + [np.cos(ang_c)] * 2, axis=-1)
    sin = np.concatenate([-np.sin(ang_r), np.sin(ang_r), -np.sin(ang_c), np.sin(ang_c)], axis=-1)
    reps = LANES // QK_DIM
    return (jnp.asarray(np.tile(cos, (1, reps)), dtype=F32), jnp.asarray(np.tile(sin, (1, reps)), dtype=F32))


def _dft_angles(rows, n):
    return 2.0 * np.pi * ((np.asarray(rows, np.int64)[:, None] * np.arange(n, dtype=np.int64)[None, :]) % n) / n


def _dft_tables_small(n):
    ang = _dft_angles(np.arange(n), n)
    return np.cos(ang).astype(np.float32), np.sin(ang).astype(np.float32)


def _dft_tables(n):
    rows = n // 2
    if n <= 4 * DFT_LO:
        return [jnp.asarray(t[:rows]).astype(BF16) for t in _dft_tables_small(n)]
    hi = rows // DFT_LO
    ang_hi = _dft_angles(np.arange(hi) * DFT_LO, n)
    ang_lo = _dft_angles(np.arange(DFT_LO), n)
    ch, sh = (jnp.asarray(f(ang_hi), dtype=F32)[:, None, :] for f in (np.cos, np.sin))
    cl, sl = (jnp.asarray(f(ang_lo), dtype=F32)[None, :, :] for f in (np.cos, np.sin))
    return [(ch * cl - sh * sl).reshape(rows, n).astype(BF16), (sh * cl + ch * sl).reshape(rows, n).astype(BF16)]


def _block_diag(blocks):
    g, r, c = blocks.shape[-3:]
    eye = jnp.eye(g, dtype=blocks.dtype)
    out = eye[:, None, :, None] * blocks[..., :, :, None, :]
    return out.reshape(blocks.shape[:-3] + (g * r, g * c))


def _prep_inproj_kernel(w_ref, perm_ref, o_ref):
    wb = w_ref[...].astype(BF16)
    nqk = perm_ref.shape[0]
    o_ref[:, 0:nqk] = _dot(wb[:, 0:nqk], perm_ref[...]).astype(BF16)
    o_ref[:, nqk:] = wb[:, nqk:]


def _prep_inproj(w_in, qk_w):
    depth, d, n = w_in.shape
    nqk = 4 * qk_w
    dst = np.arange(nqk)
    pair, head, mp, dim = dst // (2 * qk_w), dst % (2 * qk_w) // LANES, dst % LANES // QK_DIM, dst % QK_DIM
    src = ((2 * pair + mp) * HEADS + head) * QK_DIM + dim
    perm = (jnp.arange(nqk)[:, None] == jnp.asarray(src)[None, :]).astype(BF16)
    w_spec = pl.BlockSpec((None, TM, n), lambda l, j: (l, j, 0))
    return pl.pallas_call(
        _prep_inproj_kernel,
        grid=(depth, d // TM),
        in_specs=[w_spec, _resident(perm)],
        out_specs=w_spec,
        out_shape=jax.ShapeDtypeStruct(w_in.shape, BF16),
        compiler_params=_params(
            ("arbitrary", "arbitrary"),
            resident=_nbytes(perm.shape, BF16),
            streamed=_nbytes((TM, n), F32) + _nbytes((TM, n), BF16),
            temporaries=_nbytes((TM, n), BF16) + _nbytes((TM, nqk), F32)),
        name="prep_inproj",
    )(w_in, perm)


def kernel(x, c, ctx, c_ctx, w_ada, b_ada, norm1_g, norm2_g, w_in, lam_q1, lam_k1, lam_q2, lam_k2, subln_g,
           w_fourier, conv_w, conv_b, conv_ln_g, conv_ln_b, w_conv_out, w_out, w_ffn1, w_ffn3, w_ffn2, final_g):
    b, n_lat, d = x.shape
    n_ctx = ctx.shape[1]
    depth = w_ada.shape[0]
    fw = w_fourier.shape[1] * w_fourier.shape[2]
    cw = conv_w.shape[-1]
    qk_w = HEADS * QK_DIM
    attn_w = HEADS * V_DIM
    assert n_lat % GRID_W == 0 and n_ctx % TM == 0
    assert w_in.shape[-1] == 4 * qk_w + attn_w + fw + 2 * cw

    pad = (-(b + 1)) % SUBLANES
    c_rows = jnp.concatenate([c, c_ctx[None, :], jnp.zeros((pad, d), c.dtype)], axis=0)
    mod = _ada(c_rows, w_ada, b_ada)
    mod = mod.reshape(depth, mod.shape[1], 6, d)
    ctx_row = b

    rope = _rope_tables(n_lat)
    cc, cs = _dft_tables_small(fw // FOURIER_GROUPS)
    eye = np.eye(FOURIER_GROUPS, dtype=np.float32)
    chan_c = jnp.asarray(np.kron(eye, cc)).astype(BF16)
    chan_s = jnp.asarray(np.kron(eye, cs)).astype(BF16)
    lat_tables = _dft_tables(n_lat)
    ctx_tables = _dft_tables(n_ctx)
    group = cw // CONV_GROUPS
    avg = jnp.asarray(np.kron(np.eye(CONV_GROUPS), np.full((group, group), 1.0 / group)), dtype=F32).astype(BF16)

    w_in_b = _prep_inproj(w_in, qk_w)
    wf_bd = _block_diag(w_fourier).astype(BF16)
    w_pw_b = w_conv_out.astype(BF16)
    w_out_b = w1_b = w3_b = w2_b = None

    def per_layer_rows(a):
        return a.reshape(depth, 1, a.shape[-1])

    norm1, norm2, subln = per_layer_rows(norm1_g), per_layer_rows(norm2_g), per_layer_rows(subln_g)
    lam_vecs = [per_layer_rows(a) for a in (lam_q1, lam_k1, lam_q2, lam_k2)]
    cb, lg, lb = per_layer_rows(conv_b), per_layer_rows(conv_ln_g), per_layer_rows(conv_ln_b)
    fg = final_g[None, :]

    def mixers(u_f, u_c, n, tables, l):
        yf = _fourier(u_f, n, chan_c, chan_s, wf_bd, l, tables)
        yc = _conv(u_c, n, conv_w, cb, lg, lb, avg, w_pw_b, l)
        return yf, yc

    ctx_flat = ctx.reshape(1, b * n_ctx, d)
    for l in range(depth):
        last = l == depth - 1
        lam_init = 0.8 - 0.6 * math.exp(-0.3 * l)
        q, k, vt, kn, uf, uc = _inproj(x, mod, norm1, w_in_b, rope, l, None, fw, 2 * cw)
        yf, yc = mixers(uf, uc, n_lat, lat_tables, l)
        cast = (w_out, w_ffn1, w_ffn3, w_ffn2) if l == 0 else ()
        if last:
            kv_ctx = _inproj(ctx_flat, mod, norm1, w_in_b, None, l, ctx_row, fw, 2 * cw, kv_only=True)
            (o,), casted = _attention(kv_ctx, (q, k, vt, kn), lam_vecs, subln, l, lam_init, False, cast)
        else:
            qc, kc, vtc, knc, ufc, ucc = _inproj(ctx_flat, mod, norm1, w_in_b, None, l, ctx_row, fw, 2 * cw)
            (oc, o), casted = _attention((qc, kc, vtc, knc), (q, k, vt, kn), lam_vecs, subln, l, lam_init, True, cast)
        if cast:
            w_out_b, w1_b, w3_b, w2_b = casted
        if not last:
            yfc, ycc = mixers(ufc, ucc, n_ctx, ctx_tables, l)
            ctx_flat = _out_ffn(ctx_flat, oc, yfc, ycc, mod, norm2, w_out_b, w1_b, w3_b, w2_b, fg, l, ctx_row,
                                final=False)
        x = _out_ffn(x, o, yf, yc, mod, norm2, w_out_b, w1_b, w3_b, w2_b, fg, l, None, final=last)
    return x
```

```python
import functools
import math

import numpy as np
import jax
import jax.numpy as jnp
from jax import lax
from jax.experimental import pallas as pl
from jax.experimental.pallas import tpu as pltpu

F32 = jnp.float32
BF16 = jnp.bfloat16

GRID_W = 64
HEADS = 4
QK_DIM = 64
V_DIM = 2 * QK_DIM
FOURIER_GROUPS = 4
CONV_GROUPS = 4
CONV_K = 31
ROPE_BASE = 10000.0
EPS = 1e-6
LOG2_E = math.log2(math.e)
BOUND_SLACK = 1.0 + 2.0 ** -6
MIN_SOFTMAX_DENOMINATOR = 2.0 ** -90

LANES = 128
SUBLANES = 8
TM = 256
ROW_SUBTILES = 2
INPROJ_SUBTILES = 4
ATT_KC = 256
CONV_HALO = 16
CONV_CHUNK = 128
ADA_TN = 1024
DFT_LO = 64
V7X_VMEM_BYTES = 64 * 1024 * 1024
VMEM_REQUEST_BYTES = V7X_VMEM_BYTES * 15 // 16


def _nbytes(shape, dtype):
    return math.prod(shape) * jnp.dtype(dtype).itemsize


def _block_bytes(specs, arrays):
    return sum(_nbytes([s for s in spec.block_shape if s is not None], arr.dtype) for spec, arr in zip(specs, arrays))


def _params(sem, *, resident=0, streamed=0, scratch=0, temporaries=0):
    need = resident + 2 * streamed + scratch + temporaries
    assert need <= VMEM_REQUEST_BYTES, f"VMEM estimate {need} exceeds the request {VMEM_REQUEST_BYTES}"
    return pltpu.CompilerParams(dimension_semantics=sem, vmem_limit_bytes=VMEM_REQUEST_BYTES)


def _resident(arr, layer=None, depth=None):
    if layer is None:
        idx = (0,) * arr.ndim
        return pl.BlockSpec(arr.shape, lambda *_: idx, pipeline_mode=pl.Buffered(1))
    idx = (layer,) + (0,) * (arr.ndim - 1)
    if depth is not None:
        return pl.BlockSpec((arr.shape[0] // depth, arr.shape[1]), lambda *_: idx, pipeline_mode=pl.Buffered(1))
    return pl.BlockSpec((None,) + arr.shape[1:], lambda *_: idx, pipeline_mode=pl.Buffered(1))


def _segment_spec(arr, n):
    if arr.shape[0] == 1:
        return pl.BlockSpec((None, n, arr.shape[-1]), lambda i: (0, i, 0))
    return pl.BlockSpec((None, n, arr.shape[-1]), lambda i: (i, 0, 0))


def _mod_spec(mod, layer, fixed_row):
    if fixed_row is None:
        return pl.BlockSpec((None, None) + mod.shape[2:], lambda i, j: (layer, i, 0, 0))
    return pl.BlockSpec((None, None) + mod.shape[2:], lambda i, j: (layer, fixed_row, 0, 0))


def _cast_slabs(weights, grid):
    steps = grid[0] * grid[1]
    operands, specs, shapes = [], [], []
    for wt in weights:
        flat = wt.reshape(-1, wt.shape[-1])
        slab = flat.shape[0] // steps
        assert slab * steps == flat.shape[0] and slab % (2 * SUBLANES) == 0
        operands.append(flat)
        specs.append(pl.BlockSpec((slab, flat.shape[1]), lambda i, j: (i * grid[1] + j, 0)))
        shapes.append(jax.ShapeDtypeStruct(flat.shape, BF16))
    return operands, specs, shapes


def _sigmoid(x):
    return 0.5 * jnp.tanh(0.5 * x) + 0.5


def _rms(x):
    return x * lax.rsqrt(jnp.mean(x * x, axis=-1, keepdims=True) + EPS)


def _dot(a, b):
    return jnp.dot(a, b, preferred_element_type=F32)


def _ada_kernel(c_ref, w_ref, b_ref, o_ref):
    c = c_ref[...]
    s = (c * _sigmoid(c)).astype(BF16)
    o_ref[...] = _dot(s, w_ref[...].astype(BF16)) + b_ref[...]


def _ada(c_rows, w_ada, b_ada):
    depth, d, n = w_ada.shape
    rows = c_rows.shape[0]
    return pl.pallas_call(
        _ada_kernel,
        grid=(depth, n // ADA_TN),
        in_specs=[
            pl.BlockSpec((rows, d), lambda l, j: (0, 0)),
            pl.BlockSpec((None, d, ADA_TN), lambda l, j: (l, 0, j)),
            pl.BlockSpec((None, 1, ADA_TN), lambda l, j: (l, 0, j)),
        ],
        out_specs=pl.BlockSpec((None, rows, ADA_TN), lambda l, j: (l, 0, j)),
        out_shape=jax.ShapeDtypeStruct((depth, rows, n), F32),
        compiler_params=_params(
            ("arbitrary", "arbitrary"),
            resident=_nbytes(c_rows.shape, F32),
            streamed=_nbytes((d + 1 + rows, ADA_TN), F32),
            temporaries=_nbytes((d, ADA_TN), BF16) + _nbytes((rows, ADA_TN), F32)),
        name="ada",
    )(c_rows, w_ada, b_ada.reshape(depth, 1, n))


def _inproj_kernel(*refs, use_rope, kv_only, n_sub):
    x_ref, mod_ref, g_ref, w_ref = refs[:4]
    refs = refs[4:]
    if use_rope:
        cos_ref, sin_ref = refs[:2]
        refs = refs[2:]
    hw = HEADS * LANES
    lane = lax.broadcasted_iota(jnp.int32, (TM, LANES), 1)
    first_half = (lane & (QK_DIM // 4)) == 0
    row = lax.broadcasted_iota(jnp.int32, (LANES, TM), 0)
    zero = jnp.zeros((LANES, TM), F32)
    dim = lax.broadcasted_iota(jnp.int32, (LANES, LANES), 0)
    col = lax.broadcasted_iota(jnp.int32, (LANES, LANES), 1)
    map_sum = jnp.where(col == jnp.where(dim < QK_DIM, 0, 1), 1.0, 0.0).astype(BF16)

    for u in range(n_sub):
        rows = slice(u * TM, (u + 1) * TM)
        h = (_rms(x_ref[rows, :]) * g_ref[...] * (1.0 + mod_ref[1:2, :]) + mod_ref[0:1, :]).astype(BF16)

        def rope(t):
            if not use_rope:
                return t
            partner = jnp.where(first_half, pltpu.roll(t, LANES - QK_DIM // 4, 1), pltpu.roll(t, QK_DIM // 4, 1))
            return t * cos_ref[rows, :] + partner * sin_ref[rows, :]

        def put_keys(j, kf):
            k_ref[j, rows, :] = kf.astype(BF16)
            norms = _dot((kf * kf).astype(BF16), map_sum)
            kn_ref[j, u] = jnp.max(norms.reshape(TM // SUBLANES, SUBLANES, LANES), axis=0)

        if kv_only:
            k_ref, vt_ref, kn_ref = refs
            r = _dot(h, w_ref[:, hw:3 * hw])
            for j in range(HEADS):
                put_keys(j, rope(r[:, j * LANES:(j + 1) * LANES]))
                vt_ref[j, :, rows] = r[:, hw + j * LANES:hw + (j + 1) * LANES].T.astype(BF16)
            continue

        q_ref, k_ref, vt_ref, kn_ref, uf_ref, uc_ref = refs
        r = _dot(h, w_ref[...])
        for j in range(HEADS):
            qt = (rope(r[:, j * LANES:(j + 1) * LANES]) * (QK_DIM ** -0.5 * LOG2_E)).T
            for mp in range(2):
                sel = (row < QK_DIM) == (mp == 0)
                qm = qt[mp * QK_DIM:(mp + 1) * QK_DIM, :]
                qn = jnp.sqrt(jnp.sum(qm * qm, axis=0, keepdims=True)) * BOUND_SLACK
                q_ref[j, u, mp, 0:LANES, :] = jnp.where(sel, qt, zero).astype(BF16)
                q_ref[j, u, mp, LANES:2 * LANES, :] = jnp.where(row == 0, qn, zero).astype(BF16)
            put_keys(j, rope(r[:, hw + j * LANES:hw + (j + 1) * LANES]))
            vt_ref[j, :, rows] = r[:, 2 * hw + j * LANES:2 * hw + (j + 1) * LANES].T.astype(BF16)
        fw = uf_ref.shape[-1]
        uf_ref[rows, :] = r[:, 3 * hw:3 * hw + fw].astype(BF16)
        uc_ref[rows, :] = r[:, 3 * hw + fw:]


def _inproj(x, mod, g, w, rope_tables, layer, fixed_row, fw, cw2, kv_only=False):
    b, n, d = x.shape
    n_sub = INPROJ_SUBTILES if n % (INPROJ_SUBTILES * TM) == 0 else ROW_SUBTILES
    bm = n_sub * TM
    assert n % bm == 0
    use_rope = rope_tables is not None
    in_specs = [
        pl.BlockSpec((None, bm, d), lambda i, j: (i, j, 0)),
        _mod_spec(mod, layer, fixed_row),
        _resident(g, layer),
        _resident(w, layer),
    ]
    operands = [x, mod, g, w]
    if use_rope:
        in_specs += [pl.BlockSpec((bm, LANES), lambda i, j: (j, 0))] * 2
        operands += list(rope_tables)
    out_specs = [
        pl.BlockSpec((None, HEADS, n_sub, 2, 2 * LANES, TM), lambda i, j: (i, 0, j, 0, 0, 0)),
        pl.BlockSpec((None, HEADS, bm, LANES), lambda i, j: (i, 0, j, 0)),
        pl.BlockSpec((None, HEADS, V_DIM, bm), lambda i, j: (i, 0, 0, j)),
        pl.BlockSpec((None, HEADS, n_sub, SUBLANES, LANES), lambda i, j: (i, 0, j, 0, 0)),
        pl.BlockSpec((None, bm, fw), lambda i, j: (i, j, 0)),
        pl.BlockSpec((None, bm, cw2), lambda i, j: (i, j, 0)),
    ]
    out_shape = [
        jax.ShapeDtypeStruct((b, HEADS, n // TM, 2, 2 * LANES, TM), BF16),
        jax.ShapeDtypeStruct((b, HEADS, n, LANES), BF16),
        jax.ShapeDtypeStruct((b, HEADS, V_DIM, n), BF16),
        jax.ShapeDtypeStruct((b, HEADS, n // TM, SUBLANES, LANES), F32),
        jax.ShapeDtypeStruct((b, n, fw), BF16),
        jax.ShapeDtypeStruct((b, n, cw2), F32),
    ]
    if kv_only:
        out_specs, out_shape = out_specs[1:4], out_shape[1:4]
    return pl.pallas_call(
        functools.partial(_inproj_kernel, use_rope=use_rope, kv_only=kv_only, n_sub=n_sub),
        grid=(b, n // bm),
        in_specs=in_specs,
        out_specs=out_specs,
        out_shape=out_shape,
        compiler_params=_params(
            ("arbitrary", "arbitrary"),
            resident=_block_bytes(in_specs[2:4], operands[2:4]),
            streamed=_block_bytes(in_specs[:2] + in_specs[4:] + out_specs, operands[:2] + operands[4:] + out_shape),
            temporaries=2 * n_sub * _nbytes((TM, w.shape[-1]), F32)),
        name="inproj",
    )(*operands)


def _attn_kernel(*refs, lam_init, with_ctx, bounded, n_cast):
    lq1_ref, lk1_ref, lq2_ref, lk2_ref, g_ref, kc_ref, vtc_ref, knc_ref, ql_ref, kl_ref, vtl_ref, knl_ref = refs[:12]
    refs = list(refs[12:])
    s_ref = refs.pop()
    qc_ref = refs.pop(0) if with_ctx else None
    cast_src = [refs.pop(0) for _ in range(n_cast)]
    oc_ref = refs.pop(0) if with_ctx else None
    ol_ref = refs.pop(0)
    lmin_ref = refs.pop(0) if bounded else None
    for src_ref, dst_ref in zip(cast_src, refs):
        dst_ref[...] = src_ref[...].astype(BF16)
    lam = (jnp.exp(jnp.sum(lq1_ref[...] * lk1_ref[...], axis=-1, keepdims=True))
           - jnp.exp(jnp.sum(lq2_ref[...] * lk2_ref[...], axis=-1, keepdims=True)) + lam_init)

    def fold(x):
        return x.reshape(x.shape[0] // SUBLANES, SUBLANES, x.shape[1])

    def finish(o_ref, t, ot, l1):
        ot = ot * (1.0 / l1)
        ms = jnp.mean(ot * ot, axis=0, keepdims=True)
        on = (ot * lax.rsqrt(ms + EPS)).T * g_ref[...] * (1.0 - lam_init)
        o_ref[t * TM:(t + 1) * TM, :] = on.astype(BF16)

    def key_chunks(key_refs):
        return [(k_ref, vt_ref, c) for k_ref, vt_ref, _ in key_refs for c in range(0, k_ref.shape[0], ATT_KC)]

    def pipeline_bounded(q_ref, o_ref, key_refs):
        chunks = key_chunks(key_refs)
        n_tiles = q_ref.shape[0]
        lane = lax.broadcasted_iota(jnp.int32, (ATT_KC, LANES), 1)
        key_cols = []
        ksq = functools.reduce(jnp.maximum, [jnp.max(kn_ref[...], axis=0) for _, _, kn_ref in key_refs])
        kb = jnp.sqrt(jnp.max(ksq, axis=0, keepdims=True)) * BOUND_SLACK
        for mp in range(2):
            key_cols.append(jnp.where(lane == 0, -kb[:, mp:mp + 1], 0.0).astype(BF16))
        sums = {}
        lmin = None
        for u in range(n_tiles + 1):
            ta, tc = u, u - 1
            do_a, do_c = ta < n_tiles, 0 <= tc < n_tiles
            if do_a:
                qts = [q_ref[ta, mp] for mp in range(2)]
                l8 = [jnp.zeros((SUBLANES, TM), F32) for _ in range(2)]
            if do_c:
                lc = sums.pop(tc)
                rho = lam * lc[0] / lc[1]
                ot = None
            for ci, (k_ref, vt_ref, c) in enumerate(chunks):
                rows = slice(ci * ATT_KC, (ci + 1) * ATT_KC)
                if do_a:
                    for mp in range(2):
                        keys = jnp.concatenate([k_ref[c:c + ATT_KC, :], key_cols[mp]], axis=1)
                        e = jnp.exp2(_dot(keys, qts[mp]))
                        s_ref[ta % 2, mp, rows, :] = e
                        l8[mp] = l8[mp] + jnp.sum(fold(e), axis=0)
                if do_c:
                    p = (s_ref[tc % 2, 0, rows, :] - rho * s_ref[tc % 2, 1, rows, :]).astype(BF16)
                    part = _dot(vt_ref[:, c:c + ATT_KC], p)
                    ot = part if ot is None else ot + part
            if do_a:
                sums[ta] = [jnp.sum(l, axis=0, keepdims=True) for l in l8]
                low = jnp.minimum(sums[ta][0], sums[ta][1])
                lmin = low if lmin is None else jnp.minimum(lmin, low)
            if do_c:
                finish(o_ref, tc, ot, lc[0])
        return jnp.min(lmin, axis=1, keepdims=True)

    def pipeline(q_ref, o_ref, key_refs):
        chunks = key_chunks(key_refs)
        n_tiles = q_ref.shape[0]
        maxima, sums = {}, {}
        for u in range(n_tiles + 2):
            ta, tb, tc = u, u - 1, u - 2
            do_a, do_b, do_c = ta < n_tiles, 0 <= tb < n_tiles, 0 <= tc < n_tiles
            if do_a:
                qts = [q_ref[ta, mp, 0:LANES, :] for mp in range(2)]
                m8 = [None, None]
            if do_b:
                mb = maxima.pop(tb)
                l8 = [jnp.zeros((SUBLANES, TM), F32) for _ in range(2)]
            if do_c:
                lc = sums.pop(tc)
                rho = lam * lc[0] / lc[1]
                ot = None
            for ci, (k_ref, vt_ref, c) in enumerate(chunks):
                rows = slice(ci * ATT_KC, (ci + 1) * ATT_KC)
                if do_a:
                    for mp in range(2):
                        s = _dot(k_ref[c:c + ATT_KC, :], qts[mp])
                        s_ref[ta % 3, mp, rows, :] = s
                        cm = jnp.max(fold(s), axis=0)
                        m8[mp] = cm if m8[mp] is None else jnp.maximum(m8[mp], cm)
                if do_b:
                    for mp in range(2):
                        e = jnp.exp2(s_ref[tb % 3, mp, rows, :] - mb[mp])
                        s_ref[tb % 3, mp, rows, :] = e
                        l8[mp] = l8[mp] + jnp.sum(fold(e), axis=0)
                if do_c:
                    p = (s_ref[tc % 3, 0, rows, :] - rho * s_ref[tc % 3, 1, rows, :]).astype(BF16)
                    part = _dot(vt_ref[:, c:c + ATT_KC], p)
                    ot = part if ot is None else ot + part
            if do_a:
                maxima[ta] = [jnp.max(m, axis=0, keepdims=True) for m in m8]
            if do_b:
                sums[tb] = [jnp.sum(l, axis=0, keepdims=True) for l in l8]
            if do_c:
                finish(o_ref, tc, ot, lc[0])

    run = pipeline_bounded if bounded else pipeline
    ctx_keys, lat_keys = (kc_ref, vtc_ref, knc_ref), (kl_ref, vtl_ref, knl_ref)
    lmin = run(ql_ref, ol_ref, [ctx_keys, lat_keys])
    if with_ctx:
        lmin_ctx = run(qc_ref, oc_ref, [ctx_keys])
    if bounded:
        if with_ctx:
            lmin = jnp.minimum(lmin, lmin_ctx)
        lmin_ref[...] = jnp.broadcast_to(lmin, lmin_ref.shape)


def _attention_call(qkv_ctx, qkv_lat, lam_vecs, g, layer, lam_init, with_ctx, bounded, cast=()):
    q_lat, k_lat, vt_lat, kn_lat = qkv_lat
    k_ctx, vt_ctx, kn_ctx = qkv_ctx[-3:]
    b, h, n_lat, _ = k_lat.shape
    n_ctx = k_ctx.shape[2] // b
    lat_tiles, ctx_tiles = n_lat // TM, n_ctx // TM

    def tiles_spec(tiles, tail, ctx):
        zeros = (0,) * len(tail)
        if ctx:
            return pl.BlockSpec((None, None, tiles) + tail, lambda i, j: (0, j, i) + zeros)
        return pl.BlockSpec((None, None, tiles) + tail, lambda i, j: (i, j, 0) + zeros)

    q_tile, kn_tile = (2, 2 * LANES, TM), (SUBLANES, LANES)
    in_specs = [_resident(v, layer) for v in lam_vecs] + [
        _resident(g, layer),
        pl.BlockSpec((None, None, n_ctx, LANES), lambda i, j: (0, j, i, 0)),
        pl.BlockSpec((None, None, V_DIM, n_ctx), lambda i, j: (0, j, 0, i)),
        tiles_spec(ctx_tiles, kn_tile, True),
        tiles_spec(lat_tiles, q_tile, False),
        pl.BlockSpec((None, None, n_lat, LANES), lambda i, j: (i, j, 0, 0)),
        pl.BlockSpec((None, None, V_DIM, n_lat), lambda i, j: (i, j, 0, 0)),
        tiles_spec(lat_tiles, kn_tile, False),
    ]
    operands = list(lam_vecs) + [g, k_ctx, vt_ctx, kn_ctx, q_lat, k_lat, vt_lat, kn_lat]
    out_specs = [pl.BlockSpec((None, n_lat, V_DIM), lambda i, j: (i, 0, j))]
    out_shape = [jax.ShapeDtypeStruct((b, n_lat, h * V_DIM), BF16)]
    if with_ctx:
        in_specs.append(tiles_spec(ctx_tiles, q_tile, True))
        operands.append(qkv_ctx[0])
        out_specs.insert(0, pl.BlockSpec((None, n_ctx, V_DIM), lambda i, j: (0, i, j)))
        out_shape.insert(0, jax.ShapeDtypeStruct((1, b * n_ctx, h * V_DIM), BF16))
    if bounded:
        out_specs.append(pl.BlockSpec((None, None, SUBLANES, LANES), lambda i, j: (i, j, 0, 0)))
        out_shape.append(jax.ShapeDtypeStruct((b, h, SUBLANES, LANES), F32))
    cast_operands, cast_specs, cast_shapes = _cast_slabs(cast, (b, h))
    in_specs, operands = in_specs + cast_specs, operands + cast_operands
    scores_shape = (2 if bounded else 3, 2, n_ctx + n_lat, TM)
    outs = pl.pallas_call(
        functools.partial(_attn_kernel, lam_init=lam_init, with_ctx=with_ctx, bounded=bounded, n_cast=len(cast)),
        grid=(b, h),
        in_specs=in_specs,
        out_specs=out_specs + cast_specs,
        out_shape=out_shape + cast_shapes,
        scratch_shapes=[pltpu.VMEM(scores_shape, F32)],
        compiler_params=_params(
            ("arbitrary", "arbitrary"),
            streamed=_block_bytes(in_specs + out_specs + cast_specs, operands + out_shape + cast_shapes),
            scratch=_nbytes(scores_shape, F32),
            temporaries=8 * _nbytes((ATT_KC, TM), F32)),
        name="attention_bounded" if bounded else "attention_exact",
    )(*operands)
    n_main = len(out_shape)
    return list(outs[:n_main]), list(outs[n_main:])


def _attention(qkv_ctx, qkv_lat, lam_vecs, g, layer, lam_init, with_ctx, cast=()):
    args = (qkv_ctx, qkv_lat, lam_vecs, g, layer, lam_init, with_ctx)
    (*outs, lmin), casted = _attention_call(*args, bounded=True, cast=cast)
    safe = jnp.min(lmin) >= MIN_SOFTMAX_DENOMINATOR
    outs = lax.cond(safe, lambda: tuple(outs), lambda: tuple(_attention_call(*args, bounded=False)[0]))
    return list(outs), casted


def _fourier_kernel(u_ref, chan_c_ref, chan_s_ref, wf_ref, pos_c_ref, pos_s_ref, mid_ref, flip_ref, y_ref):
    n, width = u_ref.shape
    half = n // 2
    scale = (n * width // FOURIER_GROUPS) ** -0.5
    u = u_ref[...]
    a = _dot(pos_c_ref[...], u).astype(BF16)
    b = _dot(pos_s_ref[...], u).astype(BF16)
    p = _dot(a, chan_c_ref[...])
    q = _dot(b, chan_s_ref[...])
    y_ref[0:half, :] = _dot(((p - q) * scale).astype(BF16), wf_ref[...]).astype(BF16)
    mirrored = _dot(flip_ref[...], ((p + q) * scale).astype(BF16))
    mid = _dot(_dot(mid_ref[...], u).astype(BF16), chan_c_ref[...]) * scale
    rows = mid.shape[0]
    upper = jnp.concatenate([mirrored[0:rows] + mid, mirrored[rows:]], axis=0)
    y_ref[half:, :] = _dot(upper.astype(BF16), wf_ref[...]).astype(BF16)


def _fourier(uf, n, chan_c, chan_s, wf_bd, layer, pos_tables):
    fw = uf.shape[-1]
    half = n // 2
    idx = jnp.arange(half)
    flip = (idx[:, None] + idx[None, :] == half).astype(BF16)
    mid = np.zeros((SUBLANES, n), np.float32)
    mid[0] = 1.0 - 2.0 * (np.arange(n) % 2)
    mid = jnp.asarray(mid).astype(BF16)
    row_spec = _segment_spec(uf, n)
    operands = [uf, chan_c, chan_s, wf_bd, *pos_tables, mid, flip]
    in_specs = [row_spec, _resident(chan_c), _resident(chan_s), _resident(wf_bd, layer)] + [
        _resident(t) for t in (*pos_tables, mid, flip)]
    return pl.pallas_call(
        _fourier_kernel,
        grid=(uf.shape[0] * uf.shape[1] // n,),
        in_specs=in_specs,
        out_specs=row_spec,
        out_shape=jax.ShapeDtypeStruct(uf.shape, BF16),
        compiler_params=_params(
            ("arbitrary",),
            resident=_block_bytes(in_specs[1:], operands[1:]),
            streamed=2 * _nbytes((n, fw), BF16),
            temporaries=8 * _nbytes((half, fw), F32)),
        name="fourier",
    )(*operands)


def _split_dot(x, m):
    hi = x.astype(BF16)
    lo = (x - hi.astype(F32)).astype(BF16)
    return _dot(hi, m) + _dot(lo, m)


def _conv_kernel(u_ref, w_ref, b_ref, lg_ref, lb_ref, avg_ref, pw_ref, y_ref, zpad_ref, acc_a_ref, acc_b_ref):
    n, cw = y_ref.shape
    halo = jnp.zeros((CONV_HALO, cw), F32)

    def taps(i, acc_ref, slot):
        r0 = pl.multiple_of(i * CONV_CHUNK, CONV_CHUNK)
        for lo in range(0, cw, LANES):
            win = zpad_ref[pl.ds(r0, CONV_CHUNK + 2 * CONV_HALO), lo:lo + LANES]
            part = jnp.zeros((CONV_CHUNK, LANES), F32) + b_ref[:, lo:lo + LANES]
            for shift in range(SUBLANES):
                rolled = win if shift == 0 else pltpu.roll(win, win.shape[0] - shift, 0)
                for aligned in range(0, 2 * CONV_HALO, SUBLANES):
                    tap = aligned + shift - (CONV_HALO - CONV_K // 2)
                    if 0 <= tap < CONV_K:
                        part = part + rolled[aligned:aligned + CONV_CHUNK] * w_ref[tap:tap + 1, lo:lo + LANES]
            acc_ref[slot, :, lo:lo + LANES] = part

    def project(acc_ref, slot):
        acc = acc_ref[slot]
        mu = _split_dot(acc, avg_ref[...])
        dev = acc - mu
        var = _split_dot(dev * dev, avg_ref[...])
        zn = dev * lax.rsqrt(var + EPS) * lg_ref[...] + lb_ref[...]
        act = (zn * _sigmoid(zn)).astype(BF16)
        return _dot(act, pw_ref[...]).astype(BF16)

    def round_(first_tap, tap_ref, ready_ref):
        if first_tap is not None:
            taps(first_tap, tap_ref, 0)
            taps(first_tap + 1, tap_ref, 1)
        return [project(ready_ref, 0), project(ready_ref, 1)]

    def store(first_chunk, ys):
        r0 = pl.multiple_of(first_chunk * CONV_CHUNK, 2 * CONV_CHUNK)
        y_ref[pl.ds(r0, len(ys) * CONV_CHUNK), :] = jnp.concatenate(ys, axis=0)

    zpad_ref[0:CONV_HALO, :] = halo
    zpad_ref[CONV_HALO:CONV_HALO + n, :] = u_ref[:, 0:cw] * _sigmoid(u_ref[:, cw:2 * cw])
    zpad_ref[CONV_HALO + n:2 * CONV_HALO + n, :] = halo

    chunks = n // CONV_CHUNK
    assert chunks == 2 or chunks % 4 == 0

    def step(j, carry):
        c0 = 4 * j
        ys = round_(c0 + 2, acc_b_ref, acc_a_ref) + round_(c0 + 4, acc_a_ref, acc_b_ref)
        store(c0, ys)
        return carry

    taps(0, acc_a_ref, 0)
    taps(1, acc_a_ref, 1)
    if chunks == 2:
        store(0, round_(None, None, acc_a_ref))
    else:
        lax.fori_loop(0, chunks // 4 - 1, step, 0)
        store(chunks - 4, round_(chunks - 2, acc_b_ref, acc_a_ref) + round_(None, None, acc_b_ref))


def _conv(uc, n, conv_w, conv_b, ln_g, ln_b, avg, w_pw, layer):
    cw2 = uc.shape[-1]
    cw = cw2 // 2
    y_shape = jax.ShapeDtypeStruct(uc.shape[:2] + (cw,), BF16)
    operands = [uc, conv_w, conv_b, ln_g, ln_b, avg, w_pw]
    in_specs = [
        _segment_spec(uc, n),
        _resident(conv_w, layer), _resident(conv_b, layer), _resident(ln_g, layer), _resident(ln_b, layer),
        _resident(avg), _resident(w_pw, layer),
    ]
    scratch = [(n + 2 * CONV_HALO, cw), (2, CONV_CHUNK, cw), (2, CONV_CHUNK, cw)]
    return pl.pallas_call(
        _conv_kernel,
        grid=(uc.shape[0] * uc.shape[1] // n,),
        in_specs=in_specs,
        out_specs=_segment_spec(y_shape, n),
        out_shape=y_shape,
        scratch_shapes=[pltpu.VMEM(s, F32) for s in scratch],
        compiler_params=_params(
            ("arbitrary",),
            resident=_block_bytes(in_specs[1:], operands[1:]),
            streamed=_nbytes((n, cw2), F32) + _nbytes((n, cw), BF16),
            scratch=sum(_nbytes(s, F32) for s in scratch),
            temporaries=3 * _nbytes((n, cw), F32) + 16 * _nbytes((CONV_CHUNK, cw), F32)),
        name="conv",
    )(*operands)


def _ffn_kernel(x_ref, o_ref, yf_ref, yc_ref, mod_ref, g_ref, wout_ref, w1_ref, w3_ref, w2_ref, fg_ref, out_ref,
                *, final, n_sub):
    aw = o_ref.shape[-1]
    fw = yf_ref.shape[-1]
    for u in range(n_sub):
        rows = slice(u * TM, (u + 1) * TM)
        y = (_dot(o_ref[rows, :], wout_ref[0:aw, :]) + _dot(yf_ref[rows, :], wout_ref[aw:aw + fw, :])
             + _dot(yc_ref[rows, :], wout_ref[aw + fw:, :]))
        x1 = x_ref[rows, :] + mod_ref[2:3, :] * y
        h = (_rms(x1) * g_ref[...] * (1.0 + mod_ref[4:5, :]) + mod_ref[3:4, :]).astype(BF16)
        a = _dot(h, w1_ref[...])
        gated = (a * _sigmoid(a) * _dot(h, w3_ref[...])).astype(BF16)
        x2 = x1 + mod_ref[5:6, :] * _dot(gated, w2_ref[...])
        if final:
            x2 = _rms(x2) * fg_ref[...]
        out_ref[rows, :] = x2


def _out_ffn(x, o, yf, yc, mod, g2, w_out, w1, w3, w2, final_g, layer, fixed_row, final):
    b, n, d = x.shape
    depth = g2.shape[0]
    n_sub = ROW_SUBTILES
    bm = n_sub * TM
    assert n % bm == 0

    def rows(width):
        return pl.BlockSpec((None, bm, width), lambda i, j: (i, j, 0))

    operands = [x, o, yf, yc, mod, g2, w_out, w1, w3, w2, final_g]
    in_specs = [
        rows(d), rows(o.shape[-1]), rows(yf.shape[-1]), rows(yc.shape[-1]),
        _mod_spec(mod, layer, fixed_row),
        _resident(g2, layer), _resident(w_out, layer, depth), _resident(w1, layer, depth),
        _resident(w3, layer, depth), _resident(w2, layer, depth), _resident(final_g),
    ]
    d_ff = w1.shape[-1]
    return pl.pallas_call(
        functools.partial(_ffn_kernel, final=final, n_sub=n_sub),
        grid=(b, n // bm),
        in_specs=in_specs,
        out_specs=rows(d),
        out_shape=jax.ShapeDtypeStruct((b, n, d), F32),
        compiler_params=_params(
            ("arbitrary", "arbitrary"),
            resident=_block_bytes(in_specs[5:], operands[5:]),
            streamed=_block_bytes(in_specs[:5], operands[:5]) + _nbytes((bm, d), F32),
            temporaries=n_sub * (4 * _nbytes((TM, d), F32) + 2 * _nbytes((TM, d_ff), F32) + _nbytes((TM, d_ff), BF16))),
        name="out_ffn",
    )(*operands)


def _rope_tables(n_lat):
    n_freq = QK_DIM // 4
    tok = np.arange(n_lat)
    inv_freq = np.float32(ROPE_BASE) ** (-np.arange(n_freq, dtype=np.float32) / np.float32(n_freq))
    ang_r = ((tok // GRID_W).astype(np.float32)[:, None] * inv_freq).astype(np.float64)
    ang_c = ((tok % GRID_W).astype(np.float32)[:, None] * inv_freq).astype(np.float64)
    cos = np.concatenate([np.cos(ang_r)] * 2 + [np.cos(ang_c)] * 2, axis=-1)
    sin = np.concatenate([-np.sin(ang_r), np.sin(ang_r), -np.sin(ang_c), np.sin(ang_c)], axis=-1)
    reps = LANES // QK_DIM
    return (jnp.asarray(np.tile(cos, (1, reps)), dtype=F32), jnp.asarray(np.tile(sin, (1, reps)), dtype=F32))


def _dft_angles(rows, n):
    return 2.0 * np.pi * ((np.asarray(rows, np.int64)[:, None] * np.arange(n, dtype=np.int64)[None, :]) % n) / n


def _dft_tables_small(n):
    ang = _dft_angles(np.arange(n), n)
    return np.cos(ang).astype(np.float32), np.sin(ang).astype(np.float32)


def _dft_tables(n):
    rows = n // 2
    if n <= 4 * DFT_LO:
        return [jnp.asarray(t[:rows]).astype(BF16) for t in _dft_tables_small(n)]
    hi = rows // DFT_LO
    ang_hi = _dft_angles(np.arange(hi) * DFT_LO, n)
    ang_lo = _dft_angles(np.arange(DFT_LO), n)
    ch, sh = (jnp.asarray(f(ang_hi), dtype=F32)[:, None, :] for f in (np.cos, np.sin))
    cl, sl = (jnp.asarray(f(ang_lo), dtype=F32)[None, :, :] for f in (np.cos, np.sin))
    return [(ch * cl - sh * sl).reshape(rows, n).astype(BF16), (sh * cl + ch * sl).reshape(rows, n).astype(BF16)]


def _block_diag(blocks):
    g, r, c = blocks.shape[-3:]
    eye = jnp.eye(g, dtype=blocks.dtype)
    out = eye[:, None, :, None] * blocks[..., :, :, None, :]
    return out.reshape(blocks.shape[:-3] + (g * r, g * c))


def _prep_inproj_kernel(w_ref, perm_ref, o_ref):
    wb = w_ref[...].astype(BF16)
    nqk = perm_ref.shape[0]
    o_ref[:, 0:nqk] = _dot(wb[:, 0:nqk], perm_ref[...]).astype(BF16)
    o_ref[:, nqk:] = wb[:, nqk:]


def _prep_inproj(w_in, qk_w):
    depth, d, n = w_in.shape
    nqk = 4 * qk_w
    dst = np.arange(nqk)
    pair, head, mp, dim = dst // (2 * qk_w), dst % (2 * qk_w) // LANES, dst % LANES // QK_DIM, dst % QK_DIM
    src = ((2 * pair + mp) * HEADS + head) * QK_DIM + dim
    perm = (jnp.arange(nqk)[:, None] == jnp.asarray(src)[None, :]).astype(BF16)
    w_spec = pl.BlockSpec((None, TM, n), lambda l, j: (l, j, 0))
    return pl.pallas_call(
        _prep_inproj_kernel,
        grid=(depth, d // TM),
        in_specs=[w_spec, _resident(perm)],
        out_specs=w_spec,
        out_shape=jax.ShapeDtypeStruct(w_in.shape, BF16),
        compiler_params=_params(
            ("arbitrary", "arbitrary"),
            resident=_nbytes(perm.shape, BF16),
            streamed=_nbytes((TM, n), F32) + _nbytes((TM, n), BF16),
            temporaries=_nbytes((TM, n), BF16) + _nbytes((TM, nqk), F32)),
        name="prep_inproj",
    )(w_in, perm)


def kernel(x, c, ctx, c_ctx, w_ada, b_ada, norm1_g, norm2_g, w_in, lam_q1, lam_k1, lam_q2, lam_k2, subln_g,
           w_fourier, conv_w, conv_b, conv_ln_g, conv_ln_b, w_conv_out, w_out, w_ffn1, w_ffn3, w_ffn2, final_g):
    b, n_lat, d = x.shape
    n_ctx = ctx.shape[1]
    depth = w_ada.shape[0]
    fw = w_fourier.shape[1] * w_fourier.shape[2]
    cw = conv_w.shape[-1]
    qk_w = HEADS * QK_DIM
    attn_w = HEADS * V_DIM
    assert n_lat % GRID_W == 0 and n_ctx % TM == 0
    assert w_in.shape[-1] == 4 * qk_w + attn_w + fw + 2 * cw

    pad = (-(b + 1)) % SUBLANES
    c_rows = jnp.concatenate([c, c_ctx[None, :], jnp.zeros((pad, d), c.dtype)], axis=0)
    mod = _ada(c_rows, w_ada, b_ada)
    mod = mod.reshape(depth, mod.shape[1], 6, d)
    ctx_row = b

    rope = _rope_tables(n_lat)
    cc, cs = _dft_tables_small(fw // FOURIER_GROUPS)
    eye = np.eye(FOURIER_GROUPS, dtype=np.float32)
    chan_c = jnp.asarray(np.kron(eye, cc)).astype(BF16)
    chan_s = jnp.asarray(np.kron(eye, cs)).astype(BF16)
    lat_tables = _dft_tables(n_lat)
    ctx_tables = _dft_tables(n_ctx)
    group = cw // CONV_GROUPS
    avg = jnp.asarray(np.kron(np.eye(CONV_GROUPS), np.full((group, group), 1.0 / group)), dtype=F32).astype(BF16)

    w_in_b = _prep_inproj(w_in, qk_w)
    wf_bd = _block_diag(w_fourier).astype(BF16)
    w_pw_b = w_conv_out.astype(BF16)
    w_out_b = w1_b = w3_b = w2_b = None

    def per_layer_rows(a):
        return a.reshape(depth, 1, a.shape[-1])

    norm1, norm2, subln = per_layer_rows(norm1_g), per_layer_rows(norm2_g), per_layer_rows(subln_g)
    lam_vecs = [per_layer_rows(a) for a in (lam_q1, lam_k1, lam_q2, lam_k2)]
    cb, lg, lb = per_layer_rows(conv_b), per_layer_rows(conv_ln_g), per_layer_rows(conv_ln_b)
    fg = final_g[None, :]

    def mixers(u_f, u_c, n, tables, l):
        yf = _fourier(u_f, n, chan_c, chan_s, wf_bd, l, tables)
        yc = _conv(u_c, n, conv_w, cb, lg, lb, avg, w_pw_b, l)
        return yf, yc

    ctx_flat = ctx.reshape(1, b * n_ctx, d)
    for l in range(depth):
        last = l == depth - 1
        lam_init = 0.8 - 0.6 * math.exp(-0.3 * l)
        q, k, vt, kn, uf, uc = _inproj(x, mod, norm1, w_in_b, rope, l, None, fw, 2 * cw)
        yf, yc = mixers(uf, uc, n_lat, lat_tables, l)
        cast = (w_out, w_ffn1, w_ffn3, w_ffn2) if l == 0 else ()
        if last:
            kv_ctx = _inproj(ctx_flat, mod, norm1, w_in_b, None, l, ctx_row, fw, 2 * cw, kv_only=True)
            (o,), casted = _attention(kv_ctx, (q, k, vt, kn), lam_vecs, subln, l, lam_init, False, cast)
        else:
            qc, kc, vtc, knc, ufc, ucc = _inproj(ctx_flat, mod, norm1, w_in_b, None, l, ctx_row, fw, 2 * cw)
            (oc, o), casted = _attention((qc, kc, vtc, knc), (q, k, vt, kn), lam_vecs, subln, l, lam_init, True, cast)
        if cast:
            w_out_b, w1_b, w3_b, w2_b = casted
        if not last:
            yfc, ycc = mixers(ufc, ucc, n_ctx, ctx_tables, l)
            ctx_flat = _out_ffn(ctx_flat, oc, yfc, ycc, mod, norm2, w_out_b, w1_b, w3_b, w2_b, fg, l, ctx_row,
                                final=False)
        x = _out_ffn(x, o, yf, yc, mod, norm2, w_out_b, w1_b, w3_b, w2_b, fg, l, None, final=last)
    return x
```
